```python
import math
import numpy as np
import jax
import jax.numpy as jnp
from jax import lax

D_MODEL = 2048
BATCH = 16
SEQ = 2048
DEPTH = 4
DEC_BATCH = 1
DEC_SEQ = 8192
PAST_LEN = 128

HEAD_DIM = 128
N_HEADS = D_MODEL // HEAD_DIM
H_GROUP = N_HEADS // 2
GROUP_W = H_GROUP * HEAD_DIM
GRID_W = 64
NA_WIN_R = 8
NA_WIN_C = 16
NA_QCB = NA_WIN_C
NA_BAND = 2 * NA_WIN_C
DN_CONV_K = 5
DN_CHUNK = 64
HG_CHUNK = 32
MLA_Q_LORA = 512
MLA_KV_LORA = 512
MLA_NOPE = 128
MLA_ROPE = 64
MLA_V = 128
MLA_BLOCK_Q = 128
ROPE_BASE = 10000.0
D_FF = 5632
FFN_CONV_K = 3
N_EVEN = (DEPTH + 1) // 2
N_ODD = DEPTH // 2
EPS = 1e-6
AB_SPLIT = [GROUP_W, GROUP_W, GROUP_W, 3 * GROUP_W, GROUP_W, H_GROUP, H_GROUP, H_GROUP, H_GROUP]
CD_SPLIT = [GROUP_W, GROUP_W, GROUP_W, GROUP_W, GROUP_W, MLA_Q_LORA, MLA_KV_LORA, MLA_ROPE]
IN_AB = sum(AB_SPLIT)
IN_CD = sum(CD_SPLIT)

kernel_name = "hybrid_bidir_encoder_na_gdn_hgrn2_mla"


def split_sizes(x, sizes):
    idx = np.cumsum(sizes)[:-1].tolist()
    return jnp.split(x, idx, axis=-1)


def rmsnorm(x, g):
    xf = x.astype(jnp.float32)
    y = xf * lax.rsqrt(jnp.mean(xf * xf, axis=-1, keepdims=True) + EPS)
    return (y * g.astype(jnp.float32)).astype(x.dtype)


def l2norm(x):
    xf = x.astype(jnp.float32)
    return xf * lax.rsqrt(jnp.sum(xf * xf, axis=-1, keepdims=True) + EPS)


def dwconv_centred(x, w):
    k_w = w.shape[0]
    pad = k_w // 2
    length = x.shape[1]
    xp = jnp.pad(x, ((0, 0), (pad, pad), (0, 0)))
    out = xp[:, 0:length] * w[0]
    for j in range(1, k_w):
        out = out + xp[:, j:j + length] * w[j]
    return out


def flip_seq(t):
    return jnp.flip(t, axis=1)


def _na_static():
    n_cb = GRID_W // NA_QCB
    qc = np.arange(GRID_W).reshape(n_cb, NA_QCB)
    band_start = np.clip(np.arange(n_cb) * NA_QCB - NA_WIN_C // 2, 0, GRID_W - NA_BAND)
    kc = band_start[:, None] + np.arange(NA_BAND)[None, :]
    cs = np.clip(qc - NA_WIN_C // 2, 0, GRID_W - NA_WIN_C)
    rel = kc[:, None, :] - cs[:, :, None]
    mask = (rel >= 0) & (rel < NA_WIN_C)
    dc = np.clip(kc[:, None, :] - qc[:, :, None] + NA_WIN_C - 1, 0, 2 * NA_WIN_C - 2)
    return kc, mask, dc


def neighbourhood_attention(q, k, v, rpb):
    bsz, length, h, d = q.shape
    rows = length // GRID_W
    wr = min(NA_WIN_R, rows)
    n_cb = GRID_W // NA_QCB
    kc, mask_np, dc = _na_static()
    mask = jnp.asarray(mask_np)[None, None, :, :, None, :]
    qg = q.reshape(bsz, rows, n_cb, NA_QCB, h, d) * (d ** -0.5)
    kg = k.reshape(bsz, rows, GRID_W, h, d)
    vg = v.reshape(bsz, rows, GRID_W, h, d)

    def one_row(r):
        rs = jnp.clip(r - wr // 2, 0, rows - wr)
        kb = lax.dynamic_slice_in_dim(kg, rs, wr, axis=1)[:, :, kc]
        vb = lax.dynamic_slice_in_dim(vg, rs, wr, axis=1)[:, :, kc]
        qr = lax.dynamic_index_in_dim(qg, r, axis=1, keepdims=False)
        s = jnp.einsum('bjuhd,bwjmhd->bhjuwm', qr, kb).astype(jnp.float32)
        dr = rs + jnp.arange(wr) - r + NA_WIN_R - 1
        bias = rpb[:, dr][:, :, dc].transpose(0, 2, 3, 1, 4)
        s = jnp.where(mask, s + bias.astype(jnp.float32)[None], -jnp.inf)
        p = jax.nn.softmax(s.reshape(bsz, h, n_cb, NA_QCB, wr * NA_BAND), axis=-1).reshape(s.shape)
        o = jnp.einsum('bhjuwm,bwjmhd->bjuhd', p.astype(v.dtype), vb)
        return o.reshape(bsz, GRID_W, h, d)

    out = lax.map(one_row, jnp.arange(rows))
    return out.transpose(1, 0, 2, 3, 4).reshape(bsz, length, h * d)


def gated_delta_rule(q, k, v, g, beta):
    bsz, length, h, dk = q.shape
    dv = v.shape[-1]
    c = DN_CHUNK
    n = length // c
    f32 = jnp.float32

    def chunk(x):
        x = x.astype(f32).reshape((bsz, n, c, h) + x.shape[3:])
        return jnp.moveaxis(x, 3, 1)

    q, k, v, g, beta = chunk(q), chunk(k), chunk(v), chunk(g), chunk(beta)
    q = q * (dk ** -0.5)
    gc = jnp.cumsum(g, axis=-1)
    tri = jnp.tril(jnp.ones((c, c), dtype=bool))
    strict = jnp.tril(jnp.ones((c, c), dtype=bool), -1)
    gam = jnp.exp(jnp.where(tri, gc[..., :, None] - gc[..., None, :], -jnp.inf))
    kb = k * beta[..., None]
    a = jnp.where(strict, jnp.einsum('bhnid,bhnjd->bhnij', kb, k) * gam, 0.0)
    p = -a
    t = jnp.eye(c, dtype=f32) + p
    for _ in range(int(math.log2(c)) - 1):
        p = p @ p
        t = t + t @ p
    u = t @ (v * beta[..., None])
    w = t @ (kb * jnp.exp(gc)[..., None])
    aqk = jnp.einsum('bhnid,bhnjd->bhnij', q, k) * gam
    qd = q * jnp.exp(gc)[..., None]
    kd = k * jnp.exp(gc[..., -1:] - gc)[..., None]
    dlast = jnp.exp(gc[..., -1])

    def step(s_state, xs):
        u_n, w_n, aqk_n, qd_n, kd_n, dl_n = xs
        v_new = u_n - w_n @ s_state
        o = qd_n @ s_state + aqk_n @ v_new
        s_state = s_state * dl_n[..., None, None] + jnp.swapaxes(kd_n, -1, -2) @ v_new
        return s_state, o

    xs = (jnp.moveaxis(u, 2, 0), jnp.moveaxis(w, 2, 0), jnp.moveaxis(aqk, 2, 0),
          jnp.moveaxis(qd, 2, 0), jnp.moveaxis(kd, 2, 0), jnp.moveaxis(dlast, 2, 0))
    s0 = jnp.zeros((bsz, h, dk, dv), f32)
    _, o = lax.scan(step, s0, xs)
    return o.transpose(1, 0, 3, 2, 4).reshape(bsz, length, h, dv)


def gated_deltanet(qkv, z, a_fw, a_bw, b_fw, b_bw, conv_w, a_log, dt_bias, norm_g):
    bsz, length, _ = qkv.shape
    qkv = jax.nn.silu(dwconv_centred(qkv, conv_w))
    q, k, v = jnp.split(qkv, 3, axis=-1)
    heads = lambda t: t.reshape(bsz, length, H_GROUP, HEAD_DIM)
    q, k, v = l2norm(heads(q)), l2norm(heads(k)), heads(v).astype(jnp.float32)

    def gates(a_raw, b_raw, d):
        g = -jnp.exp(a_log[d].astype(jnp.float32)) * jax.nn.softplus(
            a_raw.astype(jnp.float32) + dt_bias[d].astype(jnp.float32))
        return g, jax.nn.sigmoid(b_raw.astype(jnp.float32))

    g_fw, be_fw = gates(a_fw, b_fw, 0)
    g_bw, be_bw = gates(a_bw, b_bw, 1)
    o_fw = gated_delta_rule(q, k, v, g_fw, be_fw)
    o_bw = flip_seq(gated_delta_rule(flip_seq(q), flip_seq(k), flip_seq(v), flip_seq(g_bw), flip_seq(be_bw)))
    o = rmsnorm(o_fw + o_bw, norm_g) * jax.nn.silu(heads(z).astype(jnp.float32))
    return o.reshape(bsz, length, GROUP_W).astype(qkv.dtype)


def hgrn2_chunked(q, lf, i):
    bsz, length, h, dk = q.shape
    dv = i.shape[-1]
    c = HG_CHUNK
    n = length // c

    def chunk(x):
        return x.reshape(bsz, n, c, h, x.shape[-1]).transpose(1, 0, 3, 2, 4)

    q, lf, i = chunk(q), chunk(lf), chunk(i)
    k = -jnp.expm1(lf)
    bc = jnp.cumsum(lf, axis=3)
    tri = jnp.tril(jnp.ones((c, c), dtype=bool))[:, :, None]

    def step(s_state, xs):
        q_n, k_n, i_n, b_n = xs
        o = (q_n * jnp.exp(b_n)) @ s_state
        decay = jnp.exp(jnp.where(tri, b_n[..., :, None, :] - b_n[..., None, :, :], -jnp.inf))
        att = jnp.einsum('bhtd,bhsd,bhtsd->bhts', q_n, k_n, decay)
        o = o + att @ i_n
        blast = b_n[..., -1:, :]
        s_state = s_state * jnp.exp(blast)[..., 0, :, None] + jnp.einsum(
            'bhsd,bhsv->bhdv', k_n * jnp.exp(blast - b_n), i_n)
        return s_state, o

    s0 = jnp.zeros((bsz, h, dk, dv), jnp.float32)
    _, o = lax.scan(step, s0, (q, k, i, bc))
    return o.transpose(1, 0, 3, 2, 4).reshape(bsz, length, h, dv)


def hgrn2_mixer(q_raw, f_fw, f_bw, i_raw, g_raw, lb_fw, lb_bw, norm_g):
    bsz, length, _ = q_raw.shape
    heads = lambda t: t.astype(jnp.float32).reshape(bsz, length, H_GROUP, HEAD_DIM)
    q = jax.nn.silu(heads(q_raw))
    i = heads(i_raw)

    def log_forget(z, lb):
        lb = lb.reshape(H_GROUP, HEAD_DIM)
        return jnp.logaddexp(jnp.log(lb), jnp.log1p(-lb) + jax.nn.log_sigmoid(heads(z)))

    o_fw = hgrn2_chunked(q, log_forget(f_fw, lb_fw), i)
    o_bw = flip_seq(hgrn2_chunked(flip_seq(q), flip_seq(log_forget(f_bw, lb_bw)), flip_seq(i)))
    o = rmsnorm(o_fw + o_bw, norm_g) * jax.nn.silu(heads(g_raw))
    return o.reshape(bsz, length, GROUP_W).astype(q_raw.dtype)


def rope_tables(length):
    half = MLA_ROPE // 2
    inv = ROPE_BASE ** (-jnp.arange(half, dtype=jnp.float32) / half)
    ang = jnp.arange(length, dtype=jnp.float32)[:, None] * inv[None, :]
    return jnp.cos(ang), jnp.sin(ang)


def apply_rope(x, cos, sin):
    half = MLA_ROPE // 2
    x1, x2 = x[..., :half], x[..., half:]
    return jnp.concatenate([x1 * cos - x2 * sin, x1 * sin + x2 * cos], axis=-1)


def mla(c_q, c_kv, k_r, q_norm, w_uq, kv_norm, w_ukv):
    bsz, length, _ = c_q.shape
    q = (rmsnorm(c_q, q_norm) @ w_uq).reshape(bsz, length, H_GROUP, MLA_NOPE + MLA_ROPE)
    kv = (rmsnorm(c_kv, kv_norm) @ w_ukv).reshape(bsz, length, H_GROUP, MLA_NOPE + MLA_V)
    q_nope, q_rope = q[..., :MLA_NOPE], q[..., MLA_NOPE:]
    k_nope, v = kv[..., :MLA_NOPE], kv[..., MLA_NOPE:]
    cos, sin = rope_tables(length)
    q_rope = apply_rope(q_rope, cos[:, None, :], sin[:, None, :]).astype(q_nope.dtype)
    k_rope = apply_rope(k_r, cos, sin).astype(k_nope.dtype)
    scale = (MLA_NOPE + MLA_ROPE) ** -0.5
    nb = length // MLA_BLOCK_Q
    qn_b = q_nope.reshape(bsz, nb, MLA_BLOCK_Q, H_GROUP, MLA_NOPE).swapaxes(0, 1)
    qr_b = q_rope.reshape(bsz, nb, MLA_BLOCK_Q, H_GROUP, MLA_ROPE).swapaxes(0, 1)

    def block(args):
        qn, qr = args
        s = (jnp.einsum('bqhd,bkhd->bhqk', qn, k_nope)
             + jnp.einsum('bqhr,bkr->bhqk', qr, k_rope)).astype(jnp.float32) * scale
        p = jax.nn.softmax(s, axis=-1)
        return jnp.einsum('bhqk,bkhv->bqhv', p.astype(v.dtype), v)

    o = lax.map(block, (qn_b, qr_b))
    return o.swapaxes(0, 1).reshape(bsz, length, H_GROUP * MLA_V)


def mixer_ab(h, w_in, w_out, rpb, conv_w, a_log, dt_bias, dn_norm):
    bsz, length, _ = h.shape
    qa, ka, va, qkv_b, z, a_fw, a_bw, b_fw, b_bw = split_sizes(h @ w_in, AB_SPLIT)
    heads = lambda t: t.reshape(bsz, length, H_GROUP, HEAD_DIM)
    oa = neighbourhood_attention(heads(qa), heads(ka), heads(va), rpb)
    ob = gated_deltanet(qkv_b, z, a_fw, a_bw, b_fw, b_bw, conv_w, a_log, dt_bias, dn_norm)
    return jnp.concatenate([oa.astype(h.dtype), ob.astype(h.dtype)], axis=-1) @ w_out


def mixer_cd(h, w_in, w_out, lb_fw, lb_bw, hg_norm, q_norm, w_uq, kv_norm, w_ukv):
    q, f_fw, f_bw, i, g, c_q, c_kv, k_r = split_sizes(h @ w_in, CD_SPLIT)
    oc = hgrn2_mixer(q, f_fw, f_bw, i, g, lb_fw, lb_bw, hg_norm)
    od = mla(c_q, c_kv, k_r, q_norm, w_uq, kv_norm, w_ukv)
    return jnp.concatenate([oc.astype(h.dtype), od.astype(h.dtype)], axis=-1) @ w_out


def conv_ffn(h, w_gate, w_up, conv_w, w_down):
    a = dwconv_centred(h @ w_gate, conv_w)
    return (jax.nn.silu(a) * (h @ w_up)) @ w_down


def trunk(x, c, norm_mix, norm_ffn, w_ada, b_ada, w_in_ab, w_out_ab, na_rpb, dn_conv, dn_a_log,
          dn_dt_bias, dn_norm, w_in_cd, w_out_cd, hg_lower_bounds, hg_norm, mla_q_norm, mla_w_uq,
          mla_kv_norm, mla_w_ukv, ffn_w_gate, ffn_w_up, ffn_conv, ffn_w_down, final_norm):
    lb = jnp.cumsum(jax.nn.softmax(hg_lower_bounds.astype(jnp.float32), axis=1), axis=1)
    lb = lb - lb[:, :1]
    cond = jax.nn.silu(c)
    for l in range(DEPTH):
        mod = cond @ w_ada[l] + b_ada[l]
        sh1, sc1, g1, sh2, sc2, g2 = [m[:, None, :] for m in jnp.split(mod, 6, axis=-1)]
        h = rmsnorm(x, norm_mix[l]) * (1 + sc1) + sh1
        if l % 2 == 0:
            e = l // 2
            y = mixer_ab(h, w_in_ab[e], w_out_ab[e], na_rpb[e], dn_conv[e], dn_a_log[e],
                         dn_dt_bias[e], dn_norm[e])
        else:
            o = l // 2
            y = mixer_cd(h, w_in_cd[o], w_out_cd[o], lb[0, l], lb[1, l], hg_norm[o],
                         mla_q_norm[o], mla_w_uq[o], mla_kv_norm[o], mla_w_ukv[o])
        x = x + (g1 * y).astype(x.dtype)
        h = rmsnorm(x, norm_ffn[l]) * (1 + sc2) + sh2
        x = x + (g2 * conv_ffn(h, ffn_w_gate[l], ffn_w_up[l], ffn_conv[l], ffn_w_down[l])).astype(x.dtype)
    return rmsnorm(x, final_norm)


def setup_inputs(seed: int = 0) -> dict:
    key = jax.random.key(seed)
    ks = iter(jax.random.split(key, 40))
    f32 = jnp.float32
    d = D_MODEL

    def nrm(shape, std):
        return std * jax.random.normal(next(ks), shape, f32)

    def gain(shape):
        return 1.0 + 0.02 * jax.random.normal(next(ks), shape, f32)

    x_prompt = nrm((BATCH, SEQ, d), 1.0)
    x_sample = nrm((DEC_BATCH, DEC_SEQ, d), 1.0)
    c_prompt = nrm((BATCH, d), 1.0)
    c_sample = nrm((DEC_BATCH, d), 1.0)
    dn_a_log = jnp.log(jax.random.uniform(next(ks), (N_EVEN, 2, H_GROUP), f32, 1.0, 16.0))
    dt = jnp.exp(jax.random.uniform(next(ks), (N_EVEN, 2, H_GROUP), f32, math.log(1e-3), math.log(1e-1)))
    dn_dt_bias = dt + jnp.log(-jnp.expm1(-dt))
    return {
        "x_prompt": x_prompt,
        "x_sample": x_sample,
        "c_prompt": c_prompt,
        "c_sample": c_sample,
        "norm_mix": gain((DEPTH, d)),
        "norm_ffn": gain((DEPTH, d)),
        "w_ada": nrm((DEPTH, d, 6 * d), 0.5 * d ** -0.5),
        "b_ada": nrm((DEPTH, 6 * d), 0.01),
        "w_in_ab": nrm((N_EVEN, d, IN_AB), d ** -0.5),
        "w_out_ab": nrm((N_EVEN, 2 * GROUP_W, d), (2 * GROUP_W) ** -0.5),
        "na_rpb": nrm((N_EVEN, H_GROUP, 2 * NA_WIN_R - 1, 2 * NA_WIN_C - 1), 0.1),
        "dn_conv": nrm((N_EVEN, DN_CONV_K, 3 * GROUP_W), DN_CONV_K ** -0.5),
        "dn_a_log": dn_a_log,
        "dn_dt_bias": dn_dt_bias,
        "dn_norm": gain((N_EVEN, HEAD_DIM)),
        "w_in_cd": nrm((N_ODD, d, IN_CD), d ** -0.5),
        "w_out_cd": nrm((N_ODD, GROUP_W + H_GROUP * MLA_V, d), (GROUP_W + H_GROUP * MLA_V) ** -0.5),
        "hg_lower_bounds": nrm((2, DEPTH, GROUP_W), 1.0),
        "hg_norm": gain((N_ODD, HEAD_DIM)),
        "mla_q_norm": gain((N_ODD, MLA_Q_LORA)),
        "mla_w_uq": nrm((N_ODD, MLA_Q_LORA, H_GROUP * (MLA_NOPE + MLA_ROPE)), MLA_Q_LORA ** -0.5),
        "mla_kv_norm": gain((N_ODD, MLA_KV_LORA)),
        "mla_w_ukv": nrm((N_ODD, MLA_KV_LORA, H_GROUP * (MLA_NOPE + MLA_V)), MLA_KV_LORA ** -0.5),
        "ffn_w_gate": nrm((DEPTH, d, D_FF), d ** -0.5),
        "ffn_w_up": nrm((DEPTH, d, D_FF), d ** -0.5),
        "ffn_conv": nrm((DEPTH, FFN_CONV_K, D_FF), FFN_CONV_K ** -0.5),
        "ffn_w_down": nrm((DEPTH, D_FF, d), D_FF ** -0.5),
        "final_norm": gain((d,)),
    }


def reference(x_prompt, x_sample, c_prompt, c_sample, norm_mix, norm_ffn, w_ada, b_ada, w_in_ab,
              w_out_ab, na_rpb, dn_conv, dn_a_log, dn_dt_bias, dn_norm, w_in_cd, w_out_cd,
              hg_lower_bounds, hg_norm, mla_q_norm, mla_w_uq, mla_kv_norm, mla_w_ukv, ffn_w_gate,
              ffn_w_up, ffn_conv, ffn_w_down, final_norm):
    weights = (norm_mix, norm_ffn, w_ada, b_ada, w_in_ab, w_out_ab, na_rpb, dn_conv, dn_a_log,
               dn_dt_bias, dn_norm, w_in_cd, w_out_cd, hg_lower_bounds, hg_norm, mla_q_norm,
               mla_w_uq, mla_kv_norm, mla_w_ukv, ffn_w_gate, ffn_w_up, ffn_conv, ffn_w_down,
               final_norm)
    y_prompt = trunk(x_prompt, c_prompt, *weights)
    y_sample = trunk(x_sample, c_sample, *weights)
    return (y_prompt, y_sample)
```

```python
import functools
import math

import numpy as np
import jax
import jax.numpy as jnp
from jax import lax
from jax.experimental import pallas as pl
from jax.experimental.pallas import tpu as pltpu

D_MODEL = 2048
DEPTH = 4
HEAD_DIM = 128
H_GROUP = 8
GROUP_W = H_GROUP * HEAD_DIM
GRID_W = 64
NA_WIN_R = 8
NA_WIN_C = 16
NA_QCB = NA_WIN_C
NA_BAND = 2 * NA_WIN_C
DN_CONV_K = 5
DN_CHUNK = 64
HG_CHUNK = 32
MLA_Q_LORA = 512
MLA_KV_LORA = 512
MLA_NOPE = 128
MLA_ROPE = 64
MLA_V = 128
MLA_BLOCK_Q = 128
ROPE_BASE = 10000.0
D_FF = 5632
FFN_CONV_K = 3
EPS = 1e-6
AB_SPLIT = [GROUP_W, GROUP_W, GROUP_W, 3 * GROUP_W, GROUP_W, H_GROUP, H_GROUP, H_GROUP, H_GROUP]
CD_SPLIT = [GROUP_W, GROUP_W, GROUP_W, GROUP_W, GROUP_W, MLA_Q_LORA, MLA_KV_LORA, MLA_ROPE]
AB_MAIN = 7 * GROUP_W
CD_MAIN = 6 * GROUP_W
SIDE_W = 128

F32 = jnp.float32
BF16 = jnp.bfloat16

VMEM_LIMIT_BYTES = 56 * 1024 * 1024
HALO = 16


def _params(sem):
    return pltpu.CompilerParams(dimension_semantics=sem, vmem_limit_bytes=VMEM_LIMIT_BYTES)


def _ada_kernel(c_ref, w_ref, b_ref, o_ref):
    c = c_ref[...]
    cond = c * jax.nn.sigmoid(c)
    o_ref[...] = jnp.dot(cond, w_ref[...], precision=lax.Precision.HIGHEST,
                         preferred_element_type=F32) + b_ref[...]


def _ada_call(c_all, w_ada, b_ada):
    rows = c_all.shape[0]
    tn = 1024
    return pl.pallas_call(
        _ada_kernel,
        out_shape=jax.ShapeDtypeStruct((DEPTH, rows, 6 * D_MODEL), F32),
        grid=(DEPTH, 6 * D_MODEL // tn),
        in_specs=[
            pl.BlockSpec((rows, D_MODEL), lambda l, j: (0, 0)),
            pl.BlockSpec((None, D_MODEL, tn), lambda l, j: (l, 0, j)),
            pl.BlockSpec((None, 1, tn), lambda l, j: (l, 0, j)),
        ],
        out_specs=pl.BlockSpec((None, rows, tn), lambda l, j: (l, 0, j)),
        compiler_params=_params(("arbitrary", "arbitrary")),
        name="ada_mod",
    )(c_all, w_ada, b_ada.reshape(DEPTH, 1, 6 * D_MODEL))


def _norm_mod(x, g, sh, sc):
    y = x * lax.rsqrt(jnp.mean(x * x, axis=-1, keepdims=True) + EPS)
    return (y * g) * (1.0 + sc) + sh


def _inproj_kernel(x_ref, g_ref, sh_ref, sc_ref, w_ref, ws_ref, o_ref, os_ref, h_ref):
    @pl.when(pl.program_id(2) == 0)
    def _():
        hb = _norm_mod(x_ref[...], g_ref[...], sh_ref[...], sc_ref[...]).astype(BF16)
        h_ref[...] = hb
        os_ref[...] = jnp.dot(hb, ws_ref[...], preferred_element_type=F32)

    o_ref[...] = jnp.dot(h_ref[...], w_ref[...], preferred_element_type=F32)


def _inproj_call(x, mod3, norm_g, w_main, w_side):
    bsz, length, d = x.shape
    n_main = w_main.shape[1]
    tm = min(1024, length)
    tn = 1024
    return pl.pallas_call(
        _inproj_kernel,
        out_shape=(jax.ShapeDtypeStruct((bsz, length, n_main), F32),
                   jax.ShapeDtypeStruct((bsz, length, SIDE_W), F32)),
        grid=(bsz, length // tm, n_main // tn),
        in_specs=[
            pl.BlockSpec((None, tm, d), lambda b, i, j: (b, i, 0)),
            pl.BlockSpec((1, d), lambda b, i, j: (0, 0)),
            pl.BlockSpec((None, 1, d), lambda b, i, j: (b, 0, 0)),
            pl.BlockSpec((None, 1, d), lambda b, i, j: (b, 0, 1)),
            pl.BlockSpec((d, tn), lambda b, i, j: (0, j)),
            pl.BlockSpec((d, SIDE_W), lambda b, i, j: (0, 0)),
        ],
        out_specs=(pl.BlockSpec((None, tm, tn), lambda b, i, j: (b, i, j)),
                   pl.BlockSpec((None, tm, SIDE_W), lambda b, i, j: (b, i, 0))),
        scratch_shapes=[pltpu.VMEM((tm, d), BF16)],
        compiler_params=_params(("arbitrary", "arbitrary", "arbitrary")),
        name="inproj",
    )(x, norm_g.reshape(1, d), mod3, mod3, w_main, w_side)


def _outproj_kernel(ya_ref, yb_ref, w_ref, x_ref, gate_ref, o_ref):
    half = ya_ref.shape[-1]
    y = jnp.dot(ya_ref[...].astype(BF16), w_ref[0:half, :], preferred_element_type=F32)
    y = y + jnp.dot(yb_ref[...].astype(BF16), w_ref[half:, :], preferred_element_type=F32)
    o_ref[...] = x_ref[...] + gate_ref[...] * y


def _outproj_call(ya, yb, w_out, x, mod3):
    bsz, length, d = x.shape
    half = ya.shape[-1]
    tm = min(512, length)
    return pl.pallas_call(
        _outproj_kernel,
        out_shape=jax.ShapeDtypeStruct(x.shape, F32),
        grid=(bsz, length // tm),
        in_specs=[
            pl.BlockSpec((None, tm, half), lambda b, i: (b, i, 0)),
            pl.BlockSpec((None, tm, half), lambda b, i: (b, i, 0)),
            pl.BlockSpec((2 * half, d), lambda b, i: (0, 0)),
            pl.BlockSpec((None, tm, d), lambda b, i: (b, i, 0)),
            pl.BlockSpec((None, 1, d), lambda b, i: (b, 0, 2)),
        ],
        out_specs=pl.BlockSpec((None, tm, d), lambda b, i: (b, i, 0)),
        compiler_params=_params(("arbitrary", "arbitrary")),
        name="outproj",
    )(ya, yb, w_out, x, mod3)


def _ffn_kernel(final, xp_ref, x_ref, xn_ref, g_ref, sh_ref, sc_ref, gate_ref, wg_ref, wu_ref,
                cw_ref, wd_ref, fn_ref, o_ref, h_ref, gp_ref):
    i = pl.program_id(1)
    j = pl.program_id(2)
    tm = x_ref.shape[0]

    @pl.when(j == 0)
    def _():
        g, sh, sc = g_ref[...], sh_ref[...], sc_ref[...]
        h_ref[HALO:HALO + tm, :] = _norm_mod(x_ref[...], g, sh, sc).astype(BF16)
        hp = _norm_mod(xp_ref[...], g, sh, sc)
        h_ref[0:HALO, :] = jnp.where(i == 0, 0.0, hp).astype(BF16)
        hn = _norm_mod(xn_ref[...], g, sh, sc)
        h_ref[HALO + tm:, :] = jnp.where(i == pl.num_programs(1) - 1, 0.0, hn).astype(BF16)
        o_ref[...] = jnp.zeros_like(o_ref)

    gp_ref[...] = jnp.dot(h_ref[...], wg_ref[...], preferred_element_type=F32)
    cw = cw_ref[...]
    a = gp_ref[pl.ds(HALO - 1, tm), :] * cw[0:1, :]
    a = a + gp_ref[pl.ds(HALO, tm), :] * cw[1:2, :]
    a = a + gp_ref[pl.ds(HALO + 1, tm), :] * cw[2:3, :]
    u = jnp.dot(h_ref[HALO:HALO + tm, :], wu_ref[...], preferred_element_type=F32)
    act = (a * jax.nn.sigmoid(a)) * u
    o_ref[...] += jnp.dot(act.astype(BF16), wd_ref[...], preferred_element_type=F32)

    @pl.when(j == pl.num_programs(2) - 1)
    def _():
        xo = x_ref[...] + gate_ref[...] * o_ref[...]
        if final:
            xo = xo * lax.rsqrt(jnp.mean(xo * xo, axis=-1, keepdims=True) + EPS) * fn_ref[...]
        o_ref[...] = xo


def _ffn_call(x, mod3, norm_g, w_gate, w_up, conv_w, w_down, final_g, final):
    bsz, length, d = x.shape
    tm = min(512, length)
    tf = 512
    nh = tm // HALO
    last_h = length // HALO - 1
    return pl.pallas_call(
        functools.partial(_ffn_kernel, final),
        out_shape=jax.ShapeDtypeStruct(x.shape, F32),
        grid=(bsz, length // tm, D_FF // tf),
        in_specs=[
            pl.BlockSpec((None, HALO, d), lambda b, i, j: (b, jnp.maximum(i * nh - 1, 0), 0)),
            pl.BlockSpec((None, tm, d), lambda b, i, j: (b, i, 0)),
            pl.BlockSpec((None, HALO, d), lambda b, i, j: (b, jnp.minimum((i + 1) * nh, last_h), 0)),
            pl.BlockSpec((1, d), lambda b, i, j: (0, 0)),
            pl.BlockSpec((None, 1, d), lambda b, i, j: (b, 0, 3)),
            pl.BlockSpec((None, 1, d), lambda b, i, j: (b, 0, 4)),
            pl.BlockSpec((None, 1, d), lambda b, i, j: (b, 0, 5)),
            pl.BlockSpec((d, tf), lambda b, i, j: (0, j)),
            pl.BlockSpec((d, tf), lambda b, i, j: (0, j)),
            pl.BlockSpec((FFN_CONV_K, tf), lambda b, i, j: (0, j)),
            pl.BlockSpec((tf, d), lambda b, i, j: (j, 0)),
            pl.BlockSpec((1, d), lambda b, i, j: (0, 0)),
        ],
        out_specs=pl.BlockSpec((None, tm, d), lambda b, i, j: (b, i, 0)),
        scratch_shapes=[pltpu.VMEM((tm + 2 * HALO, d), BF16),
                        pltpu.VMEM((tm + 2 * HALO, tf), F32)],
        compiler_params=_params(("arbitrary", "arbitrary", "arbitrary")),
        name="conv_ffn",
    )(x, x, x, norm_g.reshape(1, d), mod3, mod3, mod3, w_gate, w_up, conv_w, w_down,
      final_g.reshape(1, d))


def _split_sizes(x, sizes):
    idx = np.cumsum(sizes)[:-1].tolist()
    return jnp.split(x, idx, axis=-1)


def _rmsnorm(x, g):
    xf = x.astype(F32)
    y = xf * lax.rsqrt(jnp.mean(xf * xf, axis=-1, keepdims=True) + EPS)
    return (y * g.astype(F32)).astype(x.dtype)


def _l2norm(x):
    xf = x.astype(F32)
    return xf * lax.rsqrt(jnp.sum(xf * xf, axis=-1, keepdims=True) + EPS)


def _dwconv_centred(x, w):
    k_w = w.shape[0]
    pad = k_w // 2
    length = x.shape[1]
    xp = jnp.pad(x, ((0, 0), (pad, pad), (0, 0)))
    out = xp[:, 0:length] * w[0]
    for j in range(1, k_w):
        out = out + xp[:, j:j + length] * w[j]
    return out


def _flip_seq(t):
    return jnp.flip(t, axis=1)


def _na_static():
    n_cb = GRID_W // NA_QCB
    qc = np.arange(GRID_W).reshape(n_cb, NA_QCB)
    band_start = np.clip(np.arange(n_cb) * NA_QCB - NA_WIN_C // 2, 0, GRID_W - NA_BAND)
    kc = band_start[:, None] + np.arange(NA_BAND)[None, :]
    cs = np.clip(qc - NA_WIN_C // 2, 0, GRID_W - NA_WIN_C)
    rel = kc[:, None, :] - cs[:, :, None]
    mask = (rel >= 0) & (rel < NA_WIN_C)
    dc = np.clip(kc[:, None, :] - qc[:, :, None] + NA_WIN_C - 1, 0, 2 * NA_WIN_C - 2)
    return kc, mask, dc


def _neighbourhood_attention(q, k, v, rpb):
    bsz, length, h, d = q.shape
    rows = length // GRID_W
    wr = min(NA_WIN_R, rows)
    n_cb = GRID_W // NA_QCB
    kc, mask_np, dc = _na_static()
    mask = jnp.asarray(mask_np)[None, None, :, :, None, :]
    qg = q.reshape(bsz, rows, n_cb, NA_QCB, h, d) * (d ** -0.5)
    kg = k.reshape(bsz, rows, GRID_W, h, d)
    vg = v.reshape(bsz, rows, GRID_W, h, d)

    def one_row(r):
        rs = jnp.clip(r - wr // 2, 0, rows - wr)
        kb = lax.dynamic_slice_in_dim(kg, rs, wr, axis=1)[:, :, kc]
        vb = lax.dynamic_slice_in_dim(vg, rs, wr, axis=1)[:, :, kc]
        qr = lax.dynamic_index_in_dim(qg, r, axis=1, keepdims=False)
        s = jnp.einsum('bjuhd,bwjmhd->bhjuwm', qr, kb).astype(F32)
        dr = rs + jnp.arange(wr) - r + NA_WIN_R - 1
        bias = rpb[:, dr][:, :, dc].transpose(0, 2, 3, 1, 4)
        s = jnp.where(mask, s + bias.astype(F32)[None], -jnp.inf)
        p = jax.nn.softmax(s.reshape(bsz, h, n_cb, NA_QCB, wr * NA_BAND), axis=-1).reshape(s.shape)
        o = jnp.einsum('bhjuwm,bwjmhd->bjuhd', p.astype(v.dtype), vb)
        return o.reshape(bsz, GRID_W, h, d)

    out = lax.map(one_row, jnp.arange(rows))
    return out.transpose(1, 0, 2, 3, 4).reshape(bsz, length, h * d)


def _gated_delta_rule(q, k, v, g, beta):
    bsz, length, h, dk = q.shape
    dv = v.shape[-1]
    c = DN_CHUNK
    n = length // c

    def chunk(x):
        x = x.astype(F32).reshape((bsz, n, c, h) + x.shape[3:])
        return jnp.moveaxis(x, 3, 1)

    q, k, v, g, beta = chunk(q), chunk(k), chunk(v), chunk(g), chunk(beta)
    q = q * (dk ** -0.5)
    gc = jnp.cumsum(g, axis=-1)
    tri = jnp.tril(jnp.ones((c, c), dtype=bool))
    strict = jnp.tril(jnp.ones((c, c), dtype=bool), -1)
    gam = jnp.exp(jnp.where(tri, gc[..., :, None] - gc[..., None, :], -jnp.inf))
    kb = k * beta[..., None]
    a = jnp.where(strict, jnp.einsum('bhnid,bhnjd->bhnij', kb, k) * gam, 0.0)
    p = -a
    t = jnp.eye(c, dtype=F32) + p
    for _ in range(int(math.log2(c)) - 1):
        p = p @ p
        t = t + t @ p
    u = t @ (v * beta[..., None])
    w = t @ (kb * jnp.exp(gc)[..., None])
    aqk = jnp.einsum('bhnid,bhnjd->bhnij', q, k) * gam
    qd = q * jnp.exp(gc)[..., None]
    kd = k * jnp.exp(gc[..., -1:] - gc)[..., None]
    dlast = jnp.exp(gc[..., -1])

    def step(s_state, xs):
        u_n, w_n, aqk_n, qd_n, kd_n, dl_n = xs
        v_new = u_n - w_n @ s_state
        o = qd_n @ s_state + aqk_n @ v_new
        s_state = s_state * dl_n[..., None, None] + jnp.swapaxes(kd_n, -1, -2) @ v_new
        return s_state, o

    xs = (jnp.moveaxis(u, 2, 0), jnp.moveaxis(w, 2, 0), jnp.moveaxis(aqk, 2, 0),
          jnp.moveaxis(qd, 2, 0), jnp.moveaxis(kd, 2, 0), jnp.moveaxis(dlast, 2, 0))
    s0 = jnp.zeros((bsz, h, dk, dv), F32)
    _, o = lax.scan(step, s0, xs)
    return o.transpose(1, 0, 3, 2, 4).reshape(bsz, length, h, dv)


def _gated_deltanet(qkv, z, a_fw, a_bw, b_fw, b_bw, conv_w, a_log, dt_bias, norm_g):
    bsz, length, _ = qkv.shape
    qkv = jax.nn.silu(_dwconv_centred(qkv, conv_w))
    q, k, v = jnp.split(qkv, 3, axis=-1)
    heads = lambda t: t.reshape(bsz, length, H_GROUP, HEAD_DIM)
    q, k, v = _l2norm(heads(q)), _l2norm(heads(k)), heads(v).astype(F32)

    def gates(a_raw, b_raw, d):
        g = -jnp.exp(a_log[d].astype(F32)) * jax.nn.softplus(
            a_raw.astype(F32) + dt_bias[d].astype(F32))
        return g, jax.nn.sigmoid(b_raw.astype(F32))

    g_fw, be_fw = gates(a_fw, b_fw, 0)
    g_bw, be_bw = gates(a_bw, b_bw, 1)
    o_fw = _gated_delta_rule(q, k, v, g_fw, be_fw)
    o_bw = _flip_seq(_gated_delta_rule(_flip_seq(q), _flip_seq(k), _flip_seq(v), _flip_seq(g_bw),
                                       _flip_seq(be_bw)))
    o = _rmsnorm(o_fw + o_bw, norm_g) * jax.nn.silu(heads(z).astype(F32))
    return o.reshape(bsz, length, GROUP_W).astype(qkv.dtype)


def _hgrn2_chunked(q, lf, i):
    bsz, length, h, dk = q.shape
    dv = i.shape[-1]
    c = HG_CHUNK
    n = length // c

    def chunk(x):
        return x.reshape(bsz, n, c, h, x.shape[-1]).transpose(1, 0, 3, 2, 4)

    q, lf, i = chunk(q), chunk(lf), chunk(i)
    k = -jnp.expm1(lf)
    bc = jnp.cumsum(lf, axis=3)
    tri = jnp.tril(jnp.ones((c, c), dtype=bool))[:, :, None]

    def step(s_state, xs):
        q_n, k_n, i_n, b_n = xs
        o = (q_n * jnp.exp(b_n)) @ s_state
        decay = jnp.exp(jnp.where(tri, b_n[..., :, None, :] - b_n[..., None, :, :], -jnp.inf))
        att = jnp.einsum('bhtd,bhsd,bhtsd->bhts', q_n, k_n, decay)
        o = o + att @ i_n
        blast = b_n[..., -1:, :]
        s_state = s_state * jnp.exp(blast)[..., 0, :, None] + jnp.einsum(
            'bhsd,bhsv->bhdv', k_n * jnp.exp(blast - b_n), i_n)
        return s_state, o

    s0 = jnp.zeros((bsz, h, dk, dv), F32)
    _, o = lax.scan(step, s0, (q, k, i, bc))
    return o.transpose(1, 0, 3, 2, 4).reshape(bsz, length, h, dv)


def _hgrn2_mixer(q_raw, f_fw, f_bw, i_raw, g_raw, lb_fw, lb_bw, norm_g):
    bsz, length, _ = q_raw.shape
    heads = lambda t: t.astype(F32).reshape(bsz, length, H_GROUP, HEAD_DIM)
    q = jax.nn.silu(heads(q_raw))
    i = heads(i_raw)

    def log_forget(z, lb):
        lb = lb.reshape(H_GROUP, HEAD_DIM)
        return jnp.logaddexp(jnp.log(lb), jnp.log1p(-lb) + jax.nn.log_sigmoid(heads(z)))

    o_fw = _hgrn2_chunked(q, log_forget(f_fw, lb_fw), i)
    o_bw = _flip_seq(_hgrn2_chunked(_flip_seq(q), _flip_seq(log_forget(f_bw, lb_bw)), _flip_seq(i)))
    o = _rmsnorm(o_fw + o_bw, norm_g) * jax.nn.silu(heads(g_raw))
    return o.reshape(bsz, length, GROUP_W).astype(q_raw.dtype)


def _rope_tables(length):
    half = MLA_ROPE // 2
    inv = ROPE_BASE ** (-jnp.arange(half, dtype=F32) / half)
    ang = jnp.arange(length, dtype=F32)[:, None] * inv[None, :]
    return jnp.cos(ang), jnp.sin(ang)


def _apply_rope(x, cos, sin):
    half = MLA_ROPE // 2
    x1, x2 = x[..., :half], x[..., half:]
    return jnp.concatenate([x1 * cos - x2 * sin, x1 * sin + x2 * cos], axis=-1)


def _mla(c_q, c_kv, k_r, q_norm, w_uq, kv_norm, w_ukv):
    bsz, length, _ = c_q.shape
    q = (_rmsnorm(c_q, q_norm) @ w_uq).reshape(bsz, length, H_GROUP, MLA_NOPE + MLA_ROPE)
    kv = (_rmsnorm(c_kv, kv_norm) @ w_ukv).reshape(bsz, length, H_GROUP, MLA_NOPE + MLA_V)
    q_nope, q_rope = q[..., :MLA_NOPE], q[..., MLA_NOPE:]
    k_nope, v = kv[..., :MLA_NOPE], kv[..., MLA_NOPE:]
    cos, sin = _rope_tables(length)
    q_rope = _apply_rope(q_rope, cos[:, None, :], sin[:, None, :]).astype(q_nope.dtype)
    k_rope = _apply_rope(k_r, cos, sin).astype(k_nope.dtype)
    scale = (MLA_NOPE + MLA_ROPE) ** -0.5
    nb = length // MLA_BLOCK_Q
    qn_b = q_nope.reshape(bsz, nb, MLA_BLOCK_Q, H_GROUP, MLA_NOPE).swapaxes(0, 1)
    qr_b = q_rope.reshape(bsz, nb, MLA_BLOCK_Q, H_GROUP, MLA_ROPE).swapaxes(0, 1)

    def block(args):
        qn, qr = args
        s = (jnp.einsum('bqhd,bkhd->bhqk', qn, k_nope)
             + jnp.einsum('bqhr,bkr->bhqk', qr, k_rope)).astype(F32) * scale
        p = jax.nn.softmax(s, axis=-1)
        return jnp.einsum('bhqk,bkhv->bqhv', p.astype(v.dtype), v)

    o = lax.map(block, (qn_b, qr_b))
    return o.swapaxes(0, 1).reshape(bsz, length, H_GROUP * MLA_V)


def _mixer_ab(main, side, rpb, conv_w, a_log, dt_bias, dn_norm):
    bsz, length, _ = main.shape
    qa, ka, va, qkv_b, z = _split_sizes(main, AB_SPLIT[:5])
    a_fw, a_bw, b_fw, b_bw = _split_sizes(side[..., :4 * H_GROUP], AB_SPLIT[5:])
    heads = lambda t: t.reshape(bsz, length, H_GROUP, HEAD_DIM)
    oa = _neighbourhood_attention(heads(qa), heads(ka), heads(va), rpb)
    ob = _gated_deltanet(qkv_b, z, a_fw, a_bw, b_fw, b_bw, conv_w, a_log, dt_bias, dn_norm)
    return oa, ob


def _mixer_cd(main, side, lb_fw, lb_bw, hg_norm, q_norm, w_uq, kv_norm, w_ukv):
    q, f_fw, f_bw, i, g, c_q, c_kv = _split_sizes(main, CD_SPLIT[:7])
    k_r = side[..., :MLA_ROPE]
    oc = _hgrn2_mixer(q, f_fw, f_bw, i, g, lb_fw, lb_bw, hg_norm)
    od = _mla(c_q, c_kv, k_r, q_norm, w_uq, kv_norm, w_ukv)
    return oc, od


def _pad_side(w):
    return jnp.pad(w, ((0, 0), (0, SIDE_W - w.shape[1]))).astype(BF16)


def _trunk(x, mods, lb, p):
    for l in range(DEPTH):
        mod3 = mods[l]
        if l % 2 == 0:
            e = l // 2
            main, side = _inproj_call(x, mod3, p["norm_mix"][l], p["w_in_ab_main"][e],
                                      p["w_in_ab_side"][e])
            ya, yb = _mixer_ab(main, side, p["na_rpb"][e], p["dn_conv"][e], p["dn_a_log"][e],
                               p["dn_dt_bias"][e], p["dn_norm"][e])
            w_out = p["w_out_ab"][e]
        else:
            o = l // 2
            main, side = _inproj_call(x, mod3, p["norm_mix"][l], p["w_in_cd_main"][o],
                                      p["w_in_cd_side"][o])
            ya, yb = _mixer_cd(main, side, lb[0, l], lb[1, l], p["hg_norm"][o],
                               p["mla_q_norm"][o], p["mla_w_uq"][o], p["mla_kv_norm"][o],
                               p["mla_w_ukv"][o])
            w_out = p["w_out_cd"][o]
        x = _outproj_call(ya, yb, w_out, x, mod3)
        x = _ffn_call(x, mod3, p["norm_ffn"][l], p["ffn_w_gate"][l], p["ffn_w_up"][l],
                      p["ffn_conv"][l], p["ffn_w_down"][l], p["final_norm"], l == DEPTH - 1)
    return x


def kernel(x_prompt, x_sample, c_prompt, c_sample, norm_mix, norm_ffn, w_ada, b_ada, w_in_ab, w_out_ab, na_rpb, dn_conv, dn_a_log, dn_dt_bias, dn_norm, w_in_cd, w_out_cd, hg_lower_bounds, hg_norm, mla_q_norm, mla_w_uq, mla_kv_norm, mla_w_ukv, ffn_w_gate, ffn_w_up, ffn_conv, ffn_w_down, final_norm):
    n_p, n_s = c_prompt.shape[0], c_sample.shape[0]
    rows = -(-(n_p + n_s) // 8) * 8
    c_all = jnp.concatenate([c_prompt, c_sample, jnp.zeros((rows - n_p - n_s, D_MODEL), F32)], axis=0)
    mod_all = _ada_call(c_all, w_ada, b_ada)
    mods_p = [mod_all[l, :n_p].reshape(n_p, 1, 6 * D_MODEL) for l in range(DEPTH)]
    mods_s = [mod_all[l, n_p:n_p + n_s].reshape(n_s, 1, 6 * D_MODEL) for l in range(DEPTH)]

    lb = jnp.cumsum(jax.nn.softmax(hg_lower_bounds.astype(F32), axis=1), axis=1)
    lb = lb - lb[:, :1]

    p = dict(
        norm_mix=norm_mix, norm_ffn=norm_ffn, na_rpb=na_rpb, dn_conv=dn_conv, dn_a_log=dn_a_log,
        dn_dt_bias=dn_dt_bias, dn_norm=dn_norm, hg_norm=hg_norm, mla_q_norm=mla_q_norm,
        mla_w_uq=mla_w_uq, mla_kv_norm=mla_kv_norm, mla_w_ukv=mla_w_ukv, ffn_conv=ffn_conv,
        final_norm=final_norm,
        w_in_ab_main=w_in_ab[:, :, :AB_MAIN].astype(BF16),
        w_in_ab_side=jax.vmap(_pad_side)(w_in_ab[:, :, AB_MAIN:]),
        w_in_cd_main=w_in_cd[:, :, :CD_MAIN].astype(BF16),
        w_in_cd_side=jax.vmap(_pad_side)(w_in_cd[:, :, CD_MAIN:]),
        w_out_ab=w_out_ab.astype(BF16), w_out_cd=w_out_cd.astype(BF16),
        ffn_w_gate=ffn_w_gate.astype(BF16), ffn_w_up=ffn_w_up.astype(BF16),
        ffn_w_down=ffn_w_down.astype(BF16),
    )
    y_prompt = _trunk(x_prompt, mods_p, lb, p)
    y_sample = _trunk(x_sample, mods_s, lb, p)
    return (y_prompt, y_sample)
```

```python
import functools
import math

import numpy as np
import jax
import jax.numpy as jnp
from jax import lax
from jax.experimental import pallas as pl
from jax.experimental.pallas import tpu as pltpu

D_MODEL = 2048
DEPTH = 4
HEAD_DIM = 128
H_GROUP = 8
GROUP_W = H_GROUP * HEAD_DIM
GRID_W = 64
NA_WIN_R = 8
NA_WIN_C = 16
NA_QCB = NA_WIN_C
NA_BAND = 2 * NA_WIN_C
DN_CONV_K = 5
DN_CHUNK = 64
HG_CHUNK = 32
MLA_Q_LORA = 512
MLA_KV_LORA = 512
MLA_NOPE = 128
MLA_ROPE = 64
MLA_V = 128
MLA_BLOCK_Q = 128
ROPE_BASE = 10000.0
D_FF = 5632
FFN_CONV_K = 3
EPS = 1e-6
AB_SPLIT = [GROUP_W, GROUP_W, GROUP_W, 3 * GROUP_W, GROUP_W, H_GROUP, H_GROUP, H_GROUP, H_GROUP]
CD_SPLIT = [GROUP_W, GROUP_W, GROUP_W, GROUP_W, GROUP_W, MLA_Q_LORA, MLA_KV_LORA, MLA_ROPE]
AB_MAIN = 7 * GROUP_W
CD_MAIN = 6 * GROUP_W
SIDE_W = 128

F32 = jnp.float32
BF16 = jnp.bfloat16

VMEM_LIMIT_BYTES = 56 * 1024 * 1024
HALO = 16


def _params(sem):
    return pltpu.CompilerParams(dimension_semantics=sem, vmem_limit_bytes=VMEM_LIMIT_BYTES)


def _ada_kernel(c_ref, w_ref, b_ref, o_ref):
    c = c_ref[...]
    cond = c * jax.nn.sigmoid(c)
    o_ref[...] = jnp.dot(cond, w_ref[...], precision=lax.Precision.HIGHEST,
                         preferred_element_type=F32) + b_ref[...]


def _ada_call(c_all, w_ada, b_ada):
    rows = c_all.shape[0]
    tn = 1024
    return pl.pallas_call(
        _ada_kernel,
        out_shape=jax.ShapeDtypeStruct((DEPTH, rows, 6 * D_MODEL), F32),
        grid=(DEPTH, 6 * D_MODEL // tn),
        in_specs=[
            pl.BlockSpec((rows, D_MODEL), lambda l, j: (0, 0)),
            pl.BlockSpec((None, D_MODEL, tn), lambda l, j: (l, 0, j)),
            pl.BlockSpec((None, 1, tn), lambda l, j: (l, 0, j)),
        ],
        out_specs=pl.BlockSpec((None, rows, tn), lambda l, j: (l, 0, j)),
        compiler_params=_params(("arbitrary", "arbitrary")),
        name="ada_mod",
    )(c_all, w_ada, b_ada.reshape(DEPTH, 1, 6 * D_MODEL))


def _norm_mod(x, g, sh, sc):
    y = x * lax.rsqrt(jnp.mean(x * x, axis=-1, keepdims=True) + EPS)
    return (y * g) * (1.0 + sc) + sh


def _inproj_kernel(x_ref, g_ref, sh_ref, sc_ref, w_ref, ws_ref, o_ref, os_ref, h_ref):
    @pl.when(pl.program_id(2) == 0)
    def _():
        hb = _norm_mod(x_ref[...], g_ref[...], sh_ref[...], sc_ref[...]).astype(BF16)
        h_ref[...] = hb
        os_ref[...] = jnp.dot(hb, ws_ref[...], preferred_element_type=F32)

    o_ref[...] = jnp.dot(h_ref[...], w_ref[...], preferred_element_type=F32)


def _inproj_call(x, mod3, norm_g, w_main, w_side):
    bsz, length, d = x.shape
    n_main = w_main.shape[1]
    tm = min(1024, length)
    tn = 1024
    return pl.pallas_call(
        _inproj_kernel,
        out_shape=(jax.ShapeDtypeStruct((bsz, length, n_main), F32),
                   jax.ShapeDtypeStruct((bsz, length, SIDE_W), F32)),
        grid=(bsz, length // tm, n_main // tn),
        in_specs=[
            pl.BlockSpec((None, tm, d), lambda b, i, j: (b, i, 0)),
            pl.BlockSpec((1, d), lambda b, i, j: (0, 0)),
            pl.BlockSpec((None, 1, d), lambda b, i, j: (b, 0, 0)),
            pl.BlockSpec((None, 1, d), lambda b, i, j: (b, 0, 1)),
            pl.BlockSpec((d, tn), lambda b, i, j: (0, j)),
            pl.BlockSpec((d, SIDE_W), lambda b, i, j: (0, 0)),
        ],
        out_specs=(pl.BlockSpec((None, tm, tn), lambda b, i, j: (b, i, j)),
                   pl.BlockSpec((None, tm, SIDE_W), lambda b, i, j: (b, i, 0))),
        scratch_shapes=[pltpu.VMEM((tm, d), BF16)],
        compiler_params=_params(("arbitrary", "arbitrary", "arbitrary")),
        name="inproj",
    )(x, norm_g.reshape(1, d), mod3, mod3, w_main, w_side)


def _outproj_kernel(ya_ref, yb_ref, w_ref, x_ref, gate_ref, o_ref):
    half = ya_ref.shape[-1]
    y = jnp.dot(ya_ref[...].astype(BF16), w_ref[0:half, :], preferred_element_type=F32)
    y = y + jnp.dot(yb_ref[...].astype(BF16), w_ref[half:, :], preferred_element_type=F32)
    o_ref[...] = x_ref[...] + gate_ref[...] * y


def _outproj_call(ya, yb, w_out, x, mod3):
    bsz, length, d = x.shape
    half = ya.shape[-1]
    tm = min(512, length)
    return pl.pallas_call(
        _outproj_kernel,
        out_shape=jax.ShapeDtypeStruct(x.shape, F32),
        grid=(bsz, length // tm),
        in_specs=[
            pl.BlockSpec((None, tm, half), lambda b, i: (b, i, 0)),
            pl.BlockSpec((None, tm, half), lambda b, i: (b, i, 0)),
            pl.BlockSpec((2 * half, d), lambda b, i: (0, 0)),
            pl.BlockSpec((None, tm, d), lambda b, i: (b, i, 0)),
            pl.BlockSpec((None, 1, d), lambda b, i: (b, 0, 2)),
        ],
        out_specs=pl.BlockSpec((None, tm, d), lambda b, i: (b, i, 0)),
        compiler_params=_params(("arbitrary", "arbitrary")),
        name="outproj",
    )(ya, yb, w_out, x, mod3)


def _ffn_kernel(final, xp_ref, x_ref, xn_ref, g_ref, sh_ref, sc_ref, gate_ref, wg_ref, wu_ref,
                cw_ref, wd_ref, fn_ref, o_ref, h_ref, gp_ref):
    i = pl.program_id(1)
    j = pl.program_id(2)
    tm = x_ref.shape[0]

    @pl.when(j == 0)
    def _():
        g, sh, sc = g_ref[...], sh_ref[...], sc_ref[...]
        h_ref[HALO:HALO + tm, :] = _norm_mod(x_ref[...], g, sh, sc).astype(BF16)
        hp = _norm_mod(xp_ref[...], g, sh, sc)
        h_ref[0:HALO, :] = jnp.where(i == 0, 0.0, hp).astype(BF16)
        hn = _norm_mod(xn_ref[...], g, sh, sc)
        h_ref[HALO + tm:, :] = jnp.where(i == pl.num_programs(1) - 1, 0.0, hn).astype(BF16)
        o_ref[...] = jnp.zeros_like(o_ref)

    gp_ref[...] = jnp.dot(h_ref[...], wg_ref[...], preferred_element_type=F32)
    cw = cw_ref[...]
    a = gp_ref[pl.ds(HALO - 1, tm), :] * cw[0:1, :]
    a = a + gp_ref[pl.ds(HALO, tm), :] * cw[1:2, :]
    a = a + gp_ref[pl.ds(HALO + 1, tm), :] * cw[2:3, :]
    u = jnp.dot(h_ref[HALO:HALO + tm, :], wu_ref[...], preferred_element_type=F32)
    act = (a * jax.nn.sigmoid(a)) * u
    o_ref[...] += jnp.dot(act.astype(BF16), wd_ref[...], preferred_element_type=F32)

    @pl.when(j == pl.num_programs(2) - 1)
    def _():
        xo = x_ref[...] + gate_ref[...] * o_ref[...]
        if final:
            xo = xo * lax.rsqrt(jnp.mean(xo * xo, axis=-1, keepdims=True) + EPS) * fn_ref[...]
        o_ref[...] = xo


def _ffn_call(x, mod3, norm_g, w_gate, w_up, conv_w, w_down, final_g, final):
    bsz, length, d = x.shape
    tm = min(512, length)
    tf = 512
    nh = tm // HALO
    last_h = length // HALO - 1
    return pl.pallas_call(
        functools.partial(_ffn_kernel, final),
        out_shape=jax.ShapeDtypeStruct(x.shape, F32),
        grid=(bsz, length // tm, D_FF // tf),
        in_specs=[
            pl.BlockSpec((None, HALO, d), lambda b, i, j: (b, jnp.maximum(i * nh - 1, 0), 0)),
            pl.BlockSpec((None, tm, d), lambda b, i, j: (b, i, 0)),
            pl.BlockSpec((None, HALO, d), lambda b, i, j: (b, jnp.minimum((i + 1) * nh, last_h), 0)),
            pl.BlockSpec((1, d), lambda b, i, j: (0, 0)),
            pl.BlockSpec((None, 1, d), lambda b, i, j: (b, 0, 3)),
            pl.BlockSpec((None, 1, d), lambda b, i, j: (b, 0, 4)),
            pl.BlockSpec((None, 1, d), lambda b, i, j: (b, 0, 5)),
            pl.BlockSpec((d, tf), lambda b, i, j: (0, j)),
            pl.BlockSpec((d, tf), lambda b, i, j: (0, j)),
            pl.BlockSpec((FFN_CONV_K, tf), lambda b, i, j: (0, j)),
            pl.BlockSpec((tf, d), lambda b, i, j: (j, 0)),
            pl.BlockSpec((1, d), lambda b, i, j: (0, 0)),
        ],
        out_specs=pl.BlockSpec((None, tm, d), lambda b, i, j: (b, i, 0)),
        scratch_shapes=[pltpu.VMEM((tm + 2 * HALO, d), BF16),
                        pltpu.VMEM((tm + 2 * HALO, tf), F32)],
        compiler_params=_params(("arbitrary", "arbitrary", "arbitrary")),
        name="conv_ffn",
    )(x, x, x, norm_g.reshape(1, d), mod3, mod3, mod3, w_gate, w_up, conv_w, w_down,
      final_g.reshape(1, d))


def _split_sizes(x, sizes):
    idx = np.cumsum(sizes)[:-1].tolist()
    return jnp.split(x, idx, axis=-1)


def _rmsnorm(x, g):
    xf = x.astype(F32)
    y = xf * lax.rsqrt(jnp.mean(xf * xf, axis=-1, keepdims=True) + EPS)
    return (y * g.astype(F32)).astype(x.dtype)


def _l2norm(x):
    xf = x.astype(F32)
    return xf * lax.rsqrt(jnp.sum(xf * xf, axis=-1, keepdims=True) + EPS)


def _dwconv_centred(x, w):
    k_w = w.shape[0]
    pad = k_w // 2
    length = x.shape[1]
    xp = jnp.pad(x, ((0, 0), (pad, pad), (0, 0)))
    out = xp[:, 0:length] * w[0]
    for j in range(1, k_w):
        out = out + xp[:, j:j + length] * w[j]
    return out


def _flip_seq(t):
    return jnp.flip(t, axis=1)


NA_KEYS = NA_WIN_R * GRID_W
MASKED = -1e30


def _na_bias_table(rpb):
    qc = np.arange(GRID_W)[:, None]
    kc = np.arange(GRID_W)[None, :]
    cs = np.clip(qc - NA_WIN_C // 2, 0, GRID_W - NA_WIN_C)
    valid = (kc >= cs) & (kc < cs + NA_WIN_C)
    dc = np.clip(kc - qc + NA_WIN_C - 1, 0, 2 * NA_WIN_C - 2)
    dr = np.arange(NA_WIN_R)[None, :] - np.arange(NA_WIN_R)[:, None] + NA_WIN_R - 1
    t = rpb.astype(F32)[:, dr][:, :, :, dc]
    t = jnp.where(valid[None, None, None], t, MASKED)
    return t.transpose(0, 1, 3, 2, 4).reshape(rpb.shape[0], NA_WIN_R, GRID_W, NA_KEYS)


def _na_kernel(q_ref, k_ref, v_ref, bias_ref, o_ref, kb_ref, vb_ref):
    rows = q_ref.shape[0] // GRID_W
    kb_ref[...] = k_ref[...].astype(BF16)
    vb_ref[...] = v_ref[...].astype(BF16)

    def one_row(r, carry):
        rs = jnp.clip(r - NA_WIN_R // 2, 0, rows - NA_WIN_R)
        q0 = pl.multiple_of(r * GRID_W, GRID_W)
        k0 = pl.multiple_of(rs * GRID_W, GRID_W)
        q = (q_ref[pl.ds(q0, GRID_W), :] * (HEAD_DIM ** -0.5)).astype(BF16)
        s = lax.dot_general(q, kb_ref[pl.ds(k0, NA_KEYS), :], (((1,), (1,)), ((), ())),
                            preferred_element_type=F32)
        s = s + bias_ref[r - rs]
        p = jnp.exp(s - jnp.max(s, axis=-1, keepdims=True))
        o = jnp.dot(p.astype(BF16), vb_ref[pl.ds(k0, NA_KEYS), :], preferred_element_type=F32)
        o_ref[pl.ds(q0, GRID_W), :] = o / jnp.sum(p, axis=-1, keepdims=True)
        return carry

    lax.fori_loop(0, rows, one_row, 0)


def _na_call(main, rpb):
    bsz, length, _ = main.shape
    assert length % GRID_W == 0 and length // GRID_W >= NA_WIN_R
    hd = HEAD_DIM
    return pl.pallas_call(
        _na_kernel,
        out_shape=jax.ShapeDtypeStruct((bsz, length, GROUP_W), F32),
        grid=(bsz, H_GROUP),
        in_specs=[
            pl.BlockSpec((None, length, hd), lambda b, h: (b, 0, h)),
            pl.BlockSpec((None, length, hd), lambda b, h: (b, 0, H_GROUP + h)),
            pl.BlockSpec((None, length, hd), lambda b, h: (b, 0, 2 * H_GROUP + h)),
            pl.BlockSpec((None, NA_WIN_R, GRID_W, NA_KEYS), lambda b, h: (h, 0, 0, 0)),
        ],
        out_specs=pl.BlockSpec((None, length, hd), lambda b, h: (b, 0, h)),
        scratch_shapes=[pltpu.VMEM((length, hd), BF16), pltpu.VMEM((length, hd), BF16)],
        compiler_params=_params(("arbitrary", "arbitrary")),
        name="nbr_attn",
    )(main, main, main, _na_bias_table(rpb))


def _gated_delta_rule(q, k, v, g, beta):
    bsz, length, h, dk = q.shape
    dv = v.shape[-1]
    c = DN_CHUNK
    n = length // c

    def chunk(x):
        x = x.astype(F32).reshape((bsz, n, c, h) + x.shape[3:])
        return jnp.moveaxis(x, 3, 1)

    q, k, v, g, beta = chunk(q), chunk(k), chunk(v), chunk(g), chunk(beta)
    q = q * (dk ** -0.5)
    gc = jnp.cumsum(g, axis=-1)
    tri = jnp.tril(jnp.ones((c, c), dtype=bool))
    strict = jnp.tril(jnp.ones((c, c), dtype=bool), -1)
    gam = jnp.exp(jnp.where(tri, gc[..., :, None] - gc[..., None, :], -jnp.inf))
    kb = k * beta[..., None]
    a = jnp.where(strict, jnp.einsum('bhnid,bhnjd->bhnij', kb, k) * gam, 0.0)
    p = -a
    t = jnp.eye(c, dtype=F32) + p
    for _ in range(int(math.log2(c)) - 1):
        p = p @ p
        t = t + t @ p
    u = t @ (v * beta[..., None])
    w = t @ (kb * jnp.exp(gc)[..., None])
    aqk = jnp.einsum('bhnid,bhnjd->bhnij', q, k) * gam
    qd = q * jnp.exp(gc)[..., None]
    kd = k * jnp.exp(gc[..., -1:] - gc)[..., None]
    dlast = jnp.exp(gc[..., -1])

    def step(s_state, xs):
        u_n, w_n, aqk_n, qd_n, kd_n, dl_n = xs
        v_new = u_n - w_n @ s_state
        o = qd_n @ s_state + aqk_n @ v_new
        s_state = s_state * dl_n[..., None, None] + jnp.swapaxes(kd_n, -1, -2) @ v_new
        return s_state, o

    xs = (jnp.moveaxis(u, 2, 0), jnp.moveaxis(w, 2, 0), jnp.moveaxis(aqk, 2, 0),
          jnp.moveaxis(qd, 2, 0), jnp.moveaxis(kd, 2, 0), jnp.moveaxis(dlast, 2, 0))
    s0 = jnp.zeros((bsz, h, dk, dv), F32)
    _, o = lax.scan(step, s0, xs)
    return o.transpose(1, 0, 3, 2, 4).reshape(bsz, length, h, dv)


def _gated_deltanet(qkv, z, a_fw, a_bw, b_fw, b_bw, conv_w, a_log, dt_bias, norm_g):
    bsz, length, _ = qkv.shape
    qkv = jax.nn.silu(_dwconv_centred(qkv, conv_w))
    q, k, v = jnp.split(qkv, 3, axis=-1)
    heads = lambda t: t.reshape(bsz, length, H_GROUP, HEAD_DIM)
    q, k, v = _l2norm(heads(q)), _l2norm(heads(k)), heads(v).astype(F32)

    def gates(a_raw, b_raw, d):
        g = -jnp.exp(a_log[d].astype(F32)) * jax.nn.softplus(
            a_raw.astype(F32) + dt_bias[d].astype(F32))
        return g, jax.nn.sigmoid(b_raw.astype(F32))

    g_fw, be_fw = gates(a_fw, b_fw, 0)
    g_bw, be_bw = gates(a_bw, b_bw, 1)
    o_fw = _gated_delta_rule(q, k, v, g_fw, be_fw)
    o_bw = _flip_seq(_gated_delta_rule(_flip_seq(q), _flip_seq(k), _flip_seq(v), _flip_seq(g_bw),
                                       _flip_seq(be_bw)))
    o = _rmsnorm(o_fw + o_bw, norm_g) * jax.nn.silu(heads(z).astype(F32))
    return o.reshape(bsz, length, GROUP_W).astype(qkv.dtype)


def _hgrn2_chunked(q, lf, i):
    bsz, length, h, dk = q.shape
    dv = i.shape[-1]
    c = HG_CHUNK
    n = length // c

    def chunk(x):
        return x.reshape(bsz, n, c, h, x.shape[-1]).transpose(1, 0, 3, 2, 4)

    q, lf, i = chunk(q), chunk(lf), chunk(i)
    k = -jnp.expm1(lf)
    bc = jnp.cumsum(lf, axis=3)
    tri = jnp.tril(jnp.ones((c, c), dtype=bool))[:, :, None]

    def step(s_state, xs):
        q_n, k_n, i_n, b_n = xs
        o = (q_n * jnp.exp(b_n)) @ s_state
        decay = jnp.exp(jnp.where(tri, b_n[..., :, None, :] - b_n[..., None, :, :], -jnp.inf))
        att = jnp.einsum('bhtd,bhsd,bhtsd->bhts', q_n, k_n, decay)
        o = o + att @ i_n
        blast = b_n[..., -1:, :]
        s_state = s_state * jnp.exp(blast)[..., 0, :, None] + jnp.einsum(
            'bhsd,bhsv->bhdv', k_n * jnp.exp(blast - b_n), i_n)
        return s_state, o

    s0 = jnp.zeros((bsz, h, dk, dv), F32)
    _, o = lax.scan(step, s0, (q, k, i, bc))
    return o.transpose(1, 0, 3, 2, 4).reshape(bsz, length, h, dv)


def _hgrn2_mixer(q_raw, f_fw, f_bw, i_raw, g_raw, lb_fw, lb_bw, norm_g):
    bsz, length, _ = q_raw.shape
    heads = lambda t: t.astype(F32).reshape(bsz, length, H_GROUP, HEAD_DIM)
    q = jax.nn.silu(heads(q_raw))
    i = heads(i_raw)

    def log_forget(z, lb):
        lb = lb.reshape(H_GROUP, HEAD_DIM)
        return jnp.logaddexp(jnp.log(lb), jnp.log1p(-lb) + jax.nn.log_sigmoid(heads(z)))

    o_fw = _hgrn2_chunked(q, log_forget(f_fw, lb_fw), i)
    o_bw = _flip_seq(_hgrn2_chunked(_flip_seq(q), _flip_seq(log_forget(f_bw, lb_bw)), _flip_seq(i)))
    o = _rmsnorm(o_fw + o_bw, norm_g) * jax.nn.silu(heads(g_raw))
    return o.reshape(bsz, length, GROUP_W).astype(q_raw.dtype)


MLA_QK = 2 * MLA_NOPE
ROPE_SWAP = np.concatenate([np.arange(MLA_ROPE // 2, MLA_ROPE), np.arange(MLA_ROPE // 2)])


def _rope_table(length):
    half = MLA_ROPE // 2
    inv = ROPE_BASE ** (-jnp.arange(half, dtype=F32) / half)
    ang = jnp.arange(length, dtype=F32)[:, None] * inv[None, :]
    cos, sin = jnp.cos(ang), jnp.sin(ang)
    return jnp.concatenate([cos, cos, -sin, sin], axis=-1)


def _mla_q_weights(w_uq):
    w = w_uq.reshape(MLA_Q_LORA, H_GROUP, MLA_NOPE + MLA_ROPE)
    rope = w[:, :, MLA_NOPE:]
    w = jnp.concatenate([w[:, :, :MLA_NOPE], rope, rope[:, :, ROPE_SWAP]], axis=-1)
    return w.reshape(MLA_Q_LORA, H_GROUP * MLA_QK).astype(BF16)


def _plain_rmsnorm(x, g):
    return x * lax.rsqrt(jnp.mean(x * x, axis=-1, keepdims=True) + EPS) * g


def _mla_proj_kernel(cq_ref, ckv_ref, side_ref, tab_ref, qn_ref, kvn_ref, wq_ref, wkv_ref,
                     q_out, k_out, v_out):
    scale = (MLA_NOPE + MLA_ROPE) ** -0.5
    cqn = _plain_rmsnorm(cq_ref[...], qn_ref[...]).astype(BF16)
    ckvn = _plain_rmsnorm(ckv_ref[...], kvn_ref[...]).astype(BF16)
    q = jnp.dot(cqn, wq_ref[...], preferred_element_type=F32)
    kv = jnp.dot(ckvn, wkv_ref[...], preferred_element_type=F32)
    tab = tab_ref[...]
    kr = (side_ref[...] * tab).astype(BF16)
    for h in range(H_GROUP):
        c0 = h * MLA_QK
        qr = q[:, c0 + MLA_NOPE:c0 + MLA_QK] * tab
        qr = qr + pltpu.roll(qr, MLA_ROPE, axis=1)
        q_out[h, :, 0:MLA_NOPE] = (q[:, c0:c0 + MLA_NOPE] * scale).astype(BF16)
        q_out[h, :, MLA_NOPE:MLA_QK] = (qr * scale).astype(BF16)
        k_out[h, :, 0:MLA_NOPE] = kv[:, c0:c0 + MLA_NOPE].astype(BF16)
        k_out[h, :, MLA_NOPE:MLA_QK] = kr
        v_out[h] = kv[:, c0 + MLA_NOPE:c0 + MLA_QK].astype(BF16)


def _mla_attn_kernel(q_ref, k_ref, v_ref, o_ref):
    tq = q_ref.shape[0]
    length = k_ref.shape[0]
    tk = min(length, 2048)
    q = q_ref[...]
    m = l = acc = None
    for kb in range(length // tk):
        s = lax.dot_general(q, k_ref[kb * tk:(kb + 1) * tk, :], (((1,), (1,)), ((), ())),
                            preferred_element_type=F32)
        m_blk = jnp.max(s, axis=-1, keepdims=True)
        if kb == 0:
            m = m_blk
            p = jnp.exp(s - m)
            l = jnp.sum(p, axis=-1, keepdims=True)
            acc = jnp.dot(p.astype(BF16), v_ref[kb * tk:(kb + 1) * tk, :], preferred_element_type=F32)
        else:
            m_new = jnp.maximum(m, m_blk)
            alpha = jnp.exp(m - m_new)
            p = jnp.exp(s - m_new)
            l = alpha * l + jnp.sum(p, axis=-1, keepdims=True)
            acc = alpha * acc + jnp.dot(p.astype(BF16), v_ref[kb * tk:(kb + 1) * tk, :],
                                        preferred_element_type=F32)
            m = m_new
    o_ref[...] = acc / l


def _mla_call(main, side, q_norm, w_uq, kv_norm, w_ukv):
    bsz, length, _ = main.shape
    tm = min(512, length)
    lora = MLA_Q_LORA
    q_all, k_all, v_all = pl.pallas_call(
        _mla_proj_kernel,
        out_shape=(jax.ShapeDtypeStruct((bsz, H_GROUP, length, MLA_QK), BF16),
                   jax.ShapeDtypeStruct((bsz, H_GROUP, length, MLA_QK), BF16),
                   jax.ShapeDtypeStruct((bsz, H_GROUP, length, MLA_V), BF16)),
        grid=(bsz, length // tm),
        in_specs=[
            pl.BlockSpec((None, tm, lora), lambda b, i: (b, i, 10)),
            pl.BlockSpec((None, tm, lora), lambda b, i: (b, i, 11)),
            pl.BlockSpec((None, tm, SIDE_W), lambda b, i: (b, i, 0)),
            pl.BlockSpec((tm, SIDE_W), lambda b, i: (i, 0)),
            pl.BlockSpec((1, lora), lambda b, i: (0, 0)),
            pl.BlockSpec((1, lora), lambda b, i: (0, 0)),
            pl.BlockSpec((lora, H_GROUP * MLA_QK), lambda b, i: (0, 0)),
            pl.BlockSpec((lora, H_GROUP * MLA_QK), lambda b, i: (0, 0)),
        ],
        out_specs=(pl.BlockSpec((None, H_GROUP, tm, MLA_QK), lambda b, i: (b, 0, i, 0)),
                   pl.BlockSpec((None, H_GROUP, tm, MLA_QK), lambda b, i: (b, 0, i, 0)),
                   pl.BlockSpec((None, H_GROUP, tm, MLA_V), lambda b, i: (b, 0, i, 0))),
        compiler_params=_params(("arbitrary", "arbitrary")),
        name="mla_proj",
    )(main, main, side, _rope_table(length), q_norm.reshape(1, lora), kv_norm.reshape(1, lora),
      _mla_q_weights(w_uq), w_ukv.astype(BF16))

    tq = 256
    return pl.pallas_call(
        _mla_attn_kernel,
        out_shape=jax.ShapeDtypeStruct((bsz, length, H_GROUP * MLA_V), F32),
        grid=(bsz, H_GROUP, length // tq),
        in_specs=[
            pl.BlockSpec((None, None, tq, MLA_QK), lambda b, h, i: (b, h, i, 0)),
            pl.BlockSpec((None, None, length, MLA_QK), lambda b, h, i: (b, h, 0, 0)),
            pl.BlockSpec((None, None, length, MLA_V), lambda b, h, i: (b, h, 0, 0)),
        ],
        out_specs=pl.BlockSpec((None, tq, MLA_V), lambda b, h, i: (b, i, h)),
        compiler_params=_params(("arbitrary", "arbitrary", "arbitrary")),
        name="mla_attn",
    )(q_all, k_all, v_all)


def _mixer_ab(main, side, rpb, conv_w, a_log, dt_bias, dn_norm):
    _, _, _, qkv_b, z = _split_sizes(main, AB_SPLIT[:5])
    a_fw, a_bw, b_fw, b_bw = _split_sizes(side[..., :4 * H_GROUP], AB_SPLIT[5:])
    oa = _na_call(main, rpb)
    ob = _gated_deltanet(qkv_b, z, a_fw, a_bw, b_fw, b_bw, conv_w, a_log, dt_bias, dn_norm)
    return oa, ob


def _mixer_cd(main, side, lb_fw, lb_bw, hg_norm, q_norm, w_uq, kv_norm, w_ukv):
    q, f_fw, f_bw, i, g, _, _ = _split_sizes(main, CD_SPLIT[:7])
    oc = _hgrn2_mixer(q, f_fw, f_bw, i, g, lb_fw, lb_bw, hg_norm)
    od = _mla_call(main, side, q_norm, w_uq, kv_norm, w_ukv)
    return oc, od


def _pad_side(w):
    return jnp.pad(w, ((0, 0), (0, SIDE_W - w.shape[1]))).astype(BF16)


def _trunk(x, mods, lb, p):
    for l in range(DEPTH):
        mod3 = mods[l]
        if l % 2 == 0:
            e = l // 2
            main, side = _inproj_call(x, mod3, p["norm_mix"][l], p["w_in_ab_main"][e],
                                      p["w_in_ab_side"][e])
            ya, yb = _mixer_ab(main, side, p["na_rpb"][e], p["dn_conv"][e], p["dn_a_log"][e],
                               p["dn_dt_bias"][e], p["dn_norm"][e])
            w_out = p["w_out_ab"][e]
        else:
            o = l // 2
            main, side = _inproj_call(x, mod3, p["norm_mix"][l], p["w_in_cd_main"][o],
                                      p["w_in_cd_side"][o])
            ya, yb = _mixer_cd(main, side, lb[0, l], lb[1, l], p["hg_norm"][o],
                               p["mla_q_norm"][o], p["mla_w_uq"][o], p["mla_kv_norm"][o],
                               p["mla_w_ukv"][o])
            w_out = p["w_out_cd"][o]
        x = _outproj_call(ya, yb, w_out, x, mod3)
        x = _ffn_call(x, mod3, p["norm_ffn"][l], p["ffn_w_gate"][l], p["ffn_w_up"][l],
                      p["ffn_conv"][l], p["ffn_w_down"][l], p["final_norm"], l == DEPTH - 1)
    return x


def kernel(x_prompt, x_sample, c_prompt, c_sample, norm_mix, norm_ffn, w_ada, b_ada, w_in_ab, w_out_ab, na_rpb, dn_conv, dn_a_log, dn_dt_bias, dn_norm, w_in_cd, w_out_cd, hg_lower_bounds, hg_norm, mla_q_norm, mla_w_uq, mla_kv_norm, mla_w_ukv, ffn_w_gate, ffn_w_up, ffn_conv, ffn_w_down, final_norm):
    n_p, n_s = c_prompt.shape[0], c_sample.shape[0]
    rows = -(-(n_p + n_s) // 8) * 8
    c_all = jnp.concatenate([c_prompt, c_sample, jnp.zeros((rows - n_p - n_s, D_MODEL), F32)], axis=0)
    mod_all = _ada_call(c_all, w_ada, b_ada)
    mods_p = [mod_all[l, :n_p].reshape(n_p, 1, 6 * D_MODEL) for l in range(DEPTH)]
    mods_s = [mod_all[l, n_p:n_p + n_s].reshape(n_s, 1, 6 * D_MODEL) for l in range(DEPTH)]

    lb = jnp.cumsum(jax.nn.softmax(hg_lower_bounds.astype(F32), axis=1), axis=1)
    lb = lb - lb[:, :1]

    p = dict(
        norm_mix=norm_mix, norm_ffn=norm_ffn, na_rpb=na_rpb, dn_conv=dn_conv, dn_a_log=dn_a_log,
        dn_dt_bias=dn_dt_bias, dn_norm=dn_norm, hg_norm=hg_norm, mla_q_norm=mla_q_norm,
        mla_w_uq=mla_w_uq, mla_kv_norm=mla_kv_norm, mla_w_ukv=mla_w_ukv, ffn_conv=ffn_conv,
        final_norm=final_norm,
        w_in_ab_main=w_in_ab[:, :, :AB_MAIN].astype(BF16),
        w_in_ab_side=jax.vmap(_pad_side)(w_in_ab[:, :, AB_MAIN:]),
        w_in_cd_main=w_in_cd[:, :, :CD_MAIN].astype(BF16),
        w_in_cd_side=jnp.concatenate([w_in_cd[:, :, CD_MAIN:],
                                      w_in_cd[:, :, CD_MAIN:][:, :, ROPE_SWAP]], axis=-1).astype(BF16),
        w_out_ab=w_out_ab.astype(BF16), w_out_cd=w_out_cd.astype(BF16),
        ffn_w_gate=ffn_w_gate.astype(BF16), ffn_w_up=ffn_w_up.astype(BF16),
        ffn_w_down=ffn_w_down.astype(BF16),
    )
    y_prompt = _trunk(x_prompt, mods_p, lb, p)
    y_sample = _trunk(x_sample, mods_s, lb, p)
    return (y_prompt, y_sample)
```

```python
import functools
import math

import numpy as np
import jax
import jax.numpy as jnp
from jax import lax
from jax.experimental import pallas as pl
from jax.experimental.pallas import tpu as pltpu

D_MODEL = 2048
DEPTH = 4
HEAD_DIM = 128
H_GROUP = 8
GROUP_W = H_GROUP * HEAD_DIM
GRID_W = 64
NA_WIN_R = 8
NA_WIN_C = 16
NA_QCB = NA_WIN_C
NA_BAND = 2 * NA_WIN_C
DN_CONV_K = 5
DN_CHUNK = 64
HG_CHUNK = 32
MLA_Q_LORA = 512
MLA_KV_LORA = 512
MLA_NOPE = 128
MLA_ROPE = 64
MLA_V = 128
MLA_BLOCK_Q = 128
ROPE_BASE = 10000.0
D_FF = 5632
FFN_CONV_K = 3
EPS = 1e-6
AB_SPLIT = [GROUP_W, GROUP_W, GROUP_W, 3 * GROUP_W, GROUP_W, H_GROUP, H_GROUP, H_GROUP, H_GROUP]
CD_SPLIT = [GROUP_W, GROUP_W, GROUP_W, GROUP_W, GROUP_W, MLA_Q_LORA, MLA_KV_LORA, MLA_ROPE]
AB_MAIN = 7 * GROUP_W
CD_MAIN = 6 * GROUP_W
SIDE_W = 128

F32 = jnp.float32
BF16 = jnp.bfloat16

VMEM_LIMIT_BYTES = 56 * 1024 * 1024
HALO = 16


def _params(sem):
    return pltpu.CompilerParams(dimension_semantics=sem, vmem_limit_bytes=VMEM_LIMIT_BYTES)


def _ada_kernel(c_ref, w_ref, b_ref, o_ref):
    c = c_ref[...]
    cond = c * jax.nn.sigmoid(c)
    o_ref[...] = jnp.dot(cond, w_ref[...], precision=lax.Precision.HIGHEST,
                         preferred_element_type=F32) + b_ref[...]


def _ada_call(c_all, w_ada, b_ada):
    rows = c_all.shape[0]
    tn = 1024
    return pl.pallas_call(
        _ada_kernel,
        out_shape=jax.ShapeDtypeStruct((DEPTH, rows, 6 * D_MODEL), F32),
        grid=(DEPTH, 6 * D_MODEL // tn),
        in_specs=[
            pl.BlockSpec((rows, D_MODEL), lambda l, j: (0, 0)),
            pl.BlockSpec((None, D_MODEL, tn), lambda l, j: (l, 0, j)),
            pl.BlockSpec((None, 1, tn), lambda l, j: (l, 0, j)),
        ],
        out_specs=pl.BlockSpec((None, rows, tn), lambda l, j: (l, 0, j)),
        compiler_params=_params(("arbitrary", "arbitrary")),
        name="ada_mod",
    )(c_all, w_ada, b_ada.reshape(DEPTH, 1, 6 * D_MODEL))


def _norm_mod(x, g, sh, sc):
    y = x * lax.rsqrt(jnp.mean(x * x, axis=-1, keepdims=True) + EPS)
    return (y * g) * (1.0 + sc) + sh


def _inproj_kernel(x_ref, g_ref, sh_ref, sc_ref, w_ref, ws_ref, o_ref, os_ref, h_ref):
    @pl.when(pl.program_id(2) == 0)
    def _():
        hb = _norm_mod(x_ref[...], g_ref[...], sh_ref[...], sc_ref[...]).astype(BF16)
        h_ref[...] = hb
        os_ref[...] = jnp.dot(hb, ws_ref[...], preferred_element_type=F32)

    o_ref[...] = jnp.dot(h_ref[...], w_ref[...], preferred_element_type=F32)


def _inproj_call(x, mod3, norm_g, w_main, w_side):
    bsz, length, d = x.shape
    n_main = w_main.shape[1]
    tm = min(1024, length)
    tn = 1024
    return pl.pallas_call(
        _inproj_kernel,
        out_shape=(jax.ShapeDtypeStruct((bsz, length, n_main), F32),
                   jax.ShapeDtypeStruct((bsz, length, SIDE_W), F32)),
        grid=(bsz, length // tm, n_main // tn),
        in_specs=[
            pl.BlockSpec((None, tm, d), lambda b, i, j: (b, i, 0)),
            pl.BlockSpec((1, d), lambda b, i, j: (0, 0)),
            pl.BlockSpec((None, 1, d), lambda b, i, j: (b, 0, 0)),
            pl.BlockSpec((None, 1, d), lambda b, i, j: (b, 0, 1)),
            pl.BlockSpec((d, tn), lambda b, i, j: (0, j)),
            pl.BlockSpec((d, SIDE_W), lambda b, i, j: (0, 0)),
        ],
        out_specs=(pl.BlockSpec((None, tm, tn), lambda b, i, j: (b, i, j)),
                   pl.BlockSpec((None, tm, SIDE_W), lambda b, i, j: (b, i, 0))),
        scratch_shapes=[pltpu.VMEM((tm, d), BF16)],
        compiler_params=_params(("arbitrary", "arbitrary", "arbitrary")),
        name="inproj",
    )(x, norm_g.reshape(1, d), mod3, mod3, w_main, w_side)


def _outproj_kernel(ya_ref, yb_ref, w_ref, x_ref, gate_ref, o_ref):
    half = ya_ref.shape[-1]
    y = jnp.dot(ya_ref[...].astype(BF16), w_ref[0:half, :], preferred_element_type=F32)
    y = y + jnp.dot(yb_ref[...].astype(BF16), w_ref[half:, :], preferred_element_type=F32)
    o_ref[...] = x_ref[...] + gate_ref[...] * y


def _outproj_call(ya, yb, w_out, x, mod3):
    bsz, length, d = x.shape
    half = ya.shape[-1]
    tm = min(512, length)
    return pl.pallas_call(
        _outproj_kernel,
        out_shape=jax.ShapeDtypeStruct(x.shape, F32),
        grid=(bsz, length // tm),
        in_specs=[
            pl.BlockSpec((None, tm, half), lambda b, i: (b, i, 0)),
            pl.BlockSpec((None, tm, half), lambda b, i: (b, i, 0)),
            pl.BlockSpec((2 * half, d), lambda b, i: (0, 0)),
            pl.BlockSpec((None, tm, d), lambda b, i: (b, i, 0)),
            pl.BlockSpec((None, 1, d), lambda b, i: (b, 0, 2)),
        ],
        out_specs=pl.BlockSpec((None, tm, d), lambda b, i: (b, i, 0)),
        compiler_params=_params(("arbitrary", "arbitrary")),
        name="outproj",
    )(ya, yb, w_out, x, mod3)


def _ffn_kernel(final, xp_ref, x_ref, xn_ref, g_ref, sh_ref, sc_ref, gate_ref, wg_ref, wu_ref,
                cw_ref, wd_ref, fn_ref, o_ref, h_ref, gp_ref):
    i = pl.program_id(1)
    j = pl.program_id(2)
    tm = x_ref.shape[0]

    @pl.when(j == 0)
    def _():
        g, sh, sc = g_ref[...], sh_ref[...], sc_ref[...]
        h_ref[HALO:HALO + tm, :] = _norm_mod(x_ref[...], g, sh, sc).astype(BF16)
        hp = _norm_mod(xp_ref[...], g, sh, sc)
        h_ref[0:HALO, :] = jnp.where(i == 0, 0.0, hp).astype(BF16)
        hn = _norm_mod(xn_ref[...], g, sh, sc)
        h_ref[HALO + tm:, :] = jnp.where(i == pl.num_programs(1) - 1, 0.0, hn).astype(BF16)
        o_ref[...] = jnp.zeros_like(o_ref)

    gp_ref[...] = jnp.dot(h_ref[...], wg_ref[...], preferred_element_type=F32)
    cw = cw_ref[...]
    a = gp_ref[pl.ds(HALO - 1, tm), :] * cw[0:1, :]
    a = a + gp_ref[pl.ds(HALO, tm), :] * cw[1:2, :]
    a = a + gp_ref[pl.ds(HALO + 1, tm), :] * cw[2:3, :]
    u = jnp.dot(h_ref[HALO:HALO + tm, :], wu_ref[...], preferred_element_type=F32)
    act = (a * jax.nn.sigmoid(a)) * u
    o_ref[...] += jnp.dot(act.astype(BF16), wd_ref[...], preferred_element_type=F32)

    @pl.when(j == pl.num_programs(2) - 1)
    def _():
        xo = x_ref[...] + gate_ref[...] * o_ref[...]
        if final:
            xo = xo * lax.rsqrt(jnp.mean(xo * xo, axis=-1, keepdims=True) + EPS) * fn_ref[...]
        o_ref[...] = xo


def _ffn_call(x, mod3, norm_g, w_gate, w_up, conv_w, w_down, final_g, final):
    bsz, length, d = x.shape
    tm = min(512, length)
    tf = 512
    nh = tm // HALO
    last_h = length // HALO - 1
    return pl.pallas_call(
        functools.partial(_ffn_kernel, final),
        out_shape=jax.ShapeDtypeStruct(x.shape, F32),
        grid=(bsz, length // tm, D_FF // tf),
        in_specs=[
            pl.BlockSpec((None, HALO, d), lambda b, i, j: (b, jnp.maximum(i * nh - 1, 0), 0)),
            pl.BlockSpec((None, tm, d), lambda b, i, j: (b, i, 0)),
            pl.BlockSpec((None, HALO, d), lambda b, i, j: (b, jnp.minimum((i + 1) * nh, last_h), 0)),
            pl.BlockSpec((1, d), lambda b, i, j: (0, 0)),
            pl.BlockSpec((None, 1, d), lambda b, i, j: (b, 0, 3)),
            pl.BlockSpec((None, 1, d), lambda b, i, j: (b, 0, 4)),
            pl.BlockSpec((None, 1, d), lambda b, i, j: (b, 0, 5)),
            pl.BlockSpec((d, tf), lambda b, i, j: (0, j)),
            pl.BlockSpec((d, tf), lambda b, i, j: (0, j)),
            pl.BlockSpec((FFN_CONV_K, tf), lambda b, i, j: (0, j)),
            pl.BlockSpec((tf, d), lambda b, i, j: (j, 0)),
            pl.BlockSpec((1, d), lambda b, i, j: (0, 0)),
        ],
        out_specs=pl.BlockSpec((None, tm, d), lambda b, i, j: (b, i, 0)),
        scratch_shapes=[pltpu.VMEM((tm + 2 * HALO, d), BF16),
                        pltpu.VMEM((tm + 2 * HALO, tf), F32)],
        compiler_params=_params(("arbitrary", "arbitrary", "arbitrary")),
        name="conv_ffn",
    )(x, x, x, norm_g.reshape(1, d), mod3, mod3, mod3, w_gate, w_up, conv_w, w_down,
      final_g.reshape(1, d))


NA_KEYS = NA_WIN_R * GRID_W
MASKED = -1e30
NA_ROWS_PER_STEP = 4


def _na_bias_table(rpb):
    qc = np.arange(GRID_W)[:, None]
    kc = np.arange(GRID_W)[None, :]
    cs = np.clip(qc - NA_WIN_C // 2, 0, GRID_W - NA_WIN_C)
    valid = (kc >= cs) & (kc < cs + NA_WIN_C)
    dc = np.clip(kc - qc + NA_WIN_C - 1, 0, 2 * NA_WIN_C - 2)
    dr = np.arange(NA_WIN_R)[None, :] - np.arange(NA_WIN_R)[:, None] + NA_WIN_R - 1
    t = rpb.astype(F32)[:, dr][:, :, :, dc]
    t = jnp.where(valid[None, None, None], t, MASKED)
    return t.transpose(0, 1, 3, 2, 4).reshape(rpb.shape[0], NA_WIN_R, GRID_W, NA_KEYS)


def _na_kernel(q_ref, k_ref, v_ref, bias_ref, o_ref, kb_ref, vb_ref):
    rows = q_ref.shape[0] // GRID_W
    kb_ref[...] = k_ref[...].astype(BF16)
    vb_ref[...] = v_ref[...].astype(BF16)

    def row_group(g, carry):
        todo = []
        for u in range(NA_ROWS_PER_STEP):
            r = g * NA_ROWS_PER_STEP + u
            rs = jnp.clip(r - NA_WIN_R // 2, 0, rows - NA_WIN_R)
            q0 = pl.multiple_of(r * GRID_W, GRID_W)
            k0 = pl.multiple_of(rs * GRID_W, GRID_W)
            q = (q_ref[pl.ds(q0, GRID_W), :] * (HEAD_DIM ** -0.5)).astype(BF16)
            s = lax.dot_general(q, kb_ref[pl.ds(k0, NA_KEYS), :], (((1,), (1,)), ((), ())),
                                preferred_element_type=F32)
            todo.append((q0, k0, s, r - rs))
        for u, (q0, k0, s, off) in enumerate(todo):
            s = s + bias_ref[off]
            p = jnp.exp(s - jnp.max(s, axis=-1, keepdims=True))
            todo[u] = (q0, k0, p.astype(BF16), jnp.sum(p, axis=-1, keepdims=True))
        for u, (q0, k0, p, l) in enumerate(todo):
            todo[u] = (q0, jnp.dot(p, vb_ref[pl.ds(k0, NA_KEYS), :], preferred_element_type=F32), l)
        for q0, o, l in todo:
            o_ref[pl.ds(q0, GRID_W), :] = o / l
        return carry

    lax.fori_loop(0, rows // NA_ROWS_PER_STEP, row_group, 0)


def _na_call(main, rpb):
    bsz, length, _ = main.shape
    assert length % GRID_W == 0 and length // GRID_W >= NA_WIN_R
    hd = HEAD_DIM
    return pl.pallas_call(
        _na_kernel,
        out_shape=jax.ShapeDtypeStruct((bsz, length, GROUP_W), F32),
        grid=(bsz, H_GROUP),
        in_specs=[
            pl.BlockSpec((None, length, hd), lambda b, h: (b, 0, h)),
            pl.BlockSpec((None, length, hd), lambda b, h: (b, 0, H_GROUP + h)),
            pl.BlockSpec((None, length, hd), lambda b, h: (b, 0, 2 * H_GROUP + h)),
            pl.BlockSpec((None, NA_WIN_R, GRID_W, NA_KEYS), lambda b, h: (h, 0, 0, 0)),
        ],
        out_specs=pl.BlockSpec((None, length, hd), lambda b, h: (b, 0, h)),
        scratch_shapes=[pltpu.VMEM((length, hd), BF16), pltpu.VMEM((length, hd), BF16)],
        compiler_params=_params(("arbitrary", "arbitrary")),
        name="nbr_attn",
    )(main, main, main, _na_bias_table(rpb))


REC_TILE = 512
REC_HEADS = 8
HIGHEST = lax.Precision.HIGHEST


def _tri(n, upper):
    r = lax.broadcasted_iota(jnp.int32, (n, n), 0)
    c = lax.broadcasted_iota(jnp.int32, (n, n), 1)
    return (c >= r) if upper else (c <= r)


def _gated_norm_kernel(of_ref, ob_ref, z_ref, g_ref, o_ref):
    g = g_ref[...]
    for h in range(H_GROUP):
        sl = slice(h * HEAD_DIM, (h + 1) * HEAD_DIM)
        o = of_ref[:, sl] + ob_ref[:, sl]
        o = o * lax.rsqrt(jnp.mean(o * o, axis=-1, keepdims=True) + EPS) * g
        z = z_ref[:, sl]
        o_ref[:, sl] = o * (z * jax.nn.sigmoid(z))


def _gated_norm_call(o_fw, o_bw, main, gate_group, norm_g):
    bsz, length, w = o_fw.shape
    tm = min(512, length)
    blk = lambda col: pl.BlockSpec((None, tm, w), lambda b, i: (b, i, col))
    return pl.pallas_call(
        _gated_norm_kernel,
        out_shape=jax.ShapeDtypeStruct(o_fw.shape, F32),
        grid=(bsz, length // tm),
        in_specs=[blk(0), blk(0), blk(gate_group), pl.BlockSpec((1, HEAD_DIM), lambda b, i: (0, 0))],
        out_specs=blk(0),
        compiler_params=_params(("arbitrary", "arbitrary")),
        name="gated_norm",
    )(o_fw, o_bw, main, norm_g.reshape(1, HEAD_DIM))


def _gdn_conv_kernel(x_ref, w_ref, o_ref, xp_ref):
    j = pl.program_id(1)
    length = x_ref.shape[0]
    pad = 8
    xp_ref[0:pad, :] = jnp.zeros((pad, HEAD_DIM), F32)
    xp_ref[pad + length:, :] = jnp.zeros((pad, HEAD_DIM), F32)
    xp_ref[pad:pad + length, :] = x_ref[...]
    w = w_ref[...]
    y = xp_ref[pl.ds(pad - DN_CONV_K // 2, length), :] * w[0:1, :]
    for t in range(1, DN_CONV_K):
        y = y + xp_ref[pl.ds(pad - DN_CONV_K // 2 + t, length), :] * w[t:t + 1, :]
    y = y * jax.nn.sigmoid(y)
    n = y * lax.rsqrt(jnp.sum(y * y, axis=-1, keepdims=True) + EPS)
    n = n * jnp.where(j < H_GROUP, HEAD_DIM ** -0.5, 1.0)
    o_ref[...] = jnp.where(j < 2 * H_GROUP, n, y)


def _gdn_conv_call(main, conv_w):
    bsz, length, _ = main.shape
    nblk = 3 * H_GROUP
    return pl.pallas_call(
        _gdn_conv_kernel,
        out_shape=jax.ShapeDtypeStruct((bsz, length, 3 * GROUP_W), F32),
        grid=(bsz, nblk),
        in_specs=[pl.BlockSpec((None, length, HEAD_DIM), lambda b, j: (b, 0, nblk + j)),
                  pl.BlockSpec((DN_CONV_K, HEAD_DIM), lambda b, j: (0, j))],
        out_specs=pl.BlockSpec((None, length, HEAD_DIM), lambda b, j: (b, 0, j)),
        scratch_shapes=[pltpu.VMEM((length + 16, HEAD_DIM), F32)],
        compiler_params=_params(("arbitrary", "arbitrary")),
        name="gdn_conv",
    )(main, conv_w)


def _gdn_gates_kernel(s_ref, alog_ref, dt_ref, g_ref, gt_ref):
    tm = s_ref.shape[0]
    s = s_ref[...]
    x = s + dt_ref[...]
    softplus = jnp.maximum(x, 0.0) + jnp.log1p(jnp.exp(-jnp.abs(x)))
    g = -jnp.exp(alog_ref[...]) * softplus
    r = lax.broadcasted_iota(jnp.int32, (tm, tm), 0)
    c = lax.broadcasted_iota(jnp.int32, (tm, tm), 1)
    same = (r // DN_CHUNK) == (c // DN_CHUNK)
    lo = jnp.where(same & (c <= r), 1.0, 0.0)
    up = jnp.where(same & (c >= r), 1.0, 0.0)
    gcf = jnp.dot(lo, g, precision=HIGHEST, preferred_element_type=F32)
    gcb = jnp.dot(up, g, precision=HIGHEST, preferred_element_type=F32)
    lane = lax.broadcasted_iota(jnp.int32, s.shape, 1)
    out = jnp.where(lane < H_GROUP, gcf, jnp.where(lane < 2 * H_GROUP, gcb, jax.nn.sigmoid(s)))
    g_ref[...] = out
    for p in range(tm // 128):
        gt_ref[p] = out[p * 128:(p + 1) * 128, :].T[0:4 * H_GROUP, :]


def _gdn_gates_call(side, a_log, dt_bias):
    bsz, length, _ = side.shape
    tm = min(512, length)
    row = lambda t: jnp.pad(t.reshape(1, 2 * H_GROUP).astype(F32), ((0, 0), (0, SIDE_W - 2 * H_GROUP)))
    return pl.pallas_call(
        _gdn_gates_kernel,
        out_shape=(jax.ShapeDtypeStruct((bsz, length, SIDE_W), F32),
                   jax.ShapeDtypeStruct((bsz, length // 128, 4 * H_GROUP, 128), F32)),
        grid=(bsz, length // tm),
        in_specs=[pl.BlockSpec((None, tm, SIDE_W), lambda b, i: (b, i, 0)),
                  pl.BlockSpec((1, SIDE_W), lambda b, i: (0, 0)),
                  pl.BlockSpec((1, SIDE_W), lambda b, i: (0, 0))],
        out_specs=(pl.BlockSpec((None, tm, SIDE_W), lambda b, i: (b, i, 0)),
                   pl.BlockSpec((None, tm // 128, 4 * H_GROUP, 128), lambda b, i: (b, i, 0, 0))),
        compiler_params=_params(("arbitrary", "arbitrary")),
        name="gdn_gates",
    )(side, row(a_log), row(dt_bias))


def _bdot(a, b):
    return jnp.dot(a.astype(BF16), b.astype(BF16), preferred_element_type=F32)


def _gdn_chunks_local(chunks):
    c = DN_CHUNK
    r = lax.broadcasted_iota(jnp.int32, (c, c), 0)
    cc = lax.broadcasted_iota(jnp.int32, (c, c), 1)
    for d in chunks:
        incl = (cc >= r) if d["rev"] else (cc <= r)
        d["gam"] = jnp.exp(jnp.where(incl, d["gci"] - d["gcj"], MASKED))
        d["kb"] = d["k"] * d["beta"]
    for d in chunks:
        kq = jnp.concatenate([d["kb"], d["q"]], axis=0).astype(BF16)
        d["prod"] = lax.dot_general(kq, d["k"].astype(BF16), (((1,), (1,)), ((), ())),
                                    preferred_element_type=F32)
    for d in chunks:
        strict = (cc > r) if d["rev"] else (cc < r)
        d["aqk"] = d["prod"][c:] * d["gam"]
        d["p"] = -jnp.where(strict, d["prod"][0:c] * d["gam"], 0.0)
        d["t"] = jnp.where(r == cc, 1.0, 0.0) + d["p"]
    for _ in range(int(math.log2(c)) - 1):
        for d in chunks:
            d["p"] = _bdot(d["p"], d["p"])
        for d in chunks:
            d["t"] = d["t"] + _bdot(d["t"], d["p"])
    for d in chunks:
        egc = jnp.exp(d["gci"])
        rhs = jnp.concatenate([d["v"] * d["beta"], d["kb"] * egc], axis=1)
        uw = _bdot(d["t"], rhs)
        d["u"], d["w"] = uw[:, 0:HEAD_DIM], uw[:, HEAD_DIM:]
        glast = d["gci"][0:1, :] if d["rev"] else d["gci"][c - 1:c, :]
        d["qd"] = d["q"] * egc
        d["kd"] = d["k"] * jnp.exp(glast - d["gci"])
        d["dlast"] = jnp.exp(glast)


def _gdn_chunks_scan(chunks):
    c = DN_CHUNK
    for d in chunks:
        d["s"] = d["s_ref"][...]
        d["ws"] = _bdot(jnp.concatenate([d["w"], d["qd"]], axis=0), d["s"])
    for d in chunks:
        d["vn"] = (d["u"] - d["ws"][0:c]).astype(BF16)
        d["o"] = d["ws"][c:] + _bdot(d["aqk"], d["vn"])
    for d in chunks:
        d["s_ref"][...] = d["s"] * d["dlast"] + lax.dot_general(
            d["kd"].astype(BF16), d["vn"], (((0,), (0,)), ((), ())), preferred_element_type=F32)


def _gdn_kernel(qf_ref, kf_ref, vf_ref, gf_ref, gtf_ref, qb_ref, kb_ref, vb_ref, gb_ref, gtb_ref,
                of_ref, ob_ref, s_ref):
    hg = pl.program_id(1)

    @pl.when(pl.program_id(2) == 0)
    def _():
        s_ref[...] = jnp.zeros_like(s_ref)

    npair = qf_ref.shape[0] // 128
    c = DN_CHUNK

    def pair(p, carry):
        chunks = []
        for rev in (False, True):
            q_ref, k_ref, v_ref, g_ref, gt_ref, o_ref = (
                (qb_ref, kb_ref, vb_ref, gb_ref, gtb_ref, ob_ref) if rev
                else (qf_ref, kf_ref, vf_ref, gf_ref, gtf_ref, of_ref))
            pp = (npair - 1 - p) if rev else p
            base = pl.multiple_of(pp * 128, 128)
            gt = gt_ref[pp]
            sub = lax.broadcasted_iota(jnp.int32, gt.shape, 0)
            for step, half in enumerate((1, 0) if rev else (0, 1)):
                rows = pl.ds(base + half * c, c)
                g = g_ref[rows, :]
                lane = lax.broadcasted_iota(jnp.int32, g.shape, 1)
                for hl in range(REC_HEADS):
                    cols = slice(hl * HEAD_DIM, (hl + 1) * HEAD_DIM)
                    gcol = hg * REC_HEADS + hl + (H_GROUP if rev else 0)
                    chunks.append(dict(
                        rev=rev, step=step, rows=rows, cols=cols, o_ref=o_ref,
                        s_ref=s_ref.at[int(rev), hl],
                        q=q_ref[rows, cols], k=k_ref[rows, cols], v=v_ref[rows, cols],
                        gci=jnp.sum(jnp.where(lane == gcol, g, 0.0), axis=-1, keepdims=True),
                        beta=jnp.sum(jnp.where(lane == gcol + 2 * H_GROUP, g, 0.0), axis=-1,
                                     keepdims=True),
                        gcj=jnp.sum(jnp.where(sub == gcol, gt, 0.0), axis=0,
                                    keepdims=True)[:, half * c:(half + 1) * c]))
        _gdn_chunks_local(chunks)
        for step in (0, 1):
            now = [d for d in chunks if d["step"] == step]
            _gdn_chunks_scan(now)
            for d in now:
                d["o_ref"][d["rows"], d["cols"]] = d["o"]
        return carry

    lax.fori_loop(0, npair, pair, 0)


def _gdn_call(main, side, conv_w, a_log, dt_bias, norm_g):
    bsz, length, _ = main.shape
    tl = min(REC_TILE, length)
    nt = length // tl
    w = REC_HEADS * HEAD_DIM
    ng = H_GROUP // REC_HEADS
    qkv = _gdn_conv_call(main, conv_w)
    gates, gates_t = _gdn_gates_call(side, a_log, dt_bias)

    def specs(tile):
        return [pl.BlockSpec((None, tl, w), lambda b, g, i: (b, tile(i), g)),
                pl.BlockSpec((None, tl, w), lambda b, g, i: (b, tile(i), ng + g)),
                pl.BlockSpec((None, tl, w), lambda b, g, i: (b, tile(i), 2 * ng + g)),
                pl.BlockSpec((None, tl, SIDE_W), lambda b, g, i: (b, tile(i), 0)),
                pl.BlockSpec((None, tl // 128, 4 * H_GROUP, 128), lambda b, g, i: (b, tile(i), 0, 0))]

    fw = lambda i: i
    bw = lambda i: nt - 1 - i
    o_fw, o_bw = pl.pallas_call(
        _gdn_kernel,
        out_shape=(jax.ShapeDtypeStruct((bsz, length, GROUP_W), F32),) * 2,
        grid=(bsz, ng, nt),
        in_specs=specs(fw) + specs(bw),
        out_specs=(pl.BlockSpec((None, tl, w), lambda b, g, i: (b, fw(i), g)),
                   pl.BlockSpec((None, tl, w), lambda b, g, i: (b, bw(i), g))),
        scratch_shapes=[pltpu.VMEM((2, REC_HEADS, HEAD_DIM, HEAD_DIM), F32)],
        compiler_params=_params(("arbitrary", "arbitrary", "arbitrary")),
        name="gdn_scan",
    )(qkv, qkv, qkv, gates, gates_t, qkv, qkv, qkv, gates, gates_t)
    return _gated_norm_call(o_fw, o_bw, main, 6, norm_g)


HG_TILE = 64
HG_SUB = 8
HG_NSUB = HG_TILE // HG_SUB


def _hg_chunks(chains):
    n, sub, nsub = HG_TILE, HG_SUB, HG_NSUB
    row = lax.broadcasted_iota(jnp.int32, (n, HEAD_DIM), 0)
    pos = row % sub
    for c in chains:
        z, loglb, log1mlb = c["z"], c["loglb"], c["log1mlb"]
        c["q"] = c["q_raw"] * jax.nn.sigmoid(c["q_raw"])
        log_sig = jnp.minimum(z, 0.0) - jnp.log1p(jnp.exp(-jnp.abs(z)))
        bb = log1mlb + log_sig
        lf = jnp.maximum(loglb, bb) + jnp.log1p(jnp.exp(-jnp.abs(loglb - bb)))
        c["k"] = jnp.exp(log1mlb) * jax.nn.sigmoid(-z)
        tri = jnp.where(_tri(n, c["rev"]), 1.0, 0.0)
        c["b"] = jnp.dot(tri, lf, precision=HIGHEST, preferred_element_type=F32)

    for c in chains:
        b, k, inp, rev = c["b"], c["k"], c["inp"], c["rev"]
        c["order"] = list(range(nsub - 1, -1, -1)) if rev else list(range(nsub))
        c["ends"], c["contrib"] = [], []
        for j in c["order"]:
            lo = j * sub
            bend = b[lo:lo + 1, :] if rev else b[lo + sub - 1:lo + sub, :]
            ks = k[lo:lo + sub, :] * jnp.exp(bend - b[lo:lo + sub, :])
            c["contrib"].append(lax.dot_general(inp[lo:lo + sub, :].astype(BF16), ks.astype(BF16),
                                                (((0,), (0,)), ((), ())), preferred_element_type=F32))
            c["ends"].append(bend)

    for c in chains:
        b, k, q, inp, rev = c["b"], c["k"], c["q"], c["inp"], c["rev"]
        o = jnp.zeros((n, HEAD_DIM), F32)
        for d in range(sub):
            shift = (sub - d) % sub if rev else d
            valid = (pos <= sub - 1 - d) if rev else (pos >= d)
            if d == 0:
                sh = lambda x: x
            else:
                sh = lambda x: pltpu.roll(x.reshape(nsub, sub, HEAD_DIM), shift, axis=1).reshape(
                    n, HEAD_DIM)
            e = jnp.where(valid, jnp.exp(jnp.minimum(b - sh(b), 0.0)), 0.0)
            att = jnp.sum(q * sh(k) * e, axis=-1, keepdims=True)
            o = o + att * sh(inp)
        c["o"] = o

    for c in chains:
        b, q, rev = c["b"], c["q"], c["rev"]
        c["s"] = c["s_ref"][...]
        qext = [q * jnp.exp(b)]
        for m in range(nsub - 1):
            j = c["order"][m]
            lo, hi = (0, j * sub) if rev else ((j + 1) * sub, n)
            part = q[lo:hi] * jnp.exp(jnp.minimum(b[lo:hi] - c["ends"][m], 0.0))
            zeros = jnp.zeros((n - (hi - lo), HEAD_DIM), F32)
            qext.append(jnp.concatenate([part, zeros] if rev else [zeros, part], axis=0))
        ncat = jnp.concatenate([c["s"]] + c["contrib"][:nsub - 1], axis=1).astype(BF16)
        c["o"] = c["o"] + lax.dot_general(jnp.concatenate(qext, axis=1).astype(BF16), ncat,
                                          (((1,), (1,)), ((), ())), preferred_element_type=F32)
    for c in chains:
        btot = c["ends"][-1]
        s_new = c["s"] * jnp.exp(btot)
        for m in range(nsub):
            s_new = s_new + c["contrib"][m] * jnp.exp(btot - c["ends"][m])
        c["s_ref"][...] = s_new


def _hgrn2_kernel(qf_ref, ff_ref, if_ref, qb_ref, fb_ref, ib_ref, lbf_ref, lbb_ref,
                  of_ref, ob_ref, s_ref):
    @pl.when(pl.program_id(2) == 0)
    def _():
        s_ref[...] = jnp.zeros_like(s_ref)

    nstep = qf_ref.shape[0] // HG_TILE

    def step(t, carry):
        chains = []
        for rev in (False, True):
            q_ref, f_ref, i_ref, lb_ref, o_ref = ((qb_ref, fb_ref, ib_ref, lbb_ref, ob_ref) if rev
                                                  else (qf_ref, ff_ref, if_ref, lbf_ref, of_ref))
            tt = (nstep - 1 - t) if rev else t
            rows = pl.ds(pl.multiple_of(tt * HG_TILE, HG_TILE), HG_TILE)
            for hl in range(REC_HEADS):
                cols = slice(hl * HEAD_DIM, (hl + 1) * HEAD_DIM)
                chains.append(dict(rev=rev, rows=rows, cols=cols, o_ref=o_ref,
                                   s_ref=s_ref.at[int(rev), hl], q_raw=q_ref[rows, cols],
                                   z=f_ref[rows, cols], inp=i_ref[rows, cols],
                                   loglb=lb_ref[0:1, cols], log1mlb=lb_ref[1:2, cols]))
        _hg_chunks(chains)
        for c in chains:
            c["o_ref"][c["rows"], c["cols"]] = c["o"]
        return carry

    lax.fori_loop(0, nstep, step, 0)


def _hgrn2_call(main, lb_fw, lb_bw, norm_g):
    bsz, length, _ = main.shape
    tl = min(REC_TILE, length)
    nt = length // tl
    w = REC_HEADS * HEAD_DIM
    ng = H_GROUP // REC_HEADS
    fw = lambda i: i
    bw = lambda i: nt - 1 - i
    col = lambda group, tile: pl.BlockSpec((None, tl, w), lambda b, g, i: (b, tile(i), group * ng + g))
    lbrow = pl.BlockSpec((2, w), lambda b, g, i: (0, g))
    logs = lambda lb: jnp.stack([jnp.log(lb), jnp.log1p(-lb)])
    o_fw, o_bw = pl.pallas_call(
        _hgrn2_kernel,
        out_shape=(jax.ShapeDtypeStruct((bsz, length, GROUP_W), F32),) * 2,
        grid=(bsz, ng, nt),
        in_specs=[col(0, fw), col(1, fw), col(3, fw), col(0, bw), col(2, bw), col(3, bw), lbrow, lbrow],
        out_specs=(pl.BlockSpec((None, tl, w), lambda b, g, i: (b, fw(i), g)),
                   pl.BlockSpec((None, tl, w), lambda b, g, i: (b, bw(i), g))),
        scratch_shapes=[pltpu.VMEM((2, REC_HEADS, HEAD_DIM, HEAD_DIM), F32)],
        compiler_params=_params(("arbitrary", "arbitrary", "arbitrary")),
        name="hgrn2_scan",
    )(main, main, main, main, main, main, logs(lb_fw), logs(lb_bw))
    return _gated_norm_call(o_fw, o_bw, main, 4, norm_g)


MLA_QK = 2 * MLA_NOPE
ROPE_SWAP = np.concatenate([np.arange(MLA_ROPE // 2, MLA_ROPE), np.arange(MLA_ROPE // 2)])


def _rope_table(length):
    half = MLA_ROPE // 2
    inv = ROPE_BASE ** (-jnp.arange(half, dtype=F32) / half)
    ang = jnp.arange(length, dtype=F32)[:, None] * inv[None, :]
    cos, sin = jnp.cos(ang), jnp.sin(ang)
    return jnp.concatenate([cos, cos, -sin, sin], axis=-1)


def _mla_q_weights(w_uq):
    w = w_uq.reshape(MLA_Q_LORA, H_GROUP, MLA_NOPE + MLA_ROPE)
    rope = w[:, :, MLA_NOPE:]
    w = jnp.concatenate([w[:, :, :MLA_NOPE], rope, rope[:, :, ROPE_SWAP]], axis=-1)
    return w.reshape(MLA_Q_LORA, H_GROUP * MLA_QK).astype(BF16)


def _plain_rmsnorm(x, g):
    return x * lax.rsqrt(jnp.mean(x * x, axis=-1, keepdims=True) + EPS) * g


def _mla_proj_kernel(cq_ref, ckv_ref, side_ref, tab_ref, qn_ref, kvn_ref, wq_ref, wkv_ref,
                     q_out, k_out, v_out):
    scale = (MLA_NOPE + MLA_ROPE) ** -0.5
    cqn = _plain_rmsnorm(cq_ref[...], qn_ref[...]).astype(BF16)
    ckvn = _plain_rmsnorm(ckv_ref[...], kvn_ref[...]).astype(BF16)
    q = jnp.dot(cqn, wq_ref[...], preferred_element_type=F32)
    kv = jnp.dot(ckvn, wkv_ref[...], preferred_element_type=F32)
    tab = tab_ref[...]
    kr = (side_ref[...] * tab).astype(BF16)
    for h in range(H_GROUP):
        c0 = h * MLA_QK
        qr = q[:, c0 + MLA_NOPE:c0 + MLA_QK] * tab
        qr = qr + pltpu.roll(qr, MLA_ROPE, axis=1)
        q_out[h, :, 0:MLA_NOPE] = (q[:, c0:c0 + MLA_NOPE] * scale).astype(BF16)
        q_out[h, :, MLA_NOPE:MLA_QK] = (qr * scale).astype(BF16)
        k_out[h, :, 0:MLA_NOPE] = kv[:, c0:c0 + MLA_NOPE].astype(BF16)
        k_out[h, :, MLA_NOPE:MLA_QK] = kr
        v_out[h] = kv[:, c0 + MLA_NOPE:c0 + MLA_QK].astype(BF16)


def _mla_attn_kernel(q_ref, k_ref, v_ref, o_ref):
    tq = q_ref.shape[0]
    length = k_ref.shape[0]
    tk = min(length, 2048)
    q = q_ref[...]
    m = l = acc = None
    for kb in range(length // tk):
        s = lax.dot_general(q, k_ref[kb * tk:(kb + 1) * tk, :], (((1,), (1,)), ((), ())),
                            preferred_element_type=F32)
        m_blk = jnp.max(s, axis=-1, keepdims=True)
        if kb == 0:
            m = m_blk
            p = jnp.exp(s - m)
            l = jnp.sum(p, axis=-1, keepdims=True)
            acc = jnp.dot(p.astype(BF16), v_ref[kb * tk:(kb + 1) * tk, :], preferred_element_type=F32)
        else:
            m_new = jnp.maximum(m, m_blk)
            alpha = jnp.exp(m - m_new)
            p = jnp.exp(s - m_new)
            l = alpha * l + jnp.sum(p, axis=-1, keepdims=True)
            acc = alpha * acc + jnp.dot(p.astype(BF16), v_ref[kb * tk:(kb + 1) * tk, :],
                                        preferred_element_type=F32)
            m = m_new
    o_ref[...] = acc / l


def _mla_call(main, side, q_norm, w_uq, kv_norm, w_ukv):
    bsz, length, _ = main.shape
    tm = min(512, length)
    lora = MLA_Q_LORA
    q_all, k_all, v_all = pl.pallas_call(
        _mla_proj_kernel,
        out_shape=(jax.ShapeDtypeStruct((bsz, H_GROUP, length, MLA_QK), BF16),
                   jax.ShapeDtypeStruct((bsz, H_GROUP, length, MLA_QK), BF16),
                   jax.ShapeDtypeStruct((bsz, H_GROUP, length, MLA_V), BF16)),
        grid=(bsz, length // tm),
        in_specs=[
            pl.BlockSpec((None, tm, lora), lambda b, i: (b, i, 10)),
            pl.BlockSpec((None, tm, lora), lambda b, i: (b, i, 11)),
            pl.BlockSpec((None, tm, SIDE_W), lambda b, i: (b, i, 0)),
            pl.BlockSpec((tm, SIDE_W), lambda b, i: (i, 0)),
            pl.BlockSpec((1, lora), lambda b, i: (0, 0)),
            pl.BlockSpec((1, lora), lambda b, i: (0, 0)),
            pl.BlockSpec((lora, H_GROUP * MLA_QK), lambda b, i: (0, 0)),
            pl.BlockSpec((lora, H_GROUP * MLA_QK), lambda b, i: (0, 0)),
        ],
        out_specs=(pl.BlockSpec((None, H_GROUP, tm, MLA_QK), lambda b, i: (b, 0, i, 0)),
                   pl.BlockSpec((None, H_GROUP, tm, MLA_QK), lambda b, i: (b, 0, i, 0)),
                   pl.BlockSpec((None, H_GROUP, tm, MLA_V), lambda b, i: (b, 0, i, 0))),
        compiler_params=_params(("arbitrary", "arbitrary")),
        name="mla_proj",
    )(main, main, side, _rope_table(length), q_norm.reshape(1, lora), kv_norm.reshape(1, lora),
      _mla_q_weights(w_uq), w_ukv.astype(BF16))

    tq = 256
    return pl.pallas_call(
        _mla_attn_kernel,
        out_shape=jax.ShapeDtypeStruct((bsz, length, H_GROUP * MLA_V), F32),
        grid=(bsz, H_GROUP, length // tq),
        in_specs=[
            pl.BlockSpec((None, None, tq, MLA_QK), lambda b, h, i: (b, h, i, 0)),
            pl.BlockSpec((None, None, length, MLA_QK), lambda b, h, i: (b, h, 0, 0)),
            pl.BlockSpec((None, None, length, MLA_V), lambda b, h, i: (b, h, 0, 0)),
        ],
        out_specs=pl.BlockSpec((None, tq, MLA_V), lambda b, h, i: (b, i, h)),
        compiler_params=_params(("arbitrary", "arbitrary", "arbitrary")),
        name="mla_attn",
    )(q_all, k_all, v_all)


def _mixer_ab(main, side, rpb, conv_w, a_log, dt_bias, dn_norm):
    oa = _na_call(main, rpb)
    ob = _gdn_call(main, side, conv_w, a_log, dt_bias, dn_norm)
    return oa, ob


def _mixer_cd(main, side, lb_fw, lb_bw, hg_norm, q_norm, w_uq, kv_norm, w_ukv):
    oc = _hgrn2_call(main, lb_fw, lb_bw, hg_norm)
    od = _mla_call(main, side, q_norm, w_uq, kv_norm, w_ukv)
    return oc, od


def _pad_side(w):
    return jnp.pad(w, ((0, 0), (0, SIDE_W - w.shape[1]))).astype(BF16)


def _trunk(x, mods, lb, p):
    for l in range(DEPTH):
        mod3 = mods[l]
        if l % 2 == 0:
            e = l // 2
            main, side = _inproj_call(x, mod3, p["norm_mix"][l], p["w_in_ab_main"][e],
                                      p["w_in_ab_side"][e])
            ya, yb = _mixer_ab(main, side, p["na_rpb"][e], p["dn_conv"][e], p["dn_a_log"][e],
                               p["dn_dt_bias"][e], p["dn_norm"][e])
            w_out = p["w_out_ab"][e]
        else:
            o = l // 2
            main, side = _inproj_call(x, mod3, p["norm_mix"][l], p["w_in_cd_main"][o],
                                      p["w_in_cd_side"][o])
            ya, yb = _mixer_cd(main, side, lb[0, l], lb[1, l], p["hg_norm"][o],
                               p["mla_q_norm"][o], p["mla_w_uq"][o], p["mla_kv_norm"][o],
                               p["mla_w_ukv"][o])
            w_out = p["w_out_cd"][o]
        x = _outproj_call(ya, yb, w_out, x, mod3)
        x = _ffn_call(x, mod3, p["norm_ffn"][l], p["ffn_w_gate"][l], p["ffn_w_up"][l],
                      p["ffn_conv"][l], p["ffn_w_down"][l], p["final_norm"], l == DEPTH - 1)
    return x


def kernel(x_prompt, x_sample, c_prompt, c_sample, norm_mix, norm_ffn, w_ada, b_ada, w_in_ab, w_out_ab, na_rpb, dn_conv, dn_a_log, dn_dt_bias, dn_norm, w_in_cd, w_out_cd, hg_lower_bounds, hg_norm, mla_q_norm, mla_w_uq, mla_kv_norm, mla_w_ukv, ffn_w_gate, ffn_w_up, ffn_conv, ffn_w_down, final_norm):
    n_p, n_s = c_prompt.shape[0], c_sample.shape[0]
    rows = -(-(n_p + n_s) // 8) * 8
    c_all = jnp.concatenate([c_prompt, c_sample, jnp.zeros((rows - n_p - n_s, D_MODEL), F32)], axis=0)
    mod_all = _ada_call(c_all, w_ada, b_ada)
    mods_p = [mod_all[l, :n_p].reshape(n_p, 1, 6 * D_MODEL) for l in range(DEPTH)]
    mods_s = [mod_all[l, n_p:n_p + n_s].reshape(n_s, 1, 6 * D_MODEL) for l in range(DEPTH)]

    lb = jnp.cumsum(jax.nn.softmax(hg_lower_bounds.astype(F32), axis=1), axis=1)
    lb = lb - lb[:, :1]

    p = dict(
        norm_mix=norm_mix, norm_ffn=norm_ffn, na_rpb=na_rpb, dn_conv=dn_conv, dn_a_log=dn_a_log,
        dn_dt_bias=dn_dt_bias, dn_norm=dn_norm, hg_norm=hg_norm, mla_q_norm=mla_q_norm,
        mla_w_uq=mla_w_uq, mla_kv_norm=mla_kv_norm, mla_w_ukv=mla_w_ukv, ffn_conv=ffn_conv,
        final_norm=final_norm,
        w_in_ab_main=w_in_ab[:, :, :AB_MAIN].astype(BF16),
        w_in_ab_side=jax.vmap(_pad_side)(w_in_ab[:, :, AB_MAIN:]),
        w_in_cd_main=w_in_cd[:, :, :CD_MAIN].astype(BF16),
        w_in_cd_side=jnp.concatenate([w_in_cd[:, :, CD_MAIN:],
                                      w_in_cd[:, :, CD_MAIN:][:, :, ROPE_SWAP]], axis=-1).astype(BF16),
        w_out_ab=w_out_ab.astype(BF16), w_out_cd=w_out_cd.astype(BF16),
        ffn_w_gate=ffn_w_gate.astype(BF16), ffn_w_up=ffn_w_up.astype(BF16),
        ffn_w_down=ffn_w_down.astype(BF16),
    )
    y_prompt = _trunk(x_prompt, mods_p, lb, p)
    y_sample = _trunk(x_sample, mods_s, lb, p)
    return (y_prompt, y_sample)
```

```python
import functools
import math

import numpy as np
import jax
import jax.numpy as jnp
from jax import lax
from jax.experimental import pallas as pl
from jax.experimental.pallas import tpu as pltpu

D_MODEL = 2048
DEPTH = 4
HEAD_DIM = 128
H_GROUP = 8
GROUP_W = H_GROUP * HEAD_DIM
GRID_W = 64
NA_WIN_R = 8
NA_WIN_C = 16
NA_QCB = NA_WIN_C
NA_BAND = 2 * NA_WIN_C
DN_CONV_K = 5
DN_CHUNK = 64
HG_CHUNK = 32
MLA_Q_LORA = 512
MLA_KV_LORA = 512
MLA_NOPE = 128
MLA_ROPE = 64
MLA_V = 128
MLA_BLOCK_Q = 128
ROPE_BASE = 10000.0
D_FF = 5632
FFN_CONV_K = 3
EPS = 1e-6
AB_SPLIT = [GROUP_W, GROUP_W, GROUP_W, 3 * GROUP_W, GROUP_W, H_GROUP, H_GROUP, H_GROUP, H_GROUP]
CD_SPLIT = [GROUP_W, GROUP_W, GROUP_W, GROUP_W, GROUP_W, MLA_Q_LORA, MLA_KV_LORA, MLA_ROPE]
AB_MAIN = 7 * GROUP_W
CD_MAIN = 6 * GROUP_W
SIDE_W = 128

F32 = jnp.float32
BF16 = jnp.bfloat16

VMEM_LIMIT_BYTES = 56 * 1024 * 1024
HALO = 16


def _params(sem):
    return pltpu.CompilerParams(dimension_semantics=sem, vmem_limit_bytes=VMEM_LIMIT_BYTES)


def _ada_kernel(c_ref, w_ref, b_ref, o_ref):
    c = c_ref[...]
    cond = c * jax.nn.sigmoid(c)
    o_ref[...] = jnp.dot(cond, w_ref[...], precision=lax.Precision.HIGHEST,
                         preferred_element_type=F32) + b_ref[...]


def _ada_call(c_all, w_ada, b_ada):
    rows = c_all.shape[0]
    tn = 1024
    return pl.pallas_call(
        _ada_kernel,
        out_shape=jax.ShapeDtypeStruct((DEPTH, rows, 6 * D_MODEL), F32),
        grid=(DEPTH, 6 * D_MODEL // tn),
        in_specs=[
            pl.BlockSpec((rows, D_MODEL), lambda l, j: (0, 0)),
            pl.BlockSpec((None, D_MODEL, tn), lambda l, j: (l, 0, j)),
            pl.BlockSpec((None, 1, tn), lambda l, j: (l, 0, j)),
        ],
        out_specs=pl.BlockSpec((None, rows, tn), lambda l, j: (l, 0, j)),
        compiler_params=_params(("arbitrary", "arbitrary")),
        name="ada_mod",
    )(c_all, w_ada, b_ada.reshape(DEPTH, 1, 6 * D_MODEL))


def _norm_mod(x, g, sh, sc):
    y = x * lax.rsqrt(jnp.mean(x * x, axis=-1, keepdims=True) + EPS)
    return (y * g) * (1.0 + sc) + sh


def _inproj_kernel(x_ref, g_ref, sh_ref, sc_ref, w_ref, ws_ref, o_ref, os_ref, h_ref):
    @pl.when(pl.program_id(2) == 0)
    def _():
        hb = _norm_mod(x_ref[...], g_ref[...], sh_ref[...], sc_ref[...]).astype(BF16)
        h_ref[...] = hb
        os_ref[...] = jnp.dot(hb, ws_ref[...], preferred_element_type=F32)

    o_ref[...] = jnp.dot(h_ref[...], w_ref[...], preferred_element_type=F32)


def _inproj_call(x, mod3, norm_g, w_main, w_side):
    bsz, length, d = x.shape
    n_main = w_main.shape[1]
    tm = min(1024, length)
    tn = 1024
    return pl.pallas_call(
        _inproj_kernel,
        out_shape=(jax.ShapeDtypeStruct((bsz, length, n_main), F32),
                   jax.ShapeDtypeStruct((bsz, length, SIDE_W), F32)),
        grid=(bsz, length // tm, n_main // tn),
        in_specs=[
            pl.BlockSpec((None, tm, d), lambda b, i, j: (b, i, 0)),
            pl.BlockSpec((1, d), lambda b, i, j: (0, 0)),
            pl.BlockSpec((None, 1, d), lambda b, i, j: (b, 0, 0)),
            pl.BlockSpec((None, 1, d), lambda b, i, j: (b, 0, 1)),
            pl.BlockSpec((d, tn), lambda b, i, j: (0, j)),
            pl.BlockSpec((d, SIDE_W), lambda b, i, j: (0, 0)),
        ],
        out_specs=(pl.BlockSpec((None, tm, tn), lambda b, i, j: (b, i, j)),
                   pl.BlockSpec((None, tm, SIDE_W), lambda b, i, j: (b, i, 0))),
        scratch_shapes=[pltpu.VMEM((tm, d), BF16)],
        compiler_params=_params(("arbitrary", "arbitrary", "arbitrary")),
        name="inproj",
    )(x, norm_g.reshape(1, d), mod3, mod3, w_main, w_side)


def _outproj_kernel(rec_first, att_ref, of_ref, ob_ref, z_ref, ng_ref, w_ref, x_ref, gate_ref,
                    o_ref, y_ref):
    half = att_ref.shape[-1]
    ng = ng_ref[...]
    for h in range(H_GROUP):
        sl = slice(h * HEAD_DIM, (h + 1) * HEAD_DIM)
        o = of_ref[:, sl] + ob_ref[:, sl]
        o = o * lax.rsqrt(jnp.mean(o * o, axis=-1, keepdims=True) + EPS) * ng
        z = z_ref[:, sl]
        y_ref[:, sl] = (o * (z * jax.nn.sigmoid(z))).astype(BF16)
    rec0, att0 = (0, half) if rec_first else (half, 0)
    y = jnp.dot(y_ref[...], w_ref[rec0:rec0 + half, :], preferred_element_type=F32)
    y = y + jnp.dot(att_ref[...].astype(BF16), w_ref[att0:att0 + half, :],
                    preferred_element_type=F32)
    o_ref[...] = x_ref[...] + gate_ref[...] * y


def _outproj_call(att, o_fw, o_bw, main, gate_group, norm_g, rec_first, w_out, x, mod3):
    bsz, length, d = x.shape
    half = att.shape[-1]
    tm = min(512, length)
    blk = lambda col: pl.BlockSpec((None, tm, half), lambda b, i: (b, i, col))
    return pl.pallas_call(
        functools.partial(_outproj_kernel, rec_first),
        out_shape=jax.ShapeDtypeStruct(x.shape, F32),
        grid=(bsz, length // tm),
        in_specs=[
            blk(0), blk(0), blk(0), blk(gate_group),
            pl.BlockSpec((1, HEAD_DIM), lambda b, i: (0, 0)),
            pl.BlockSpec((2 * half, d), lambda b, i: (0, 0)),
            pl.BlockSpec((None, tm, d), lambda b, i: (b, i, 0)),
            pl.BlockSpec((None, 1, d), lambda b, i: (b, 0, 2)),
        ],
        out_specs=pl.BlockSpec((None, tm, d), lambda b, i: (b, i, 0)),
        scratch_shapes=[pltpu.VMEM((tm, half), BF16)],
        compiler_params=_params(("arbitrary", "arbitrary")),
        name="outproj",
    )(att, o_fw, o_bw, main, norm_g.reshape(1, HEAD_DIM), w_out, x, mod3)


def _ffn_kernel(final, xp_ref, x_ref, xn_ref, g_ref, sh_ref, sc_ref, gate_ref, wg_ref, wu_ref,
                cw_ref, wd_ref, fn_ref, o_ref, h_ref, gp_ref):
    i = pl.program_id(1)
    j = pl.program_id(2)
    tm = x_ref.shape[0]

    @pl.when(j == 0)
    def _():
        g, sh, sc = g_ref[...], sh_ref[...], sc_ref[...]
        h_ref[HALO:HALO + tm, :] = _norm_mod(x_ref[...], g, sh, sc).astype(BF16)
        hp = _norm_mod(xp_ref[...], g, sh, sc)
        h_ref[0:HALO, :] = jnp.where(i == 0, 0.0, hp).astype(BF16)
        hn = _norm_mod(xn_ref[...], g, sh, sc)
        h_ref[HALO + tm:, :] = jnp.where(i == pl.num_programs(1) - 1, 0.0, hn).astype(BF16)
        o_ref[...] = jnp.zeros_like(o_ref)

    gp_ref[...] = jnp.dot(h_ref[...], wg_ref[...], preferred_element_type=F32)
    cw = cw_ref[...]
    a = gp_ref[pl.ds(HALO - 1, tm), :] * cw[0:1, :]
    a = a + gp_ref[pl.ds(HALO, tm), :] * cw[1:2, :]
    a = a + gp_ref[pl.ds(HALO + 1, tm), :] * cw[2:3, :]
    u = jnp.dot(h_ref[HALO:HALO + tm, :], wu_ref[...], preferred_element_type=F32)
    act = (a * jax.nn.sigmoid(a)) * u
    o_ref[...] += jnp.dot(act.astype(BF16), wd_ref[...], preferred_element_type=F32)

    @pl.when(j == pl.num_programs(2) - 1)
    def _():
        xo = x_ref[...] + gate_ref[...] * o_ref[...]
        if final:
            xo = xo * lax.rsqrt(jnp.mean(xo * xo, axis=-1, keepdims=True) + EPS) * fn_ref[...]
        o_ref[...] = xo


def _ffn_call(x, mod3, norm_g, w_gate, w_up, conv_w, w_down, final_g, final):
    bsz, length, d = x.shape
    tm = min(512, length)
    tf = 512
    nh = tm // HALO
    last_h = length // HALO - 1
    return pl.pallas_call(
        functools.partial(_ffn_kernel, final),
        out_shape=jax.ShapeDtypeStruct(x.shape, F32),
        grid=(bsz, length // tm, D_FF // tf),
        in_specs=[
            pl.BlockSpec((None, HALO, d), lambda b, i, j: (b, jnp.maximum(i * nh - 1, 0), 0)),
            pl.BlockSpec((None, tm, d), lambda b, i, j: (b, i, 0)),
            pl.BlockSpec((None, HALO, d), lambda b, i, j: (b, jnp.minimum((i + 1) * nh, last_h), 0)),
            pl.BlockSpec((1, d), lambda b, i, j: (0, 0)),
            pl.BlockSpec((None, 1, d), lambda b, i, j: (b, 0, 3)),
            pl.BlockSpec((None, 1, d), lambda b, i, j: (b, 0, 4)),
            pl.BlockSpec((None, 1, d), lambda b, i, j: (b, 0, 5)),
            pl.BlockSpec((d, tf), lambda b, i, j: (0, j)),
            pl.BlockSpec((d, tf), lambda b, i, j: (0, j)),
            pl.BlockSpec((FFN_CONV_K, tf), lambda b, i, j: (0, j)),
            pl.BlockSpec((tf, d), lambda b, i, j: (j, 0)),
            pl.BlockSpec((1, d), lambda b, i, j: (0, 0)),
        ],
        out_specs=pl.BlockSpec((None, tm, d), lambda b, i, j: (b, i, 0)),
        scratch_shapes=[pltpu.VMEM((tm + 2 * HALO, d), BF16),
                        pltpu.VMEM((tm + 2 * HALO, tf), F32)],
        compiler_params=_params(("arbitrary", "arbitrary", "arbitrary")),
        name="conv_ffn",
    )(x, x, x, norm_g.reshape(1, d), mod3, mod3, mod3, w_gate, w_up, conv_w, w_down,
      final_g.reshape(1, d))


NA_KEYS = NA_WIN_R * GRID_W
MASKED = -1e30
NA_ROWS_PER_STEP = 4


def _na_bias_table(rpb):
    qc = np.arange(GRID_W)[:, None]
    kc = np.arange(GRID_W)[None, :]
    cs = np.clip(qc - NA_WIN_C // 2, 0, GRID_W - NA_WIN_C)
    valid = (kc >= cs) & (kc < cs + NA_WIN_C)
    dc = np.clip(kc - qc + NA_WIN_C - 1, 0, 2 * NA_WIN_C - 2)
    dr = np.arange(NA_WIN_R)[None, :] - np.arange(NA_WIN_R)[:, None] + NA_WIN_R - 1
    t = rpb.astype(F32)[:, dr][:, :, :, dc]
    t = jnp.where(valid[None, None, None], t, MASKED)
    return t.transpose(0, 1, 3, 2, 4).reshape(rpb.shape[0], NA_WIN_R, GRID_W, NA_KEYS)


def _na_kernel(q_ref, k_ref, v_ref, bias_ref, o_ref, kb_ref, vb_ref):
    rows = q_ref.shape[0] // GRID_W
    kb_ref[...] = k_ref[...].astype(BF16)
    vb_ref[...] = v_ref[...].astype(BF16)

    def row_group(g, carry):
        todo = []
        for u in range(NA_ROWS_PER_STEP):
            r = g * NA_ROWS_PER_STEP + u
            rs = jnp.clip(r - NA_WIN_R // 2, 0, rows - NA_WIN_R)
            q0 = pl.multiple_of(r * GRID_W, GRID_W)
            k0 = pl.multiple_of(rs * GRID_W, GRID_W)
            q = (q_ref[pl.ds(q0, GRID_W), :] * (HEAD_DIM ** -0.5)).astype(BF16)
            s = lax.dot_general(q, kb_ref[pl.ds(k0, NA_KEYS), :], (((1,), (1,)), ((), ())),
                                preferred_element_type=F32)
            todo.append((q0, k0, s, r - rs))
        for u, (q0, k0, s, off) in enumerate(todo):
            s = s + bias_ref[off]
            p = jnp.exp(s - jnp.max(s, axis=-1, keepdims=True))
            todo[u] = (q0, k0, p.astype(BF16), jnp.sum(p, axis=-1, keepdims=True))
        for u, (q0, k0, p, l) in enumerate(todo):
            todo[u] = (q0, jnp.dot(p, vb_ref[pl.ds(k0, NA_KEYS), :], preferred_element_type=F32), l)
        for q0, o, l in todo:
            o_ref[pl.ds(q0, GRID_W), :] = o / l
        return carry

    lax.fori_loop(0, rows // NA_ROWS_PER_STEP, row_group, 0)


def _na_call(main, rpb):
    bsz, length, _ = main.shape
    assert length % GRID_W == 0 and length // GRID_W >= NA_WIN_R
    hd = HEAD_DIM
    return pl.pallas_call(
        _na_kernel,
        out_shape=jax.ShapeDtypeStruct((bsz, length, GROUP_W), F32),
        grid=(bsz, H_GROUP),
        in_specs=[
            pl.BlockSpec((None, length, hd), lambda b, h: (b, 0, h)),
            pl.BlockSpec((None, length, hd), lambda b, h: (b, 0, H_GROUP + h)),
            pl.BlockSpec((None, length, hd), lambda b, h: (b, 0, 2 * H_GROUP + h)),
            pl.BlockSpec((None, NA_WIN_R, GRID_W, NA_KEYS), lambda b, h: (h, 0, 0, 0)),
        ],
        out_specs=pl.BlockSpec((None, length, hd), lambda b, h: (b, 0, h)),
        scratch_shapes=[pltpu.VMEM((length, hd), BF16), pltpu.VMEM((length, hd), BF16)],
        compiler_params=_params(("arbitrary", "arbitrary")),
        name="nbr_attn",
    )(main, main, main, _na_bias_table(rpb))


REC_TILE = 512
REC_HEADS = 8
HIGHEST = lax.Precision.HIGHEST


def _tri(n, upper):
    r = lax.broadcasted_iota(jnp.int32, (n, n), 0)
    c = lax.broadcasted_iota(jnp.int32, (n, n), 1)
    return (c >= r) if upper else (c <= r)


def _gdn_conv_kernel(x_ref, w_ref, o_ref, xp_ref):
    length, cw = x_ref.shape
    col0 = pl.program_id(1) * cw
    pad = 8
    xp_ref[0:pad, :] = jnp.zeros((pad, cw), F32)
    xp_ref[pad + length:, :] = jnp.zeros((pad, cw), F32)
    xp_ref[pad:pad + length, :] = x_ref[...]
    w = w_ref[...]
    y = xp_ref[pl.ds(pad - DN_CONV_K // 2, length), :] * w[0:1, :]
    for t in range(1, DN_CONV_K):
        y = y + xp_ref[pl.ds(pad - DN_CONV_K // 2 + t, length), :] * w[t:t + 1, :]
    y = y * jax.nn.sigmoid(y)
    qscale = jnp.where(col0 < GROUP_W, HEAD_DIM ** -0.5, 1.0)
    for h in range(cw // HEAD_DIM):
        yh = y[:, h * HEAD_DIM:(h + 1) * HEAD_DIM]
        nh = yh * (lax.rsqrt(jnp.sum(yh * yh, axis=-1, keepdims=True) + EPS) * qscale)
        o_ref[:, h * HEAD_DIM:(h + 1) * HEAD_DIM] = jnp.where(col0 < 2 * GROUP_W, nh, yh)


def _gdn_conv_call(main, conv_w):
    bsz, length, _ = main.shape
    cw = HEAD_DIM
    nblk = 3 * GROUP_W // cw
    return pl.pallas_call(
        _gdn_conv_kernel,
        out_shape=jax.ShapeDtypeStruct((bsz, length, 3 * GROUP_W), F32),
        grid=(bsz, nblk),
        in_specs=[pl.BlockSpec((None, length, cw), lambda b, j: (b, 0, nblk + j)),
                  pl.BlockSpec((DN_CONV_K, cw), lambda b, j: (0, j))],
        out_specs=pl.BlockSpec((None, length, cw), lambda b, j: (b, 0, j)),
        scratch_shapes=[pltpu.VMEM((length + 16, cw), F32)],
        compiler_params=_params(("arbitrary", "arbitrary")),
        name="gdn_conv",
    )(main, conv_w)


def _gdn_gates_kernel(s_ref, alog_ref, dt_ref, g_ref, gt_ref):
    tm = s_ref.shape[0]
    s = s_ref[...]
    x = s + dt_ref[...]
    softplus = jnp.maximum(x, 0.0) + jnp.log1p(jnp.exp(-jnp.abs(x)))
    g = -jnp.exp(alog_ref[...]) * softplus
    r = lax.broadcasted_iota(jnp.int32, (tm, tm), 0)
    c = lax.broadcasted_iota(jnp.int32, (tm, tm), 1)
    same = (r // DN_CHUNK) == (c // DN_CHUNK)
    lo = jnp.where(same & (c <= r), 1.0, 0.0)
    up = jnp.where(same & (c >= r), 1.0, 0.0)
    gcf = jnp.dot(lo, g, precision=HIGHEST, preferred_element_type=F32)
    gcb = jnp.dot(up, g, precision=HIGHEST, preferred_element_type=F32)
    lane = lax.broadcasted_iota(jnp.int32, s.shape, 1)
    out = jnp.where(lane < H_GROUP, gcf, jnp.where(lane < 2 * H_GROUP, gcb, jax.nn.sigmoid(s)))
    g_ref[...] = out
    for p in range(tm // 128):
        gt_ref[p] = out[p * 128:(p + 1) * 128, :].T[0:4 * H_GROUP, :]


def _gdn_gates_call(side, a_log, dt_bias):
    bsz, length, _ = side.shape
    tm = min(512, length)
    row = lambda t: jnp.pad(t.reshape(1, 2 * H_GROUP).astype(F32), ((0, 0), (0, SIDE_W - 2 * H_GROUP)))
    return pl.pallas_call(
        _gdn_gates_kernel,
        out_shape=(jax.ShapeDtypeStruct((bsz, length, SIDE_W), F32),
                   jax.ShapeDtypeStruct((bsz, length // 128, 4 * H_GROUP, 128), F32)),
        grid=(bsz, length // tm),
        in_specs=[pl.BlockSpec((None, tm, SIDE_W), lambda b, i: (b, i, 0)),
                  pl.BlockSpec((1, SIDE_W), lambda b, i: (0, 0)),
                  pl.BlockSpec((1, SIDE_W), lambda b, i: (0, 0))],
        out_specs=(pl.BlockSpec((None, tm, SIDE_W), lambda b, i: (b, i, 0)),
                   pl.BlockSpec((None, tm // 128, 4 * H_GROUP, 128), lambda b, i: (b, i, 0, 0))),
        compiler_params=_params(("arbitrary", "arbitrary")),
        name="gdn_gates",
    )(side, row(a_log), row(dt_bias))


def _bdot(a, b):
    return jnp.dot(a.astype(BF16), b.astype(BF16), preferred_element_type=F32)


def _gdn_chunks_local(chunks):
    c = DN_CHUNK
    r = lax.broadcasted_iota(jnp.int32, (c, c), 0)
    cc = lax.broadcasted_iota(jnp.int32, (c, c), 1)
    for d in chunks:
        incl = (cc >= r) if d["rev"] else (cc <= r)
        d["gam"] = jnp.exp(jnp.where(incl, d["gci"] - d["gcj"], MASKED))
        d["kb"] = d["k"] * d["beta"]
    for d in chunks:
        kq = jnp.concatenate([d["kb"], d["q"]], axis=0).astype(BF16)
        d["prod"] = lax.dot_general(kq, d["k"].astype(BF16), (((1,), (1,)), ((), ())),
                                    preferred_element_type=F32)
    for d in chunks:
        strict = (cc > r) if d["rev"] else (cc < r)
        d["aqk"] = d["prod"][c:] * d["gam"]
        d["p"] = -jnp.where(strict, d["prod"][0:c] * d["gam"], 0.0)
        d["t"] = jnp.where(r == cc, 1.0, 0.0) + d["p"]
    for _ in range(int(math.log2(c)) - 1):
        for d in chunks:
            d["p"] = _bdot(d["p"], d["p"])
        for d in chunks:
            d["t"] = d["t"] + _bdot(d["t"], d["p"])
    for d in chunks:
        egc = jnp.exp(d["gci"])
        rhs = jnp.concatenate([d["v"] * d["beta"], d["kb"] * egc], axis=1)
        uw = _bdot(d["t"], rhs)
        d["u"], d["w"] = uw[:, 0:HEAD_DIM], uw[:, HEAD_DIM:]
        glast = d["gci"][0:1, :] if d["rev"] else d["gci"][c - 1:c, :]
        d["qd"] = d["q"] * egc
        d["kd"] = d["k"] * jnp.exp(glast - d["gci"])
        d["dlast"] = jnp.exp(glast)


def _gdn_chunks_scan(chunks):
    c = DN_CHUNK
    for d in chunks:
        d["s"] = d["s_ref"][...]
        d["ws"] = _bdot(jnp.concatenate([d["w"], d["qd"]], axis=0), d["s"])
    for d in chunks:
        d["vn"] = (d["u"] - d["ws"][0:c]).astype(BF16)
        d["o"] = d["ws"][c:] + _bdot(d["aqk"], d["vn"])
    for d in chunks:
        d["s_ref"][...] = d["s"] * d["dlast"] + lax.dot_general(
            d["kd"].astype(BF16), d["vn"], (((0,), (0,)), ((), ())), preferred_element_type=F32)


def _gdn_kernel(qf_ref, kf_ref, vf_ref, gf_ref, gtf_ref, qb_ref, kb_ref, vb_ref, gb_ref, gtb_ref,
                of_ref, ob_ref, s_ref):
    hg = pl.program_id(1)

    @pl.when(pl.program_id(2) == 0)
    def _():
        s_ref[...] = jnp.zeros_like(s_ref)

    npair = qf_ref.shape[0] // 128
    c = DN_CHUNK

    def pair(p, carry):
        chunks = []
        for rev in (False, True):
            q_ref, k_ref, v_ref, g_ref, gt_ref, o_ref = (
                (qb_ref, kb_ref, vb_ref, gb_ref, gtb_ref, ob_ref) if rev
                else (qf_ref, kf_ref, vf_ref, gf_ref, gtf_ref, of_ref))
            pp = (npair - 1 - p) if rev else p
            base = pl.multiple_of(pp * 128, 128)
            gt = gt_ref[pp]
            sub = lax.broadcasted_iota(jnp.int32, gt.shape, 0)
            for step, half in enumerate((1, 0) if rev else (0, 1)):
                rows = pl.ds(base + half * c, c)
                g = g_ref[rows, :]
                lane = lax.broadcasted_iota(jnp.int32, g.shape, 1)
                for hl in range(REC_HEADS):
                    cols = slice(hl * HEAD_DIM, (hl + 1) * HEAD_DIM)
                    gcol = hg * REC_HEADS + hl + (H_GROUP if rev else 0)
                    chunks.append(dict(
                        rev=rev, step=step, rows=rows, cols=cols, o_ref=o_ref,
                        s_ref=s_ref.at[int(rev), hl],
                        q=q_ref[rows, cols], k=k_ref[rows, cols], v=v_ref[rows, cols],
                        gci=jnp.sum(jnp.where(lane == gcol, g, 0.0), axis=-1, keepdims=True),
                        beta=jnp.sum(jnp.where(lane == gcol + 2 * H_GROUP, g, 0.0), axis=-1,
                                     keepdims=True),
                        gcj=jnp.sum(jnp.where(sub == gcol, gt, 0.0), axis=0,
                                    keepdims=True)[:, half * c:(half + 1) * c]))
        _gdn_chunks_local(chunks)
        for step in (0, 1):
            now = [d for d in chunks if d["step"] == step]
            _gdn_chunks_scan(now)
            for d in now:
                d["o_ref"][d["rows"], d["cols"]] = d["o"]
        return carry

    lax.fori_loop(0, npair, pair, 0)


def _gdn_call(main, side, conv_w, a_log, dt_bias):
    bsz, length, _ = main.shape
    tl = min(REC_TILE, length)
    nt = length // tl
    w = REC_HEADS * HEAD_DIM
    ng = H_GROUP // REC_HEADS
    qkv = _gdn_conv_call(main, conv_w)
    gates, gates_t = _gdn_gates_call(side, a_log, dt_bias)

    def specs(tile):
        return [pl.BlockSpec((None, tl, w), lambda b, g, i: (b, tile(i), g)),
                pl.BlockSpec((None, tl, w), lambda b, g, i: (b, tile(i), ng + g)),
                pl.BlockSpec((None, tl, w), lambda b, g, i: (b, tile(i), 2 * ng + g)),
                pl.BlockSpec((None, tl, SIDE_W), lambda b, g, i: (b, tile(i), 0)),
                pl.BlockSpec((None, tl // 128, 4 * H_GROUP, 128), lambda b, g, i: (b, tile(i), 0, 0))]

    fw = lambda i: i
    bw = lambda i: nt - 1 - i
    return pl.pallas_call(
        _gdn_kernel,
        out_shape=(jax.ShapeDtypeStruct((bsz, length, GROUP_W), F32),) * 2,
        grid=(bsz, ng, nt),
        in_specs=specs(fw) + specs(bw),
        out_specs=(pl.BlockSpec((None, tl, w), lambda b, g, i: (b, fw(i), g)),
                   pl.BlockSpec((None, tl, w), lambda b, g, i: (b, bw(i), g))),
        scratch_shapes=[pltpu.VMEM((2, REC_HEADS, HEAD_DIM, HEAD_DIM), F32)],
        compiler_params=_params(("arbitrary", "arbitrary", "arbitrary")),
        name="gdn_scan",
    )(qkv, qkv, qkv, gates, gates_t, qkv, qkv, qkv, gates, gates_t)


HG_TILE = 64
HG_SUB = 8
HG_NSUB = HG_TILE // HG_SUB
LOG2_E = 1.4426950408889634


def _tri_cumsum(x, rev):
    n = x.shape[0]
    tri = jnp.where(_tri(n, rev), 1.0, 0.0).astype(BF16)
    hi = x.astype(BF16)
    r1 = x - hi.astype(F32)
    mid = r1.astype(BF16)
    lo = (r1 - mid.astype(F32)).astype(BF16)
    parts = jnp.dot(tri, jnp.concatenate([hi, mid, lo], axis=1), preferred_element_type=F32)
    w = x.shape[1]
    return (parts[:, 0:w] + parts[:, w:2 * w]) + parts[:, 2 * w:]


def _hg_chunks(chains):
    n, sub, nsub = HG_TILE, HG_SUB, HG_NSUB
    row = lax.broadcasted_iota(jnp.int32, (n, HEAD_DIM), 0)
    pos = row % sub
    tn_dims = (((0,), (0,)), ((), ()))
    for c in chains:
        z, loglb, log1mlb = c["z"], c["loglb"], c["log1mlb"]
        c["q"] = c["q_raw"] * jax.nn.sigmoid(c["q_raw"])
        log_sig = jnp.minimum(z, 0.0) - jnp.log1p(jnp.exp(-jnp.abs(z)))
        bb = log1mlb + log_sig
        lf = jnp.maximum(loglb, bb) + jnp.log1p(jnp.exp(-jnp.abs(loglb - bb)))
        b = _tri_cumsum(lf, c["rev"])
        c["b2"] = b * LOG2_E
        c["c2"] = (b - (bb - z)) * LOG2_E
        c["inp16"] = c["inp"].astype(BF16)

    for c in chains:
        b2, c2, rev = c["b2"], c["c2"], c["rev"]
        c["order"] = list(range(nsub - 1, -1, -1)) if rev else list(range(nsub))
        c["ends"], c["contrib"] = [], []
        for j in c["order"]:
            lo = j * sub
            bend = b2[lo:lo + 1, :] if rev else b2[lo + sub - 1:lo + sub, :]
            c["ends"].append(bend)
            if len(c["contrib"]) < nsub - 1:
                ks = jnp.exp2(bend - c2[lo:lo + sub, :])
                c["contrib"].append(lax.dot_general(c["inp16"][lo:lo + sub, :], ks.astype(BF16),
                                                    tn_dims, preferred_element_type=F32))

    for c in chains:
        b2, c2, q, inp, rev = c["b2"], c["c2"], c["q"], c["inp"], c["rev"]
        o = jnp.sum(q * jnp.exp2(b2 - c2), axis=-1, keepdims=True) * inp
        for d in range(1, sub):
            shift = (sub - d) % sub if rev else d
            valid = (pos <= sub - 1 - d) if rev else (pos >= d)
            sh = lambda x: pltpu.roll(x.reshape(nsub, sub, HEAD_DIM), shift, axis=1).reshape(
                n, HEAD_DIM)
            e = jnp.where(valid, jnp.exp2(b2 - sh(c2)), 0.0)
            o = o + jnp.sum(q * e, axis=-1, keepdims=True) * sh(inp)
        c["o"] = o

    for c in chains:
        b2, q, rev = c["b2"], c["q"], c["rev"]
        c["s"] = c["s_ref"][...]
        qext = [q * jnp.exp2(b2)]
        for m in range(nsub - 1):
            j = c["order"][m]
            lo, hi = (0, j * sub) if rev else ((j + 1) * sub, n)
            part = q[lo:hi] * jnp.exp2(b2[lo:hi] - c["ends"][m])
            zeros = jnp.zeros((n - (hi - lo), HEAD_DIM), F32)
            qext.append(jnp.concatenate([part, zeros] if rev else [zeros, part], axis=0))
        ncat = jnp.concatenate([c["s"]] + c["contrib"], axis=1).astype(BF16)
        c["o"] = c["o"] + lax.dot_general(jnp.concatenate(qext, axis=1).astype(BF16), ncat,
                                          (((1,), (1,)), ((), ())), preferred_element_type=F32)
    for c in chains:
        btot = c["ends"][-1]
        kd = jnp.exp2(btot - c["c2"]).astype(BF16)
        c["s_ref"][...] = c["s"] * jnp.exp2(btot) + lax.dot_general(
            c["inp16"], kd, tn_dims, preferred_element_type=F32)


def _hgrn2_kernel(qf_ref, ff_ref, if_ref, qb_ref, fb_ref, ib_ref, lbf_ref, lbb_ref,
                  of_ref, ob_ref, s_ref):
    @pl.when(pl.program_id(2) == 0)
    def _():
        s_ref[...] = jnp.zeros_like(s_ref)

    nstep = qf_ref.shape[0] // HG_TILE

    def step(t, carry):
        chains = []
        for rev in (False, True):
            q_ref, f_ref, i_ref, lb_ref, o_ref = ((qb_ref, fb_ref, ib_ref, lbb_ref, ob_ref) if rev
                                                  else (qf_ref, ff_ref, if_ref, lbf_ref, of_ref))
            tt = (nstep - 1 - t) if rev else t
            rows = pl.ds(pl.multiple_of(tt * HG_TILE, HG_TILE), HG_TILE)
            for hl in range(REC_HEADS):
                cols = slice(hl * HEAD_DIM, (hl + 1) * HEAD_DIM)
                chains.append(dict(rev=rev, rows=rows, cols=cols, o_ref=o_ref,
                                   s_ref=s_ref.at[int(rev), hl], q_raw=q_ref[rows, cols],
                                   z=f_ref[rows, cols], inp=i_ref[rows, cols],
                                   loglb=lb_ref[0:1, cols], log1mlb=lb_ref[1:2, cols]))
        _hg_chunks(chains)
        for c in chains:
            c["o_ref"][c["rows"], c["cols"]] = c["o"]
        return carry

    lax.fori_loop(0, nstep, step, 0)


def _hgrn2_call(main, lb_fw, lb_bw):
    bsz, length, _ = main.shape
    tl = min(REC_TILE, length)
    nt = length // tl
    w = REC_HEADS * HEAD_DIM
    ng = H_GROUP // REC_HEADS
    fw = lambda i: i
    bw = lambda i: nt - 1 - i
    col = lambda group, tile: pl.BlockSpec((None, tl, w), lambda b, g, i: (b, tile(i), group * ng + g))
    lbrow = pl.BlockSpec((2, w), lambda b, g, i: (0, g))
    logs = lambda lb: jnp.stack([jnp.log(lb), jnp.log1p(-lb)])
    return pl.pallas_call(
        _hgrn2_kernel,
        out_shape=(jax.ShapeDtypeStruct((bsz, length, GROUP_W), F32),) * 2,
        grid=(bsz, ng, nt),
        in_specs=[col(0, fw), col(1, fw), col(3, fw), col(0, bw), col(2, bw), col(3, bw), lbrow, lbrow],
        out_specs=(pl.BlockSpec((None, tl, w), lambda b, g, i: (b, fw(i), g)),
                   pl.BlockSpec((None, tl, w), lambda b, g, i: (b, bw(i), g))),
        scratch_shapes=[pltpu.VMEM((2, REC_HEADS, HEAD_DIM, HEAD_DIM), F32)],
        compiler_params=_params(("arbitrary", "arbitrary", "arbitrary")),
        name="hgrn2_scan",
    )(main, main, main, main, main, main, logs(lb_fw), logs(lb_bw))


MLA_QK = 2 * MLA_NOPE
ROPE_SWAP = np.concatenate([np.arange(MLA_ROPE // 2, MLA_ROPE), np.arange(MLA_ROPE // 2)])


def _rope_table(length):
    half = MLA_ROPE // 2
    inv = ROPE_BASE ** (-jnp.arange(half, dtype=F32) / half)
    ang = jnp.arange(length, dtype=F32)[:, None] * inv[None, :]
    cos, sin = jnp.cos(ang), jnp.sin(ang)
    return jnp.concatenate([cos, cos, -sin, sin], axis=-1)


def _mla_q_weights(w_uq):
    w = w_uq.reshape(MLA_Q_LORA, H_GROUP, MLA_NOPE + MLA_ROPE)
    rope = w[:, :, MLA_NOPE:]
    w = jnp.concatenate([w[:, :, :MLA_NOPE], rope, rope[:, :, ROPE_SWAP]], axis=-1)
    return w.reshape(MLA_Q_LORA, H_GROUP * MLA_QK).astype(BF16)


def _plain_rmsnorm(x, g):
    return x * lax.rsqrt(jnp.mean(x * x, axis=-1, keepdims=True) + EPS) * g


def _mla_proj_kernel(cq_ref, ckv_ref, side_ref, tab_ref, qn_ref, kvn_ref, wq_ref, wkv_ref,
                     q_out, k_out, v_out):
    scale = (MLA_NOPE + MLA_ROPE) ** -0.5
    cqn = _plain_rmsnorm(cq_ref[...], qn_ref[...]).astype(BF16)
    ckvn = _plain_rmsnorm(ckv_ref[...], kvn_ref[...]).astype(BF16)
    q = jnp.dot(cqn, wq_ref[...], preferred_element_type=F32)
    kv = jnp.dot(ckvn, wkv_ref[...], preferred_element_type=F32)
    tab = tab_ref[...]
    kr = (side_ref[...] * tab).astype(BF16)
    for h in range(H_GROUP):
        c0 = h * MLA_QK
        qr = q[:, c0 + MLA_NOPE:c0 + MLA_QK] * tab
        qr = qr + pltpu.roll(qr, MLA_ROPE, axis=1)
        q_out[h, :, 0:MLA_NOPE] = (q[:, c0:c0 + MLA_NOPE] * scale).astype(BF16)
        q_out[h, :, MLA_NOPE:MLA_QK] = (qr * scale).astype(BF16)
        k_out[h, :, 0:MLA_NOPE] = kv[:, c0:c0 + MLA_NOPE].astype(BF16)
        k_out[h, :, MLA_NOPE:MLA_QK] = kr
        v_out[h] = kv[:, c0 + MLA_NOPE:c0 + MLA_QK].astype(BF16)


def _mla_attn_kernel(q_ref, k_ref, v_ref, o_ref):
    tq = q_ref.shape[0]
    length = k_ref.shape[0]
    tk = min(length, 2048)
    q = q_ref[...]
    m = l = acc = None
    for kb in range(length // tk):
        s = lax.dot_general(q, k_ref[kb * tk:(kb + 1) * tk, :], (((1,), (1,)), ((), ())),
                            preferred_element_type=F32)
        m_blk = jnp.max(s, axis=-1, keepdims=True)
        if kb == 0:
            m = m_blk
            p = jnp.exp(s - m)
            l = jnp.sum(p, axis=-1, keepdims=True)
            acc = jnp.dot(p.astype(BF16), v_ref[kb * tk:(kb + 1) * tk, :], preferred_element_type=F32)
        else:
            m_new = jnp.maximum(m, m_blk)
            alpha = jnp.exp(m - m_new)
            p = jnp.exp(s - m_new)
            l = alpha * l + jnp.sum(p, axis=-1, keepdims=True)
            acc = alpha * acc + jnp.dot(p.astype(BF16), v_ref[kb * tk:(kb + 1) * tk, :],
                                        preferred_element_type=F32)
            m = m_new
    o_ref[...] = acc / l


def _mla_call(main, side, q_norm, w_uq, kv_norm, w_ukv):
    bsz, length, _ = main.shape
    tm = min(512, length)
    lora = MLA_Q_LORA
    q_all, k_all, v_all = pl.pallas_call(
        _mla_proj_kernel,
        out_shape=(jax.ShapeDtypeStruct((bsz, H_GROUP, length, MLA_QK), BF16),
                   jax.ShapeDtypeStruct((bsz, H_GROUP, length, MLA_QK), BF16),
                   jax.ShapeDtypeStruct((bsz, H_GROUP, length, MLA_V), BF16)),
        grid=(bsz, length // tm),
        in_specs=[
            pl.BlockSpec((None, tm, lora), lambda b, i: (b, i, 10)),
            pl.BlockSpec((None, tm, lora), lambda b, i: (b, i, 11)),
            pl.BlockSpec((None, tm, SIDE_W), lambda b, i: (b, i, 0)),
            pl.BlockSpec((tm, SIDE_W), lambda b, i: (i, 0)),
            pl.BlockSpec((1, lora), lambda b, i: (0, 0)),
            pl.BlockSpec((1, lora), lambda b, i: (0, 0)),
            pl.BlockSpec((lora, H_GROUP * MLA_QK), lambda b, i: (0, 0)),
            pl.BlockSpec((lora, H_GROUP * MLA_QK), lambda b, i: (0, 0)),
        ],
        out_specs=(pl.BlockSpec((None, H_GROUP, tm, MLA_QK), lambda b, i: (b, 0, i, 0)),
                   pl.BlockSpec((None, H_GROUP, tm, MLA_QK), lambda b, i: (b, 0, i, 0)),
                   pl.BlockSpec((None, H_GROUP, tm, MLA_V), lambda b, i: (b, 0, i, 0))),
        compiler_params=_params(("arbitrary", "arbitrary")),
        name="mla_proj",
    )(main, main, side, _rope_table(length), q_norm.reshape(1, lora), kv_norm.reshape(1, lora),
      _mla_q_weights(w_uq), w_ukv.astype(BF16))

    tq = 512 if length > 2048 else 256
    return pl.pallas_call(
        _mla_attn_kernel,
        out_shape=jax.ShapeDtypeStruct((bsz, length, H_GROUP * MLA_V), F32),
        grid=(bsz, H_GROUP, length // tq),
        in_specs=[
            pl.BlockSpec((None, None, tq, MLA_QK), lambda b, h, i: (b, h, i, 0)),
            pl.BlockSpec((None, None, length, MLA_QK), lambda b, h, i: (b, h, 0, 0)),
            pl.BlockSpec((None, None, length, MLA_V), lambda b, h, i: (b, h, 0, 0)),
        ],
        out_specs=pl.BlockSpec((None, tq, MLA_V), lambda b, h, i: (b, i, h)),
        compiler_params=_params(("arbitrary", "arbitrary", "arbitrary")),
        name="mla_attn",
    )(q_all, k_all, v_all)


def _pad_side(w):
    return jnp.pad(w, ((0, 0), (0, SIDE_W - w.shape[1]))).astype(BF16)


def _trunk(x, mods, lb, p):
    for l in range(DEPTH):
        mod3 = mods[l]
        if l % 2 == 0:
            e = l // 2
            main, side = _inproj_call(x, mod3, p["norm_mix"][l], p["w_in_ab_main"][e],
                                      p["w_in_ab_side"][e])
            att = _na_call(main, p["na_rpb"][e])
            o_fw, o_bw = _gdn_call(main, side, p["dn_conv"][e], p["dn_a_log"][e], p["dn_dt_bias"][e])
            x = _outproj_call(att, o_fw, o_bw, main, 6, p["dn_norm"][e], False, p["w_out_ab"][e],
                              x, mod3)
        else:
            o = l // 2
            main, side = _inproj_call(x, mod3, p["norm_mix"][l], p["w_in_cd_main"][o],
                                      p["w_in_cd_side"][o])
            o_fw, o_bw = _hgrn2_call(main, lb[0, l], lb[1, l])
            att = _mla_call(main, side, p["mla_q_norm"][o], p["mla_w_uq"][o], p["mla_kv_norm"][o],
                            p["mla_w_ukv"][o])
            x = _outproj_call(att, o_fw, o_bw, main, 4, p["hg_norm"][o], True, p["w_out_cd"][o],
                              x, mod3)
        x = _ffn_call(x, mod3, p["norm_ffn"][l], p["ffn_w_gate"][l], p["ffn_w_up"][l],
                      p["ffn_conv"][l], p["ffn_w_down"][l], p["final_norm"], l == DEPTH - 1)
    return x


def kernel(x_prompt, x_sample, c_prompt, c_sample, norm_mix, norm_ffn, w_ada, b_ada, w_in_ab, w_out_ab, na_rpb, dn_conv, dn_a_log, dn_dt_bias, dn_norm, w_in_cd, w_out_cd, hg_lower_bounds, hg_norm, mla_q_norm, mla_w_uq, mla_kv_norm, mla_w_ukv, ffn_w_gate, ffn_w_up, ffn_conv, ffn_w_down, final_norm):
    n_p, n_s = c_prompt.shape[0], c_sample.shape[0]
    rows = -(-(n_p + n_s) // 8) * 8
    c_all = jnp.concatenate([c_prompt, c_sample, jnp.zeros((rows - n_p - n_s, D_MODEL), F32)], axis=0)
    mod_all = _ada_call(c_all, w_ada, b_ada)
    mods_p = [mod_all[l, :n_p].reshape(n_p, 1, 6 * D_MODEL) for l in range(DEPTH)]
    mods_s = [mod_all[l, n_p:n_p + n_s].reshape(n_s, 1, 6 * D_MODEL) for l in range(DEPTH)]

    lb = jnp.cumsum(jax.nn.softmax(hg_lower_bounds.astype(F32), axis=1), axis=1)
    lb = lb - lb[:, :1]

    p = dict(
        norm_mix=norm_mix, norm_ffn=norm_ffn, na_rpb=na_rpb, dn_conv=dn_conv, dn_a_log=dn_a_log,
        dn_dt_bias=dn_dt_bias, dn_norm=dn_norm, hg_norm=hg_norm, mla_q_norm=mla_q_norm,
        mla_w_uq=mla_w_uq, mla_kv_norm=mla_kv_norm, mla_w_ukv=mla_w_ukv, ffn_conv=ffn_conv,
        final_norm=final_norm,
        w_in_ab_main=w_in_ab[:, :, :AB_MAIN].astype(BF16),
        w_in_ab_side=jax.vmap(_pad_side)(w_in_ab[:, :, AB_MAIN:]),
        w_in_cd_main=w_in_cd[:, :, :CD_MAIN].astype(BF16),
        w_in_cd_side=jnp.concatenate([w_in_cd[:, :, CD_MAIN:],
                                      w_in_cd[:, :, CD_MAIN:][:, :, ROPE_SWAP]], axis=-1).astype(BF16),
        w_out_ab=w_out_ab.astype(BF16), w_out_cd=w_out_cd.astype(BF16),
        ffn_w_gate=ffn_w_gate.astype(BF16), ffn_w_up=ffn_w_up.astype(BF16),
        ffn_w_down=ffn_w_down.astype(BF16),
    )
    y_prompt = _trunk(x_prompt, mods_p, lb, p)
    y_sample = _trunk(x_sample, mods_s, lb, p)
    return (y_prompt, y_sample)
```

```python
import functools
import math

import numpy as np
import jax
import jax.numpy as jnp
from jax import lax
from jax.experimental import pallas as pl
from jax.experimental.pallas import tpu as pltpu

D_MODEL = 2048
DEPTH = 4
HEAD_DIM = 128
H_GROUP = 8
GROUP_W = H_GROUP * HEAD_DIM
GRID_W = 64
NA_WIN_R = 8
NA_WIN_C = 16
NA_QCB = NA_WIN_C
NA_BAND = 2 * NA_WIN_C
DN_CONV_K = 5
DN_CHUNK = 64
HG_CHUNK = 32
MLA_Q_LORA = 512
MLA_KV_LORA = 512
MLA_NOPE = 128
MLA_ROPE = 64
MLA_V = 128
MLA_BLOCK_Q = 128
ROPE_BASE = 10000.0
D_FF = 5632
FFN_CONV_K = 3
EPS = 1e-6
AB_SPLIT = [GROUP_W, GROUP_W, GROUP_W, 3 * GROUP_W, GROUP_W, H_GROUP, H_GROUP, H_GROUP, H_GROUP]
CD_SPLIT = [GROUP_W, GROUP_W, GROUP_W, GROUP_W, GROUP_W, MLA_Q_LORA, MLA_KV_LORA, MLA_ROPE]
AB_MAIN = 7 * GROUP_W
CD_MAIN = 6 * GROUP_W
SIDE_W = 128

F32 = jnp.float32
BF16 = jnp.bfloat16

VMEM_LIMIT_BYTES = 56 * 1024 * 1024
HALO = 16


def _params(sem):
    return pltpu.CompilerParams(dimension_semantics=sem, vmem_limit_bytes=VMEM_LIMIT_BYTES)


def _ada_kernel(c_ref, w_ref, b_ref, o_ref):
    c = c_ref[...]
    cond = c * jax.nn.sigmoid(c)
    o_ref[...] = jnp.dot(cond, w_ref[...], precision=lax.Precision.HIGHEST,
                         preferred_element_type=F32) + b_ref[...]


def _ada_call(c_all, w_ada, b_ada):
    rows = c_all.shape[0]
    tn = 1024
    return pl.pallas_call(
        _ada_kernel,
        out_shape=jax.ShapeDtypeStruct((DEPTH, rows, 6 * D_MODEL), F32),
        grid=(DEPTH, 6 * D_MODEL // tn),
        in_specs=[
            pl.BlockSpec((rows, D_MODEL), lambda l, j: (0, 0)),
            pl.BlockSpec((None, D_MODEL, tn), lambda l, j: (l, 0, j)),
            pl.BlockSpec((None, 1, tn), lambda l, j: (l, 0, j)),
        ],
        out_specs=pl.BlockSpec((None, rows, tn), lambda l, j: (l, 0, j)),
        compiler_params=_params(("arbitrary", "arbitrary")),
        name="ada_mod",
    )(c_all, w_ada, b_ada.reshape(DEPTH, 1, 6 * D_MODEL))


def _norm_mod(x, g, sh, sc):
    y = x * lax.rsqrt(jnp.mean(x * x, axis=-1, keepdims=True) + EPS)
    return (y * g) * (1.0 + sc) + sh


def _inproj_kernel(x_ref, g_ref, sh_ref, sc_ref, w_ref, ws_ref, o_ref, os_ref, h_ref):
    @pl.when(pl.program_id(2) == 0)
    def _():
        hb = _norm_mod(x_ref[...], g_ref[...], sh_ref[...], sc_ref[...]).astype(BF16)
        h_ref[...] = hb
        os_ref[...] = jnp.dot(hb, ws_ref[...], preferred_element_type=F32)

    o_ref[...] = jnp.dot(h_ref[...], w_ref[...], preferred_element_type=F32)


def _inproj_call(x, mod3, norm_g, w_main, w_side):
    bsz, length, d = x.shape
    n_main = w_main.shape[1]
    tm = min(1024, length)
    tn = 1024
    return pl.pallas_call(
        _inproj_kernel,
        out_shape=(jax.ShapeDtypeStruct((bsz, length, n_main), F32),
                   jax.ShapeDtypeStruct((bsz, length, SIDE_W), F32)),
        grid=(bsz, length // tm, n_main // tn),
        in_specs=[
            pl.BlockSpec((None, tm, d), lambda b, i, j: (b, i, 0)),
            pl.BlockSpec((1, d), lambda b, i, j: (0, 0)),
            pl.BlockSpec((None, 1, d), lambda b, i, j: (b, 0, 0)),
            pl.BlockSpec((None, 1, d), lambda b, i, j: (b, 0, 1)),
            pl.BlockSpec((d, tn), lambda b, i, j: (0, j)),
            pl.BlockSpec((d, SIDE_W), lambda b, i, j: (0, 0)),
        ],
        out_specs=(pl.BlockSpec((None, tm, tn), lambda b, i, j: (b, i, j)),
                   pl.BlockSpec((None, tm, SIDE_W), lambda b, i, j: (b, i, 0))),
        scratch_shapes=[pltpu.VMEM((tm, d), BF16)],
        compiler_params=_params(("arbitrary", "arbitrary", "arbitrary")),
        name="inproj",
    )(x, norm_g.reshape(1, d), mod3, mod3, w_main, w_side)


def _outproj_kernel(rec_first, att_ref, of_ref, ob_ref, z_ref, ng_ref, w_ref, x_ref, gate_ref,
                    o_ref, y_ref):
    half = att_ref.shape[-1]
    ng = ng_ref[...]
    for h in range(H_GROUP):
        sl = slice(h * HEAD_DIM, (h + 1) * HEAD_DIM)
        o = of_ref[:, sl] + ob_ref[:, sl]
        o = o * lax.rsqrt(jnp.mean(o * o, axis=-1, keepdims=True) + EPS) * ng
        z = z_ref[:, sl]
        y_ref[:, sl] = (o * (z * jax.nn.sigmoid(z))).astype(BF16)
    rec0, att0 = (0, half) if rec_first else (half, 0)
    y = jnp.dot(y_ref[...], w_ref[rec0:rec0 + half, :], preferred_element_type=F32)
    y = y + jnp.dot(att_ref[...].astype(BF16), w_ref[att0:att0 + half, :],
                    preferred_element_type=F32)
    o_ref[...] = x_ref[...] + gate_ref[...] * y


def _outproj_call(att, o_fw, o_bw, main, gate_group, norm_g, rec_first, w_out, x, mod3):
    bsz, length, d = x.shape
    half = att.shape[-1]
    tm = min(512, length)
    blk = lambda col: pl.BlockSpec((None, tm, half), lambda b, i: (b, i, col))
    return pl.pallas_call(
        functools.partial(_outproj_kernel, rec_first),
        out_shape=jax.ShapeDtypeStruct(x.shape, F32),
        grid=(bsz, length // tm),
        in_specs=[
            blk(0), blk(0), blk(0), blk(gate_group),
            pl.BlockSpec((1, HEAD_DIM), lambda b, i: (0, 0)),
            pl.BlockSpec((2 * half, d), lambda b, i: (0, 0)),
            pl.BlockSpec((None, tm, d), lambda b, i: (b, i, 0)),
            pl.BlockSpec((None, 1, d), lambda b, i: (b, 0, 2)),
        ],
        out_specs=pl.BlockSpec((None, tm, d), lambda b, i: (b, i, 0)),
        scratch_shapes=[pltpu.VMEM((tm, half), BF16)],
        compiler_params=_params(("arbitrary", "arbitrary")),
        name="outproj",
    )(att, o_fw, o_bw, main, norm_g.reshape(1, HEAD_DIM), w_out, x, mod3)


def _ffn_kernel(final, xp_ref, x_ref, xn_ref, g_ref, sh_ref, sc_ref, gate_ref, wg_ref, wu_ref,
                cw_ref, wd_ref, fn_ref, o_ref, h_ref, gp_ref):
    i = pl.program_id(1)
    j = pl.program_id(2)
    tm = x_ref.shape[0]

    @pl.when(j == 0)
    def _():
        g, sh, sc = g_ref[...], sh_ref[...], sc_ref[...]
        h_ref[HALO:HALO + tm, :] = _norm_mod(x_ref[...], g, sh, sc).astype(BF16)
        hp = _norm_mod(xp_ref[...], g, sh, sc)
        h_ref[0:HALO, :] = jnp.where(i == 0, 0.0, hp).astype(BF16)
        hn = _norm_mod(xn_ref[...], g, sh, sc)
        h_ref[HALO + tm:, :] = jnp.where(i == pl.num_programs(1) - 1, 0.0, hn).astype(BF16)
        o_ref[...] = jnp.zeros_like(o_ref)

    gp_ref[...] = jnp.dot(h_ref[...], wg_ref[...], preferred_element_type=F32)
    cw = cw_ref[...]
    a = gp_ref[pl.ds(HALO - 1, tm), :] * cw[0:1, :]
    a = a + gp_ref[pl.ds(HALO, tm), :] * cw[1:2, :]
    a = a + gp_ref[pl.ds(HALO + 1, tm), :] * cw[2:3, :]
    u = jnp.dot(h_ref[HALO:HALO + tm, :], wu_ref[...], preferred_element_type=F32)
    act = (a * jax.nn.sigmoid(a)) * u
    o_ref[...] += jnp.dot(act.astype(BF16), wd_ref[...], preferred_element_type=F32)

    @pl.when(j == pl.num_programs(2) - 1)
    def _():
        xo = x_ref[...] + gate_ref[...] * o_ref[...]
        if final:
            xo = xo * lax.rsqrt(jnp.mean(xo * xo, axis=-1, keepdims=True) + EPS) * fn_ref[...]
        o_ref[...] = xo


def _ffn_call(x, mod3, norm_g, w_gate, w_up, conv_w, w_down, final_g, final):
    bsz, length, d = x.shape
    tm = min(512, length)
    tf = 512
    nh = tm // HALO
    last_h = length // HALO - 1
    return pl.pallas_call(
        functools.partial(_ffn_kernel, final),
        out_shape=jax.ShapeDtypeStruct(x.shape, F32),
        grid=(bsz, length // tm, D_FF // tf),
        in_specs=[
            pl.BlockSpec((None, HALO, d), lambda b, i, j: (b, jnp.maximum(i * nh - 1, 0), 0)),
            pl.BlockSpec((None, tm, d), lambda b, i, j: (b, i, 0)),
            pl.BlockSpec((None, HALO, d), lambda b, i, j: (b, jnp.minimum((i + 1) * nh, last_h), 0)),
            pl.BlockSpec((1, d), lambda b, i, j: (0, 0)),
            pl.BlockSpec((None, 1, d), lambda b, i, j: (b, 0, 3)),
            pl.BlockSpec((None, 1, d), lambda b, i, j: (b, 0, 4)),
            pl.BlockSpec((None, 1, d), lambda b, i, j: (b, 0, 5)),
            pl.BlockSpec((d, tf), lambda b, i, j: (0, j)),
            pl.BlockSpec((d, tf), lambda b, i, j: (0, j)),
            pl.BlockSpec((FFN_CONV_K, tf), lambda b, i, j: (0, j)),
            pl.BlockSpec((tf, d), lambda b, i, j: (j, 0)),
            pl.BlockSpec((1, d), lambda b, i, j: (0, 0)),
        ],
        out_specs=pl.BlockSpec((None, tm, d), lambda b, i, j: (b, i, 0)),
        scratch_shapes=[pltpu.VMEM((tm + 2 * HALO, d), BF16),
                        pltpu.VMEM((tm + 2 * HALO, tf), F32)],
        compiler_params=_params(("arbitrary", "arbitrary", "arbitrary")),
        name="conv_ffn",
    )(x, x, x, norm_g.reshape(1, d), mod3, mod3, mod3, w_gate, w_up, conv_w, w_down,
      final_g.reshape(1, d))


NA_KEYS = NA_WIN_R * GRID_W
MASKED = -1e30
NA_ROWS_PER_STEP = 8


def _na_bias_table(rpb):
    qc = np.arange(GRID_W)[:, None]
    kc = np.arange(GRID_W)[None, :]
    cs = np.clip(qc - NA_WIN_C // 2, 0, GRID_W - NA_WIN_C)
    valid = (kc >= cs) & (kc < cs + NA_WIN_C)
    dc = np.clip(kc - qc + NA_WIN_C - 1, 0, 2 * NA_WIN_C - 2)
    dr = np.arange(NA_WIN_R)[None, :] - np.arange(NA_WIN_R)[:, None] + NA_WIN_R - 1
    t = rpb.astype(F32)[:, dr][:, :, :, dc]
    t = jnp.where(valid[None, None, None], t, MASKED)
    return t.transpose(0, 1, 3, 2, 4).reshape(rpb.shape[0], NA_WIN_R, GRID_W, NA_KEYS)


def _na_kernel(q_ref, k_ref, v_ref, bias_ref, o_ref, kb_ref, vb_ref):
    rows = q_ref.shape[0] // GRID_W
    kb_ref[...] = k_ref[...].astype(BF16)
    vb_ref[...] = v_ref[...].astype(BF16)

    def row_group(g, carry):
        todo = []
        for u in range(NA_ROWS_PER_STEP):
            r = g * NA_ROWS_PER_STEP + u
            rs = jnp.clip(r - NA_WIN_R // 2, 0, rows - NA_WIN_R)
            q0 = pl.multiple_of(r * GRID_W, GRID_W)
            k0 = pl.multiple_of(rs * GRID_W, GRID_W)
            q = (q_ref[pl.ds(q0, GRID_W), :] * (HEAD_DIM ** -0.5)).astype(BF16)
            s = lax.dot_general(q, kb_ref[pl.ds(k0, NA_KEYS), :], (((1,), (1,)), ((), ())),
                                preferred_element_type=F32)
            todo.append((q0, k0, s, r - rs))
        for u, (q0, k0, s, off) in enumerate(todo):
            s = s + bias_ref[off]
            p = jnp.exp(s - jnp.max(s, axis=-1, keepdims=True))
            todo[u] = (q0, k0, p.astype(BF16), jnp.sum(p, axis=-1, keepdims=True))
        for u, (q0, k0, p, l) in enumerate(todo):
            todo[u] = (q0, jnp.dot(p, vb_ref[pl.ds(k0, NA_KEYS), :], preferred_element_type=F32), l)
        for q0, o, l in todo:
            o_ref[pl.ds(q0, GRID_W), :] = o / l
        return carry

    lax.fori_loop(0, rows // NA_ROWS_PER_STEP, row_group, 0)


def _na_call(main, rpb):
    bsz, length, _ = main.shape
    assert length % (GRID_W * NA_ROWS_PER_STEP) == 0 and length // GRID_W >= NA_WIN_R
    hd = HEAD_DIM
    return pl.pallas_call(
        _na_kernel,
        out_shape=jax.ShapeDtypeStruct((bsz, length, GROUP_W), F32),
        grid=(bsz, H_GROUP),
        in_specs=[
            pl.BlockSpec((None, length, hd), lambda b, h: (b, 0, h)),
            pl.BlockSpec((None, length, hd), lambda b, h: (b, 0, H_GROUP + h)),
            pl.BlockSpec((None, length, hd), lambda b, h: (b, 0, 2 * H_GROUP + h)),
            pl.BlockSpec((None, NA_WIN_R, GRID_W, NA_KEYS), lambda b, h: (h, 0, 0, 0)),
        ],
        out_specs=pl.BlockSpec((None, length, hd), lambda b, h: (b, 0, h)),
        scratch_shapes=[pltpu.VMEM((length, hd), BF16), pltpu.VMEM((length, hd), BF16)],
        compiler_params=_params(("arbitrary", "arbitrary")),
        name="nbr_attn",
    )(main, main, main, _na_bias_table(rpb))


REC_TILE = 512
REC_HEADS = 8
HIGHEST = lax.Precision.HIGHEST


def _tri(n, upper):
    r = lax.broadcasted_iota(jnp.int32, (n, n), 0)
    c = lax.broadcasted_iota(jnp.int32, (n, n), 1)
    return (c >= r) if upper else (c <= r)


def _gdn_conv_kernel(x_ref, w_ref, o_ref, xp_ref):
    length, cw = x_ref.shape
    col0 = pl.program_id(1) * cw
    pad = 8
    xp_ref[0:pad, :] = jnp.zeros((pad, cw), F32)
    xp_ref[pad + length:, :] = jnp.zeros((pad, cw), F32)
    xp_ref[pad:pad + length, :] = x_ref[...]
    w = w_ref[...]
    y = xp_ref[pl.ds(pad - DN_CONV_K // 2, length), :] * w[0:1, :]
    for t in range(1, DN_CONV_K):
        y = y + xp_ref[pl.ds(pad - DN_CONV_K // 2 + t, length), :] * w[t:t + 1, :]
    y = y * jax.nn.sigmoid(y)
    qscale = jnp.where(col0 < GROUP_W, HEAD_DIM ** -0.5, 1.0)
    for h in range(cw // HEAD_DIM):
        yh = y[:, h * HEAD_DIM:(h + 1) * HEAD_DIM]
        nh = yh * (lax.rsqrt(jnp.sum(yh * yh, axis=-1, keepdims=True) + EPS) * qscale)
        o_ref[:, h * HEAD_DIM:(h + 1) * HEAD_DIM] = jnp.where(col0 < 2 * GROUP_W, nh, yh)


def _gdn_conv_call(main, conv_w):
    bsz, length, _ = main.shape
    cw = HEAD_DIM
    nblk = 3 * GROUP_W // cw
    return pl.pallas_call(
        _gdn_conv_kernel,
        out_shape=jax.ShapeDtypeStruct((bsz, length, 3 * GROUP_W), F32),
        grid=(bsz, nblk),
        in_specs=[pl.BlockSpec((None, length, cw), lambda b, j: (b, 0, nblk + j)),
                  pl.BlockSpec((DN_CONV_K, cw), lambda b, j: (0, j))],
        out_specs=pl.BlockSpec((None, length, cw), lambda b, j: (b, 0, j)),
        scratch_shapes=[pltpu.VMEM((length + 16, cw), F32)],
        compiler_params=_params(("arbitrary", "arbitrary")),
        name="gdn_conv",
    )(main, conv_w)


def _gdn_gates_kernel(s_ref, alog_ref, dt_ref, g_ref, gt_ref):
    tm = s_ref.shape[0]
    s = s_ref[...]
    x = s + dt_ref[...]
    softplus = jnp.maximum(x, 0.0) + jnp.log1p(jnp.exp(-jnp.abs(x)))
    g = -jnp.exp(alog_ref[...]) * softplus
    r = lax.broadcasted_iota(jnp.int32, (tm, tm), 0)
    c = lax.broadcasted_iota(jnp.int32, (tm, tm), 1)
    same = (r // DN_CHUNK) == (c // DN_CHUNK)
    lo = jnp.where(same & (c <= r), 1.0, 0.0)
    up = jnp.where(same & (c >= r), 1.0, 0.0)
    gcf = jnp.dot(lo, g, precision=HIGHEST, preferred_element_type=F32)
    gcb = jnp.dot(up, g, precision=HIGHEST, preferred_element_type=F32)
    lane = lax.broadcasted_iota(jnp.int32, s.shape, 1)
    out = jnp.where(lane < H_GROUP, gcf, jnp.where(lane < 2 * H_GROUP, gcb, jax.nn.sigmoid(s)))
    g_ref[...] = out
    for p in range(tm // 128):
        gt_ref[p] = out[p * 128:(p + 1) * 128, :].T[0:4 * H_GROUP, :]


def _gdn_gates_call(side, a_log, dt_bias):
    bsz, length, _ = side.shape
    tm = min(512, length)
    row = lambda t: jnp.pad(t.reshape(1, 2 * H_GROUP).astype(F32), ((0, 0), (0, SIDE_W - 2 * H_GROUP)))
    return pl.pallas_call(
        _gdn_gates_kernel,
        out_shape=(jax.ShapeDtypeStruct((bsz, length, SIDE_W), F32),
                   jax.ShapeDtypeStruct((bsz, length // 128, 4 * H_GROUP, 128), F32)),
        grid=(bsz, length // tm),
        in_specs=[pl.BlockSpec((None, tm, SIDE_W), lambda b, i: (b, i, 0)),
                  pl.BlockSpec((1, SIDE_W), lambda b, i: (0, 0)),
                  pl.BlockSpec((1, SIDE_W), lambda b, i: (0, 0))],
        out_specs=(pl.BlockSpec((None, tm, SIDE_W), lambda b, i: (b, i, 0)),
                   pl.BlockSpec((None, tm // 128, 4 * H_GROUP, 128), lambda b, i: (b, i, 0, 0))),
        compiler_params=_params(("arbitrary", "arbitrary")),
        name="gdn_gates",
    )(side, row(a_log), row(dt_bias))


def _bdot(a, b):
    return jnp.dot(a.astype(BF16), b.astype(BF16), preferred_element_type=F32)


def _gdn_chunks_local(chunks):
    c = DN_CHUNK
    r = lax.broadcasted_iota(jnp.int32, (c, c), 0)
    cc = lax.broadcasted_iota(jnp.int32, (c, c), 1)
    for d in chunks:
        incl = (cc >= r) if d["rev"] else (cc <= r)
        d["gam"] = jnp.exp(jnp.where(incl, d["gci"] - d["gcj"], MASKED))
        d["kb"] = d["k"] * d["beta"]
    for d in chunks:
        kq = jnp.concatenate([d["kb"], d["q"]], axis=0).astype(BF16)
        d["prod"] = lax.dot_general(kq, d["k"].astype(BF16), (((1,), (1,)), ((), ())),
                                    preferred_element_type=F32)
    for d in chunks:
        strict = (cc > r) if d["rev"] else (cc < r)
        d["aqk"] = d["prod"][c:] * d["gam"]
        d["p"] = -jnp.where(strict, d["prod"][0:c] * d["gam"], 0.0)
        d["t"] = jnp.where(r == cc, 1.0, 0.0) + d["p"]
    for _ in range(int(math.log2(c)) - 1):
        for d in chunks:
            d["p"] = _bdot(d["p"], d["p"])
        for d in chunks:
            d["t"] = d["t"] + _bdot(d["t"], d["p"])
    for d in chunks:
        egc = jnp.exp(d["gci"])
        rhs = jnp.concatenate([d["v"] * d["beta"], d["kb"] * egc], axis=1)
        uw = _bdot(d["t"], rhs)
        d["u"], d["w"] = uw[:, 0:HEAD_DIM], uw[:, HEAD_DIM:]
        glast = d["gci"][0:1, :] if d["rev"] else d["gci"][c - 1:c, :]
        d["qd"] = d["q"] * egc
        d["kd"] = d["k"] * jnp.exp(glast - d["gci"])
        d["dlast"] = jnp.exp(glast)


def _gdn_chunks_scan(chunks):
    c = DN_CHUNK
    for d in chunks:
        d["s"] = d["s_ref"][...]
        d["ws"] = _bdot(jnp.concatenate([d["w"], d["qd"]], axis=0), d["s"])
    for d in chunks:
        d["vn"] = (d["u"] - d["ws"][0:c]).astype(BF16)
        d["o"] = d["ws"][c:] + _bdot(d["aqk"], d["vn"])
    for d in chunks:
        d["s_ref"][...] = d["s"] * d["dlast"] + lax.dot_general(
            d["kd"].astype(BF16), d["vn"], (((0,), (0,)), ((), ())), preferred_element_type=F32)


def _gdn_kernel(qf_ref, kf_ref, vf_ref, gf_ref, gtf_ref, qb_ref, kb_ref, vb_ref, gb_ref, gtb_ref,
                of_ref, ob_ref, s_ref):
    hg = pl.program_id(1)

    @pl.when(pl.program_id(2) == 0)
    def _():
        s_ref[...] = jnp.zeros_like(s_ref)

    npair = qf_ref.shape[0] // 128
    c = DN_CHUNK

    def pair(p, carry):
        chunks = []
        for rev in (False, True):
            q_ref, k_ref, v_ref, g_ref, gt_ref, o_ref = (
                (qb_ref, kb_ref, vb_ref, gb_ref, gtb_ref, ob_ref) if rev
                else (qf_ref, kf_ref, vf_ref, gf_ref, gtf_ref, of_ref))
            pp = (npair - 1 - p) if rev else p
            base = pl.multiple_of(pp * 128, 128)
            gt = gt_ref[pp]
            sub = lax.broadcasted_iota(jnp.int32, gt.shape, 0)
            for step, half in enumerate((1, 0) if rev else (0, 1)):
                rows = pl.ds(base + half * c, c)
                g = g_ref[rows, :]
                lane = lax.broadcasted_iota(jnp.int32, g.shape, 1)
                for hl in range(REC_HEADS):
                    cols = slice(hl * HEAD_DIM, (hl + 1) * HEAD_DIM)
                    gcol = hg * REC_HEADS + hl + (H_GROUP if rev else 0)
                    chunks.append(dict(
                        rev=rev, step=step, rows=rows, cols=cols, o_ref=o_ref,
                        s_ref=s_ref.at[int(rev), hl],
                        q=q_ref[rows, cols], k=k_ref[rows, cols], v=v_ref[rows, cols],
                        gci=jnp.sum(jnp.where(lane == gcol, g, 0.0), axis=-1, keepdims=True),
                        beta=jnp.sum(jnp.where(lane == gcol + 2 * H_GROUP, g, 0.0), axis=-1,
                                     keepdims=True),
                        gcj=jnp.sum(jnp.where(sub == gcol, gt, 0.0), axis=0,
                                    keepdims=True)[:, half * c:(half + 1) * c]))
        _gdn_chunks_local(chunks)
        for step in (0, 1):
            now = [d for d in chunks if d["step"] == step]
            _gdn_chunks_scan(now)
            for d in now:
                d["o_ref"][d["rows"], d["cols"]] = d["o"]
        return carry

    lax.fori_loop(0, npair, pair, 0)


def _gdn_call(main, side, conv_w, a_log, dt_bias):
    bsz, length, _ = main.shape
    tl = min(REC_TILE, length)
    nt = length // tl
    w = REC_HEADS * HEAD_DIM
    ng = H_GROUP // REC_HEADS
    qkv = _gdn_conv_call(main, conv_w)
    gates, gates_t = _gdn_gates_call(side, a_log, dt_bias)

    def specs(tile):
        return [pl.BlockSpec((None, tl, w), lambda b, g, i: (b, tile(i), g)),
                pl.BlockSpec((None, tl, w), lambda b, g, i: (b, tile(i), ng + g)),
                pl.BlockSpec((None, tl, w), lambda b, g, i: (b, tile(i), 2 * ng + g)),
                pl.BlockSpec((None, tl, SIDE_W), lambda b, g, i: (b, tile(i), 0)),
                pl.BlockSpec((None, tl // 128, 4 * H_GROUP, 128), lambda b, g, i: (b, tile(i), 0, 0))]

    fw = lambda i: i
    bw = lambda i: nt - 1 - i
    return pl.pallas_call(
        _gdn_kernel,
        out_shape=(jax.ShapeDtypeStruct((bsz, length, GROUP_W), F32),) * 2,
        grid=(bsz, ng, nt),
        in_specs=specs(fw) + specs(bw),
        out_specs=(pl.BlockSpec((None, tl, w), lambda b, g, i: (b, fw(i), g)),
                   pl.BlockSpec((None, tl, w), lambda b, g, i: (b, bw(i), g))),
        scratch_shapes=[pltpu.VMEM((2, REC_HEADS, HEAD_DIM, HEAD_DIM), F32)],
        compiler_params=_params(("arbitrary", "arbitrary", "arbitrary")),
        name="gdn_scan",
    )(qkv, qkv, qkv, gates, gates_t, qkv, qkv, qkv, gates, gates_t)


HG_TILE = 64
HG_SUB = 8
HG_NSUB = HG_TILE // HG_SUB
LOG2_E = 1.4426950408889634


def _tri_cumsum(x, rev):
    n = x.shape[0]
    tri = jnp.where(_tri(n, rev), 1.0, 0.0).astype(BF16)
    hi = x.astype(BF16)
    r1 = x - hi.astype(F32)
    mid = r1.astype(BF16)
    lo = (r1 - mid.astype(F32)).astype(BF16)
    parts = jnp.dot(tri, jnp.concatenate([hi, mid, lo], axis=1), preferred_element_type=F32)
    w = x.shape[1]
    return (parts[:, 0:w] + parts[:, w:2 * w]) + parts[:, 2 * w:]


def _hg_chunks(chains):
    n, sub, nsub = HG_TILE, HG_SUB, HG_NSUB
    row = lax.broadcasted_iota(jnp.int32, (n, HEAD_DIM), 0)
    pos = row % sub
    tn_dims = (((0,), (0,)), ((), ()))
    for c in chains:
        z, loglb, log1mlb = c["z"], c["loglb"], c["log1mlb"]
        c["q"] = c["q_raw"] * jax.nn.sigmoid(c["q_raw"])
        log_sig = jnp.minimum(z, 0.0) - jnp.log1p(jnp.exp(-jnp.abs(z)))
        bb = log1mlb + log_sig
        lf = jnp.maximum(loglb, bb) + jnp.log1p(jnp.exp(-jnp.abs(loglb - bb)))
        b = _tri_cumsum(lf, c["rev"])
        c["b2"] = b * LOG2_E
        c["c2"] = (b - (bb - z)) * LOG2_E
        c["inp16"] = c["inp"].astype(BF16)

    for c in chains:
        b2, c2, rev = c["b2"], c["c2"], c["rev"]
        c["order"] = list(range(nsub - 1, -1, -1)) if rev else list(range(nsub))
        c["ends"], c["contrib"] = [], []
        for j in c["order"]:
            lo = j * sub
            bend = b2[lo:lo + 1, :] if rev else b2[lo + sub - 1:lo + sub, :]
            c["ends"].append(bend)
            if len(c["contrib"]) < nsub - 1:
                ks = jnp.exp2(bend - c2[lo:lo + sub, :])
                c["contrib"].append(lax.dot_general(c["inp16"][lo:lo + sub, :], ks.astype(BF16),
                                                    tn_dims, preferred_element_type=F32))

    for c in chains:
        b2, c2, q, inp, rev = c["b2"], c["c2"], c["q"], c["inp"], c["rev"]
        o = jnp.sum(q * jnp.exp2(b2 - c2), axis=-1, keepdims=True) * inp
        for d in range(1, sub):
            shift = (sub - d) % sub if rev else d
            valid = (pos <= sub - 1 - d) if rev else (pos >= d)
            sh = lambda x: pltpu.roll(x.reshape(nsub, sub, HEAD_DIM), shift, axis=1).reshape(
                n, HEAD_DIM)
            e = jnp.where(valid, jnp.exp2(b2 - sh(c2)), 0.0)
            o = o + jnp.sum(q * e, axis=-1, keepdims=True) * sh(inp)
        c["o"] = o

    for c in chains:
        b2, q, rev = c["b2"], c["q"], c["rev"]
        c["s"] = c["s_ref"][...]
        qext = [q * jnp.exp2(b2)]
        for m in range(nsub - 1):
            j = c["order"][m]
            lo, hi = (0, j * sub) if rev else ((j + 1) * sub, n)
            part = q[lo:hi] * jnp.exp2(b2[lo:hi] - c["ends"][m])
            zeros = jnp.zeros((n - (hi - lo), HEAD_DIM), F32)
            qext.append(jnp.concatenate([part, zeros] if rev else [zeros, part], axis=0))
        ncat = jnp.concatenate([c["s"]] + c["contrib"], axis=1).astype(BF16)
        c["o"] = c["o"] + lax.dot_general(jnp.concatenate(qext, axis=1).astype(BF16), ncat,
                                          (((1,), (1,)), ((), ())), preferred_element_type=F32)
    for c in chains:
        btot = c["ends"][-1]
        kd = jnp.exp2(btot - c["c2"]).astype(BF16)
        c["s_ref"][...] = c["s"] * jnp.exp2(btot) + lax.dot_general(
            c["inp16"], kd, tn_dims, preferred_element_type=F32)


def _hgrn2_kernel(qf_ref, ff_ref, if_ref, qb_ref, fb_ref, ib_ref, lbf_ref, lbb_ref,
                  of_ref, ob_ref, s_ref):
    @pl.when(pl.program_id(2) == 0)
    def _():
        s_ref[...] = jnp.zeros_like(s_ref)

    nstep = qf_ref.shape[0] // HG_TILE

    def step(t, carry):
        chains = []
        for rev in (False, True):
            q_ref, f_ref, i_ref, lb_ref, o_ref = ((qb_ref, fb_ref, ib_ref, lbb_ref, ob_ref) if rev
                                                  else (qf_ref, ff_ref, if_ref, lbf_ref, of_ref))
            tt = (nstep - 1 - t) if rev else t
            rows = pl.ds(pl.multiple_of(tt * HG_TILE, HG_TILE), HG_TILE)
            for hl in range(REC_HEADS):
                cols = slice(hl * HEAD_DIM, (hl + 1) * HEAD_DIM)
                chains.append(dict(rev=rev, rows=rows, cols=cols, o_ref=o_ref,
                                   s_ref=s_ref.at[int(rev), hl], q_raw=q_ref[rows, cols],
                                   z=f_ref[rows, cols], inp=i_ref[rows, cols],
                                   loglb=lb_ref[0:1, cols], log1mlb=lb_ref[1:2, cols]))
        _hg_chunks(chains)
        for c in chains:
            c["o_ref"][c["rows"], c["cols"]] = c["o"]
        return carry

    lax.fori_loop(0, nstep, step, 0)


def _hgrn2_call(main, lb_fw, lb_bw):
    bsz, length, _ = main.shape
    tl = min(REC_TILE, length)
    nt = length // tl
    w = REC_HEADS * HEAD_DIM
    ng = H_GROUP // REC_HEADS
    fw = lambda i: i
    bw = lambda i: nt - 1 - i
    col = lambda group, tile: pl.BlockSpec((None, tl, w), lambda b, g, i: (b, tile(i), group * ng + g))
    lbrow = pl.BlockSpec((2, w), lambda b, g, i: (0, g))
    logs = lambda lb: jnp.stack([jnp.log(lb), jnp.log1p(-lb)])
    return pl.pallas_call(
        _hgrn2_kernel,
        out_shape=(jax.ShapeDtypeStruct((bsz, length, GROUP_W), F32),) * 2,
        grid=(bsz, ng, nt),
        in_specs=[col(0, fw), col(1, fw), col(3, fw), col(0, bw), col(2, bw), col(3, bw), lbrow, lbrow],
        out_specs=(pl.BlockSpec((None, tl, w), lambda b, g, i: (b, fw(i), g)),
                   pl.BlockSpec((None, tl, w), lambda b, g, i: (b, bw(i), g))),
        scratch_shapes=[pltpu.VMEM((2, REC_HEADS, HEAD_DIM, HEAD_DIM), F32)],
        compiler_params=_params(("arbitrary", "arbitrary", "arbitrary")),
        name="hgrn2_scan",
    )(main, main, main, main, main, main, logs(lb_fw), logs(lb_bw))


MLA_QK = 2 * MLA_NOPE
MLA_KEY_BLOCK = 512
ROPE_SWAP = np.concatenate([np.arange(MLA_ROPE // 2, MLA_ROPE), np.arange(MLA_ROPE // 2)])


def _rope_table(length):
    half = MLA_ROPE // 2
    inv = ROPE_BASE ** (-jnp.arange(half, dtype=F32) / half)
    ang = jnp.arange(length, dtype=F32)[:, None] * inv[None, :]
    cos, sin = jnp.cos(ang), jnp.sin(ang)
    return jnp.concatenate([cos, cos, -sin, sin], axis=-1)


def _mla_q_weights(w_uq):
    w = w_uq.reshape(MLA_Q_LORA, H_GROUP, MLA_NOPE + MLA_ROPE)
    rope = w[:, :, MLA_NOPE:]
    w = jnp.concatenate([w[:, :, :MLA_NOPE], rope, rope[:, :, ROPE_SWAP]], axis=-1)
    return w.reshape(MLA_Q_LORA, H_GROUP * MLA_QK).astype(BF16)


def _plain_rmsnorm(x, g):
    return x * lax.rsqrt(jnp.mean(x * x, axis=-1, keepdims=True) + EPS) * g


def _mla_proj_kernel(cq_ref, ckv_ref, side_ref, tab_ref, qn_ref, kvn_ref, wq_ref, wkv_ref,
                     q_out, k_out, v_out):
    scale = (MLA_NOPE + MLA_ROPE) ** -0.5
    cqn = _plain_rmsnorm(cq_ref[...], qn_ref[...]).astype(BF16)
    ckvn = _plain_rmsnorm(ckv_ref[...], kvn_ref[...]).astype(BF16)
    q = jnp.dot(cqn, wq_ref[...], preferred_element_type=F32)
    kv = jnp.dot(ckvn, wkv_ref[...], preferred_element_type=F32)
    tab = tab_ref[...]
    kr = (side_ref[...] * tab).astype(BF16)
    for h in range(H_GROUP):
        c0 = h * MLA_QK
        qr = q[:, c0 + MLA_NOPE:c0 + MLA_QK] * tab
        qr = qr + pltpu.roll(qr, MLA_ROPE, axis=1)
        q_out[h, :, 0:MLA_NOPE] = (q[:, c0:c0 + MLA_NOPE] * scale).astype(BF16)
        q_out[h, :, MLA_NOPE:MLA_QK] = (qr * scale).astype(BF16)
        k_out[h, :, 0:MLA_NOPE] = kv[:, c0:c0 + MLA_NOPE].astype(BF16)
        k_out[h, :, MLA_NOPE:MLA_QK] = kr
        v_out[h] = kv[:, c0 + MLA_NOPE:c0 + MLA_QK].astype(BF16)


def _mla_attn_kernel(q_ref, k_ref, v_ref, o_ref):
    length = k_ref.shape[0]
    tk = MLA_KEY_BLOCK
    nk = length // tk
    q = q_ref[...]

    def scores(kb):
        return lax.dot_general(q, k_ref[kb * tk:(kb + 1) * tk, :], (((1,), (1,)), ((), ())),
                               preferred_element_type=F32)

    s_next = scores(0)
    m = l = acc = None
    for kb in range(nk):
        s = s_next
        if kb + 1 < nk:
            s_next = scores(kb + 1)
        m_blk = jnp.max(s, axis=-1, keepdims=True)
        if kb == 0:
            m = m_blk
            p = jnp.exp(s - m)
            l = jnp.sum(p, axis=-1, keepdims=True)
            acc = jnp.dot(p.astype(BF16), v_ref[0:tk, :], preferred_element_type=F32)
        else:
            m_new = jnp.maximum(m, m_blk)
            alpha = jnp.exp(m - m_new)
            p = jnp.exp(s - m_new)
            l = alpha * l + jnp.sum(p, axis=-1, keepdims=True)
            acc = alpha * acc + jnp.dot(p.astype(BF16), v_ref[kb * tk:(kb + 1) * tk, :],
                                        preferred_element_type=F32)
            m = m_new
    o_ref[...] = acc / l


def _mla_call(main, side, q_norm, w_uq, kv_norm, w_ukv):
    bsz, length, _ = main.shape
    tm = min(512, length)
    lora = MLA_Q_LORA
    q_all, k_all, v_all = pl.pallas_call(
        _mla_proj_kernel,
        out_shape=(jax.ShapeDtypeStruct((bsz, H_GROUP, length, MLA_QK), BF16),
                   jax.ShapeDtypeStruct((bsz, H_GROUP, length, MLA_QK), BF16),
                   jax.ShapeDtypeStruct((bsz, H_GROUP, length, MLA_V), BF16)),
        grid=(bsz, length // tm),
        in_specs=[
            pl.BlockSpec((None, tm, lora), lambda b, i: (b, i, 10)),
            pl.BlockSpec((None, tm, lora), lambda b, i: (b, i, 11)),
            pl.BlockSpec((None, tm, SIDE_W), lambda b, i: (b, i, 0)),
            pl.BlockSpec((tm, SIDE_W), lambda b, i: (i, 0)),
            pl.BlockSpec((1, lora), lambda b, i: (0, 0)),
            pl.BlockSpec((1, lora), lambda b, i: (0, 0)),
            pl.BlockSpec((lora, H_GROUP * MLA_QK), lambda b, i: (0, 0)),
            pl.BlockSpec((lora, H_GROUP * MLA_QK), lambda b, i: (0, 0)),
        ],
        out_specs=(pl.BlockSpec((None, H_GROUP, tm, MLA_QK), lambda b, i: (b, 0, i, 0)),
                   pl.BlockSpec((None, H_GROUP, tm, MLA_QK), lambda b, i: (b, 0, i, 0)),
                   pl.BlockSpec((None, H_GROUP, tm, MLA_V), lambda b, i: (b, 0, i, 0))),
        compiler_params=_params(("arbitrary", "arbitrary")),
        name="mla_proj",
    )(main, main, side, _rope_table(length), q_norm.reshape(1, lora), kv_norm.reshape(1, lora),
      _mla_q_weights(w_uq), w_ukv.astype(BF16))

    tq = 512
    return pl.pallas_call(
        _mla_attn_kernel,
        out_shape=jax.ShapeDtypeStruct((bsz, length, H_GROUP * MLA_V), F32),
        grid=(bsz, H_GROUP, length // tq),
        in_specs=[
            pl.BlockSpec((None, None, tq, MLA_QK), lambda b, h, i: (b, h, i, 0)),
            pl.BlockSpec((None, None, length, MLA_QK), lambda b, h, i: (b, h, 0, 0)),
            pl.BlockSpec((None, None, length, MLA_V), lambda b, h, i: (b, h, 0, 0)),
        ],
        out_specs=pl.BlockSpec((None, tq, MLA_V), lambda b, h, i: (b, i, h)),
        compiler_params=_params(("arbitrary", "arbitrary", "arbitrary")),
        name="mla_attn",
    )(q_all, k_all, v_all)


def _pad_side(w):
    return jnp.pad(w, ((0, 0), (0, SIDE_W - w.shape[1]))).astype(BF16)


def _trunk(x, mods, lb, p):
    for l in range(DEPTH):
        mod3 = mods[l]
        if l % 2 == 0:
            e = l // 2
            main, side = _inproj_call(x, mod3, p["norm_mix"][l], p["w_in_ab_main"][e],
                                      p["w_in_ab_side"][e])
            att = _na_call(main, p["na_rpb"][e])
            o_fw, o_bw = _gdn_call(main, side, p["dn_conv"][e], p["dn_a_log"][e], p["dn_dt_bias"][e])
            x = _outproj_call(att, o_fw, o_bw, main, 6, p["dn_norm"][e], False, p["w_out_ab"][e],
                              x, mod3)
        else:
            o = l // 2
            main, side = _inproj_call(x, mod3, p["norm_mix"][l], p["w_in_cd_main"][o],
                                      p["w_in_cd_side"][o])
            o_fw, o_bw = _hgrn2_call(main, lb[0, l], lb[1, l])
            att = _mla_call(main, side, p["mla_q_norm"][o], p["mla_w_uq"][o], p["mla_kv_norm"][o],
                            p["mla_w_ukv"][o])
            x = _outproj_call(att, o_fw, o_bw, main, 4, p["hg_norm"][o], True, p["w_out_cd"][o],
                              x, mod3)
        x = _ffn_call(x, mod3, p["norm_ffn"][l], p["ffn_w_gate"][l], p["ffn_w_up"][l],
                      p["ffn_conv"][l], p["ffn_w_down"][l], p["final_norm"], l == DEPTH - 1)
    return x


def kernel(x_prompt, x_sample, c_prompt, c_sample, norm_mix, norm_ffn, w_ada, b_ada, w_in_ab, w_out_ab, na_rpb, dn_conv, dn_a_log, dn_dt_bias, dn_norm, w_in_cd, w_out_cd, hg_lower_bounds, hg_norm, mla_q_norm, mla_w_uq, mla_kv_norm, mla_w_ukv, ffn_w_gate, ffn_w_up, ffn_conv, ffn_w_down, final_norm):
    n_p, n_s = c_prompt.shape[0], c_sample.shape[0]
    rows = -(-(n_p + n_s) // 8) * 8
    c_all = jnp.concatenate([c_prompt, c_sample, jnp.zeros((rows - n_p - n_s, D_MODEL), F32)], axis=0)
    mod_all = _ada_call(c_all, w_ada, b_ada)
    mods_p = [mod_all[l, :n_p].reshape(n_p, 1, 6 * D_MODEL) for l in range(DEPTH)]
    mods_s = [mod_all[l, n_p:n_p + n_s].reshape(n_s, 1, 6 * D_MODEL) for l in range(DEPTH)]

    lb = jnp.cumsum(jax.nn.softmax(hg_lower_bounds.astype(F32), axis=1), axis=1)
    lb = lb - lb[:, :1]

    p = dict(
        norm_mix=norm_mix, norm_ffn=norm_ffn, na_rpb=na_rpb, dn_conv=dn_conv, dn_a_log=dn_a_log,
        dn_dt_bias=dn_dt_bias, dn_norm=dn_norm, hg_norm=hg_norm, mla_q_norm=mla_q_norm,
        mla_w_uq=mla_w_uq, mla_kv_norm=mla_kv_norm, mla_w_ukv=mla_w_ukv, ffn_conv=ffn_conv,
        final_norm=final_norm,
        w_in_ab_main=w_in_ab[:, :, :AB_MAIN].astype(BF16),
        w_in_ab_side=jax.vmap(_pad_side)(w_in_ab[:, :, AB_MAIN:]),
        w_in_cd_main=w_in_cd[:, :, :CD_MAIN].astype(BF16),
        w_in_cd_side=jnp.concatenate([w_in_cd[:, :, CD_MAIN:],
                                      w_in_cd[:, :, CD_MAIN:][:, :, ROPE_SWAP]], axis=-1).astype(BF16),
        w_out_ab=w_out_ab.astype(BF16), w_out_cd=w_out_cd.astype(BF16),
        ffn_w_gate=ffn_w_gate.astype(BF16), ffn_w_up=ffn_w_up.astype(BF16),
        ffn_w_down=ffn_w_down.astype(BF16),
    )
    y_prompt = _trunk(x_prompt, mods_p, lb, p)
    y_sample = _trunk(x_sample, mods_s, lb, p)
    return (y_prompt, y_sample)
```

```python
import functools
import math

import numpy as np
import jax
import jax.numpy as jnp
from jax import lax
from jax.experimental import pallas as pl
from jax.experimental.pallas import tpu as pltpu

D_MODEL = 2048
DEPTH = 4
HEAD_DIM = 128
H_GROUP = 8
GROUP_W = H_GROUP * HEAD_DIM
GRID_W = 64
NA_WIN_R = 8
NA_WIN_C = 16
NA_QCB = NA_WIN_C
NA_BAND = 2 * NA_WIN_C
DN_CONV_K = 5
DN_CHUNK = 64
HG_CHUNK = 32
MLA_Q_LORA = 512
MLA_KV_LORA = 512
MLA_NOPE = 128
MLA_ROPE = 64
MLA_V = 128
MLA_BLOCK_Q = 128
ROPE_BASE = 10000.0
D_FF = 5632
FFN_CONV_K = 3
EPS = 1e-6
AB_SPLIT = [GROUP_W, GROUP_W, GROUP_W, 3 * GROUP_W, GROUP_W, H_GROUP, H_GROUP, H_GROUP, H_GROUP]
CD_SPLIT = [GROUP_W, GROUP_W, GROUP_W, GROUP_W, GROUP_W, MLA_Q_LORA, MLA_KV_LORA, MLA_ROPE]
AB_MAIN = 7 * GROUP_W
CD_MAIN = 6 * GROUP_W
SIDE_W = 128

F32 = jnp.float32
BF16 = jnp.bfloat16
LOG2_E = 1.4426950408889634

VMEM_LIMIT_BYTES = 56 * 1024 * 1024
HALO = 16

def _params(sem):
    return pltpu.CompilerParams(dimension_semantics=sem, vmem_limit_bytes=VMEM_LIMIT_BYTES)


def _ada_kernel(c_ref, w_ref, b_ref, o_ref):
    c = c_ref[...]
    cond = c * jax.nn.sigmoid(c)
    o_ref[...] = jnp.dot(cond, w_ref[...], precision=lax.Precision.HIGHEST,
                         preferred_element_type=F32) + b_ref[...]


def _ada_call(c_all, w_ada, b_ada):
    rows = c_all.shape[0]
    tn = 1024
    return pl.pallas_call(
        _ada_kernel,
        out_shape=jax.ShapeDtypeStruct((DEPTH, rows, 6 * D_MODEL), F32),
        grid=(DEPTH, 6 * D_MODEL // tn),
        in_specs=[
            pl.BlockSpec((rows, D_MODEL), lambda l, j: (0, 0)),
            pl.BlockSpec((None, D_MODEL, tn), lambda l, j: (l, 0, j)),
            pl.BlockSpec((None, 1, tn), lambda l, j: (l, 0, j)),
        ],
        out_specs=pl.BlockSpec((None, rows, tn), lambda l, j: (l, 0, j)),
        compiler_params=_params(("arbitrary", "arbitrary")),
        name="ada_mod",
    )(c_all, w_ada, b_ada.reshape(DEPTH, 1, 6 * D_MODEL))


def _norm_mod(x, g, sh, sc):
    y = x * lax.rsqrt(jnp.mean(x * x, axis=-1, keepdims=True) + EPS)
    return (y * g) * (1.0 + sc) + sh


def _inproj_kernel(x_ref, g_ref, sh_ref, sc_ref, w_ref, ws_ref, o_ref, os_ref, h_ref):
    @pl.when(pl.program_id(2) == 0)
    def _():
        hb = _norm_mod(x_ref[...], g_ref[...], sh_ref[...], sc_ref[...]).astype(BF16)
        h_ref[...] = hb
        os_ref[...] = jnp.dot(hb, ws_ref[...], preferred_element_type=F32)

    o_ref[...] = jnp.dot(h_ref[...], w_ref[...], preferred_element_type=F32)


def _inproj_call(x, mod3, norm_g, w_main, w_side):
    bsz, length, d = x.shape
    n_main = w_main.shape[1]
    tm = min(1024, length)
    tn = 1024
    return pl.pallas_call(
        _inproj_kernel,
        out_shape=(jax.ShapeDtypeStruct((bsz, length, n_main), F32),
                   jax.ShapeDtypeStruct((bsz, length, SIDE_W), F32)),
        grid=(bsz, length // tm, n_main // tn),
        in_specs=[
            pl.BlockSpec((None, tm, d), lambda b, i, j: (b, i, 0)),
            pl.BlockSpec((1, d), lambda b, i, j: (0, 0)),
            pl.BlockSpec((None, 1, d), lambda b, i, j: (b, 0, 0)),
            pl.BlockSpec((None, 1, d), lambda b, i, j: (b, 0, 1)),
            pl.BlockSpec((d, tn), lambda b, i, j: (0, j)),
            pl.BlockSpec((d, SIDE_W), lambda b, i, j: (0, 0)),
        ],
        out_specs=(pl.BlockSpec((None, tm, tn), lambda b, i, j: (b, i, j)),
                   pl.BlockSpec((None, tm, SIDE_W), lambda b, i, j: (b, i, 0))),
        scratch_shapes=[pltpu.VMEM((tm, d), BF16)],
        compiler_params=_params(("arbitrary", "arbitrary", "arbitrary")),
        name="inproj",
    )(x, norm_g.reshape(1, d), mod3, mod3, w_main, w_side)


def _outproj_kernel(rec_first, att_ref, of_ref, ob_ref, z_ref, ng_ref, w_ref, x_ref, gate_ref,
                    o_ref, y_ref):
    half = att_ref.shape[-1]
    ng = ng_ref[...]
    for h in range(H_GROUP):
        sl = slice(h * HEAD_DIM, (h + 1) * HEAD_DIM)
        o = of_ref[:, sl] + ob_ref[:, sl]
        o = o * lax.rsqrt(jnp.mean(o * o, axis=-1, keepdims=True) + EPS) * ng
        z = z_ref[:, sl]
        y_ref[:, sl] = (o * (z * jax.nn.sigmoid(z))).astype(BF16)
    rec0, att0 = (0, half) if rec_first else (half, 0)
    y = jnp.dot(y_ref[...], w_ref[rec0:rec0 + half, :], preferred_element_type=F32)
    y = y + jnp.dot(att_ref[...], w_ref[att0:att0 + half, :], preferred_element_type=F32)
    o_ref[...] = x_ref[...] + gate_ref[...] * y


def _outproj_call(att, o_fw, o_bw, main, gate_group, norm_g, rec_first, w_out, x, mod3):
    bsz, length, d = x.shape
    half = att.shape[-1]
    tm = min(512, length)
    blk = lambda col: pl.BlockSpec((None, tm, half), lambda b, i: (b, i, col))
    return pl.pallas_call(
        functools.partial(_outproj_kernel, rec_first),
        out_shape=jax.ShapeDtypeStruct(x.shape, F32),
        grid=(bsz, length // tm),
        in_specs=[
            blk(0), blk(0), blk(0), blk(gate_group),
            pl.BlockSpec((1, HEAD_DIM), lambda b, i: (0, 0)),
            pl.BlockSpec((2 * half, d), lambda b, i: (0, 0)),
            pl.BlockSpec((None, tm, d), lambda b, i: (b, i, 0)),
            pl.BlockSpec((None, 1, d), lambda b, i: (b, 0, 2)),
        ],
        out_specs=pl.BlockSpec((None, tm, d), lambda b, i: (b, i, 0)),
        scratch_shapes=[pltpu.VMEM((tm, half), BF16)],
        compiler_params=_params(("arbitrary", "arbitrary")),
        name="outproj",
    )(att, o_fw, o_bw, main, norm_g.reshape(1, HEAD_DIM), w_out, x, mod3)


def _ffn_kernel(final, xp_ref, x_ref, xn_ref, g_ref, sh_ref, sc_ref, gate_ref, wg_ref, wu_ref,
                cw_ref, wd_ref, fn_ref, o_ref, h_ref, gp_ref):
    i = pl.program_id(1)
    j = pl.program_id(2)
    tm = x_ref.shape[0]

    @pl.when(j == 0)
    def _():
        g, sh, sc = g_ref[...], sh_ref[...], sc_ref[...]
        h_ref[HALO:HALO + tm, :] = _norm_mod(x_ref[...], g, sh, sc).astype(BF16)
        hp = _norm_mod(xp_ref[...], g, sh, sc)
        h_ref[0:HALO, :] = jnp.where(i == 0, 0.0, hp).astype(BF16)
        hn = _norm_mod(xn_ref[...], g, sh, sc)
        h_ref[HALO + tm:, :] = jnp.where(i == pl.num_programs(1) - 1, 0.0, hn).astype(BF16)
        o_ref[...] = jnp.zeros_like(o_ref)

    gp_ref[...] = jnp.dot(h_ref[...], wg_ref[...], preferred_element_type=F32)
    cw = cw_ref[...]
    a = gp_ref[pl.ds(HALO - 1, tm), :] * cw[0:1, :]
    a = a + gp_ref[pl.ds(HALO, tm), :] * cw[1:2, :]
    a = a + gp_ref[pl.ds(HALO + 1, tm), :] * cw[2:3, :]
    u = jnp.dot(h_ref[HALO:HALO + tm, :], wu_ref[...], preferred_element_type=F32)
    act = (a * jax.nn.sigmoid(a)) * u
    o_ref[...] += jnp.dot(act.astype(BF16), wd_ref[...], preferred_element_type=F32)

    @pl.when(j == pl.num_programs(2) - 1)
    def _():
        xo = x_ref[...] + gate_ref[...] * o_ref[...]
        if final:
            xo = xo * lax.rsqrt(jnp.mean(xo * xo, axis=-1, keepdims=True) + EPS) * fn_ref[...]
        o_ref[...] = xo


def _ffn_call(x, mod3, norm_g, w_gate, w_up, conv_w, w_down, final_g, final):
    bsz, length, d = x.shape
    tm = min(512, length)
    tf = 512
    nh = tm // HALO
    last_h = length // HALO - 1
    return pl.pallas_call(
        functools.partial(_ffn_kernel, final),
        out_shape=jax.ShapeDtypeStruct(x.shape, F32),
        grid=(bsz, length // tm, D_FF // tf),
        in_specs=[
            pl.BlockSpec((None, HALO, d), lambda b, i, j: (b, jnp.maximum(i * nh - 1, 0), 0)),
            pl.BlockSpec((None, tm, d), lambda b, i, j: (b, i, 0)),
            pl.BlockSpec((None, HALO, d), lambda b, i, j: (b, jnp.minimum((i + 1) * nh, last_h), 0)),
            pl.BlockSpec((1, d), lambda b, i, j: (0, 0)),
            pl.BlockSpec((None, 1, d), lambda b, i, j: (b, 0, 3)),
            pl.BlockSpec((None, 1, d), lambda b, i, j: (b, 0, 4)),
            pl.BlockSpec((None, 1, d), lambda b, i, j: (b, 0, 5)),
            pl.BlockSpec((d, tf), lambda b, i, j: (0, j)),
            pl.BlockSpec((d, tf), lambda b, i, j: (0, j)),
            pl.BlockSpec((FFN_CONV_K, tf), lambda b, i, j: (0, j)),
            pl.BlockSpec((tf, d), lambda b, i, j: (j, 0)),
            pl.BlockSpec((1, d), lambda b, i, j: (0, 0)),
        ],
        out_specs=pl.BlockSpec((None, tm, d), lambda b, i, j: (b, i, 0)),
        scratch_shapes=[pltpu.VMEM((tm + 2 * HALO, d), BF16),
                        pltpu.VMEM((tm + 2 * HALO, tf), F32)],
        compiler_params=_params(("arbitrary", "arbitrary", "arbitrary")),
        name="conv_ffn",
    )(x, x, x, norm_g.reshape(1, d), mod3, mod3, mod3, w_gate, w_up, conv_w, w_down,
      final_g.reshape(1, d))


NA_KEYS = NA_WIN_R * GRID_W
MASKED = -1e30
NA_ROWS_PER_STEP = 8


def _na_bias_table(rpb):
    qc = np.arange(GRID_W)[:, None]
    kc = np.arange(GRID_W)[None, :]
    cs = np.clip(qc - NA_WIN_C // 2, 0, GRID_W - NA_WIN_C)
    valid = (kc >= cs) & (kc < cs + NA_WIN_C)
    dc = np.clip(kc - qc + NA_WIN_C - 1, 0, 2 * NA_WIN_C - 2)
    dr = np.arange(NA_WIN_R)[None, :] - np.arange(NA_WIN_R)[:, None] + NA_WIN_R - 1
    t = rpb.astype(F32)[:, dr][:, :, :, dc]
    t = jnp.where(valid[None, None, None], t * LOG2_E, MASKED)
    return t.transpose(0, 1, 3, 2, 4).reshape(rpb.shape[0], NA_WIN_R, GRID_W, NA_KEYS)


def _na_kernel(q_ref, k_ref, v_ref, bias_ref, o_ref, kb_ref, vb_ref):
    rows = q_ref.shape[0] // GRID_W
    kb_ref[...] = k_ref[...].astype(BF16)
    vb_ref[...] = v_ref[...].astype(BF16)

    def row_group(g, carry):
        todo = []
        for u in range(NA_ROWS_PER_STEP):
            r = g * NA_ROWS_PER_STEP + u
            rs = jnp.clip(r - NA_WIN_R // 2, 0, rows - NA_WIN_R)
            q0 = pl.multiple_of(r * GRID_W, GRID_W)
            k0 = pl.multiple_of(rs * GRID_W, GRID_W)
            q = (q_ref[pl.ds(q0, GRID_W), :] * (HEAD_DIM ** -0.5 * LOG2_E)).astype(BF16)
            s = lax.dot_general(q, kb_ref[pl.ds(k0, NA_KEYS), :], (((1,), (1,)), ((), ())),
                                preferred_element_type=F32)
            todo.append((q0, k0, s, r - rs))
        for u, (q0, k0, s, off) in enumerate(todo):
            s = s + bias_ref[off]
            p = jnp.exp2(s - jnp.max(s, axis=-1, keepdims=True))
            todo[u] = (q0, k0, p.astype(BF16), jnp.sum(p, axis=-1, keepdims=True))
        for u, (q0, k0, p, l) in enumerate(todo):
            todo[u] = (q0, jnp.dot(p, vb_ref[pl.ds(k0, NA_KEYS), :], preferred_element_type=F32), l)
        for q0, o, l in todo:
            o_ref[pl.ds(q0, GRID_W), :] = (o / l).astype(o_ref.dtype)
        return carry

    lax.fori_loop(0, rows // NA_ROWS_PER_STEP, row_group, 0)


def _na_call(main, rpb):
    bsz, length, _ = main.shape
    assert length % (GRID_W * NA_ROWS_PER_STEP) == 0 and length // GRID_W >= NA_WIN_R
    hd = HEAD_DIM
    return pl.pallas_call(
        _na_kernel,
        out_shape=jax.ShapeDtypeStruct((bsz, length, GROUP_W), BF16),
        grid=(bsz, H_GROUP),
        in_specs=[
            pl.BlockSpec((None, length, hd), lambda b, h: (b, 0, h)),
            pl.BlockSpec((None, length, hd), lambda b, h: (b, 0, H_GROUP + h)),
            pl.BlockSpec((None, length, hd), lambda b, h: (b, 0, 2 * H_GROUP + h)),
            pl.BlockSpec((None, NA_WIN_R, GRID_W, NA_KEYS), lambda b, h: (h, 0, 0, 0)),
        ],
        out_specs=pl.BlockSpec((None, length, hd), lambda b, h: (b, 0, h)),
        scratch_shapes=[pltpu.VMEM((length, hd), BF16), pltpu.VMEM((length, hd), BF16)],
        compiler_params=_params(("arbitrary", "arbitrary")),
        name="nbr_attn",
    )(main, main, main, _na_bias_table(rpb))


REC_TILE = 512
REC_HEADS = 8
HIGHEST = lax.Precision.HIGHEST


def _tri(n, upper):
    r = lax.broadcasted_iota(jnp.int32, (n, n), 0)
    c = lax.broadcasted_iota(jnp.int32, (n, n), 1)
    return (c >= r) if upper else (c <= r)


def _gdn_conv_kernel(x_ref, w_ref, o_ref, xp_ref):
    length, cw = x_ref.shape
    col0 = pl.program_id(1) * cw
    pad = 8
    xp_ref[0:pad, :] = jnp.zeros((pad, cw), F32)
    xp_ref[pad + length:, :] = jnp.zeros((pad, cw), F32)
    xp_ref[pad:pad + length, :] = x_ref[...]
    w = w_ref[...]
    y = xp_ref[pl.ds(pad - DN_CONV_K // 2, length), :] * w[0:1, :]
    for t in range(1, DN_CONV_K):
        y = y + xp_ref[pl.ds(pad - DN_CONV_K // 2 + t, length), :] * w[t:t + 1, :]
    y = y * jax.nn.sigmoid(y)
    qscale = jnp.where(col0 < GROUP_W, HEAD_DIM ** -0.5, 1.0)
    for h in range(cw // HEAD_DIM):
        yh = y[:, h * HEAD_DIM:(h + 1) * HEAD_DIM]
        nh = yh * (lax.rsqrt(jnp.sum(yh * yh, axis=-1, keepdims=True) + EPS) * qscale)
        o_ref[:, h * HEAD_DIM:(h + 1) * HEAD_DIM] = jnp.where(col0 < 2 * GROUP_W, nh, yh)


def _gdn_conv_call(main, conv_w):
    bsz, length, _ = main.shape
    cw = HEAD_DIM
    nblk = 3 * GROUP_W // cw
    return pl.pallas_call(
        _gdn_conv_kernel,
        out_shape=jax.ShapeDtypeStruct((bsz, length, 3 * GROUP_W), F32),
        grid=(bsz, nblk),
        in_specs=[pl.BlockSpec((None, length, cw), lambda b, j: (b, 0, nblk + j)),
                  pl.BlockSpec((DN_CONV_K, cw), lambda b, j: (0, j))],
        out_specs=pl.BlockSpec((None, length, cw), lambda b, j: (b, 0, j)),
        scratch_shapes=[pltpu.VMEM((length + 16, cw), F32)],
        compiler_params=_params(("arbitrary", "arbitrary")),
        name="gdn_conv",
    )(main, conv_w)


def _gdn_gates_kernel(s_ref, alog_ref, dt_ref, g_ref, gt_ref):
    tm = s_ref.shape[0]
    s = s_ref[...]
    x = s + dt_ref[...]
    softplus = jnp.maximum(x, 0.0) + jnp.log1p(jnp.exp(-jnp.abs(x)))
    g = -jnp.exp(alog_ref[...]) * softplus
    r = lax.broadcasted_iota(jnp.int32, (tm, tm), 0)
    c = lax.broadcasted_iota(jnp.int32, (tm, tm), 1)
    same = (r // DN_CHUNK) == (c // DN_CHUNK)
    lo = jnp.where(same & (c <= r), 1.0, 0.0)
    up = jnp.where(same & (c >= r), 1.0, 0.0)
    gcf = jnp.dot(lo, g, precision=HIGHEST, preferred_element_type=F32)
    gcb = jnp.dot(up, g, precision=HIGHEST, preferred_element_type=F32)
    lane = lax.broadcasted_iota(jnp.int32, s.shape, 1)
    out = jnp.where(lane < H_GROUP, gcf, jnp.where(lane < 2 * H_GROUP, gcb, jax.nn.sigmoid(s)))
    g_ref[...] = out
    for p in range(tm // 128):
        gt_ref[p] = out[p * 128:(p + 1) * 128, :].T[0:4 * H_GROUP, :]


def _gdn_gates_call(side, a_log, dt_bias):
    bsz, length, _ = side.shape
    tm = min(512, length)
    row = lambda t: jnp.pad(t.reshape(1, 2 * H_GROUP).astype(F32), ((0, 0), (0, SIDE_W - 2 * H_GROUP)))
    return pl.pallas_call(
        _gdn_gates_kernel,
        out_shape=(jax.ShapeDtypeStruct((bsz, length, SIDE_W), F32),
                   jax.ShapeDtypeStruct((bsz, length // 128, 4 * H_GROUP, 128), F32)),
        grid=(bsz, length // tm),
        in_specs=[pl.BlockSpec((None, tm, SIDE_W), lambda b, i: (b, i, 0)),
                  pl.BlockSpec((1, SIDE_W), lambda b, i: (0, 0)),
                  pl.BlockSpec((1, SIDE_W), lambda b, i: (0, 0))],
        out_specs=(pl.BlockSpec((None, tm, SIDE_W), lambda b, i: (b, i, 0)),
                   pl.BlockSpec((None, tm // 128, 4 * H_GROUP, 128), lambda b, i: (b, i, 0, 0))),
        compiler_params=_params(("arbitrary", "arbitrary")),
        name="gdn_gates",
    )(side, row(a_log), row(dt_bias))


def _bdot(a, b):
    return jnp.dot(a.astype(BF16), b.astype(BF16), preferred_element_type=F32)


def _gdn_chunks_local(chunks):
    c = DN_CHUNK
    r = lax.broadcasted_iota(jnp.int32, (c, c), 0)
    cc = lax.broadcasted_iota(jnp.int32, (c, c), 1)
    for d in chunks:
        incl = (cc >= r) if d["rev"] else (cc <= r)
        d["gam"] = jnp.exp(jnp.where(incl, d["gci"] - d["gcj"], MASKED))
        d["kb"] = d["k"] * d["beta"]
    for d in chunks:
        kq = jnp.concatenate([d["kb"], d["q"]], axis=0).astype(BF16)
        d["prod"] = lax.dot_general(kq, d["k"].astype(BF16), (((1,), (1,)), ((), ())),
                                    preferred_element_type=F32)
    for d in chunks:
        strict = (cc > r) if d["rev"] else (cc < r)
        d["aqk"] = d["prod"][c:] * d["gam"]
        d["p"] = -jnp.where(strict, d["prod"][0:c] * d["gam"], 0.0)
        d["t"] = jnp.where(r == cc, 1.0, 0.0) + d["p"]
    for _ in range(int(math.log2(c)) - 1):
        for d in chunks:
            d["p"] = _bdot(d["p"], d["p"])
        for d in chunks:
            d["t"] = d["t"] + _bdot(d["t"], d["p"])
    for d in chunks:
        egc = jnp.exp(d["gci"])
        rhs = jnp.concatenate([d["v"] * d["beta"], d["kb"] * egc], axis=1)
        uw = _bdot(d["t"], rhs)
        d["u"], d["w"] = uw[:, 0:HEAD_DIM], uw[:, HEAD_DIM:]
        glast = d["gci"][0:1, :] if d["rev"] else d["gci"][c - 1:c, :]
        d["qd"] = d["q"] * egc
        d["kd"] = d["k"] * jnp.exp(glast - d["gci"])
        d["dlast"] = jnp.exp(glast)


def _gdn_chunks_scan(chunks):
    c = DN_CHUNK
    for d in chunks:
        d["s"] = d["s_ref"][...]
        d["ws"] = _bdot(jnp.concatenate([d["w"], d["qd"]], axis=0), d["s"])
    for d in chunks:
        d["vn"] = (d["u"] - d["ws"][0:c]).astype(BF16)
        d["o"] = d["ws"][c:] + _bdot(d["aqk"], d["vn"])
    for d in chunks:
        d["s_ref"][...] = d["s"] * d["dlast"] + lax.dot_general(
            d["kd"].astype(BF16), d["vn"], (((0,), (0,)), ((), ())), preferred_element_type=F32)


def _gdn_kernel(qf_ref, kf_ref, vf_ref, gf_ref, gtf_ref, qb_ref, kb_ref, vb_ref, gb_ref, gtb_ref,
                of_ref, ob_ref, s_ref):
    hg = pl.program_id(1)

    @pl.when(pl.program_id(2) == 0)
    def _():
        s_ref[...] = jnp.zeros_like(s_ref)

    npair = qf_ref.shape[0] // 128
    c = DN_CHUNK

    def pair(p, carry):
        chunks = []
        for rev in (False, True):
            q_ref, k_ref, v_ref, g_ref, gt_ref, o_ref = (
                (qb_ref, kb_ref, vb_ref, gb_ref, gtb_ref, ob_ref) if rev
                else (qf_ref, kf_ref, vf_ref, gf_ref, gtf_ref, of_ref))
            pp = (npair - 1 - p) if rev else p
            base = pl.multiple_of(pp * 128, 128)
            gt = gt_ref[pp]
            sub = lax.broadcasted_iota(jnp.int32, gt.shape, 0)
            for step, half in enumerate((1, 0) if rev else (0, 1)):
                rows = pl.ds(base + half * c, c)
                g = g_ref[rows, :]
                lane = lax.broadcasted_iota(jnp.int32, g.shape, 1)
                for hl in range(REC_HEADS):
                    cols = slice(hl * HEAD_DIM, (hl + 1) * HEAD_DIM)
                    gcol = hg * REC_HEADS + hl + (H_GROUP if rev else 0)
                    chunks.append(dict(
                        rev=rev, step=step, rows=rows, cols=cols, o_ref=o_ref,
                        s_ref=s_ref.at[int(rev), hl],
                        q=q_ref[rows, cols], k=k_ref[rows, cols], v=v_ref[rows, cols],
                        gci=jnp.sum(jnp.where(lane == gcol, g, 0.0), axis=-1, keepdims=True),
                        beta=jnp.sum(jnp.where(lane == gcol + 2 * H_GROUP, g, 0.0), axis=-1,
                                     keepdims=True),
                        gcj=jnp.sum(jnp.where(sub == gcol, gt, 0.0), axis=0,
                                    keepdims=True)[:, half * c:(half + 1) * c]))
        _gdn_chunks_local(chunks)
        for step in (0, 1):
            now = [d for d in chunks if d["step"] == step]
            _gdn_chunks_scan(now)
            for d in now:
                d["o_ref"][d["rows"], d["cols"]] = d["o"]
        return carry

    lax.fori_loop(0, npair, pair, 0)


def _gdn_call(main, side, conv_w, a_log, dt_bias):
    bsz, length, _ = main.shape
    tl = min(REC_TILE, length)
    nt = length // tl
    w = REC_HEADS * HEAD_DIM
    ng = H_GROUP // REC_HEADS
    qkv = _gdn_conv_call(main, conv_w)
    gates, gates_t = _gdn_gates_call(side, a_log, dt_bias)

    def specs(tile):
        return [pl.BlockSpec((None, tl, w), lambda b, g, i: (b, tile(i), g)),
                pl.BlockSpec((None, tl, w), lambda b, g, i: (b, tile(i), ng + g)),
                pl.BlockSpec((None, tl, w), lambda b, g, i: (b, tile(i), 2 * ng + g)),
                pl.BlockSpec((None, tl, SIDE_W), lambda b, g, i: (b, tile(i), 0)),
                pl.BlockSpec((None, tl // 128, 4 * H_GROUP, 128), lambda b, g, i: (b, tile(i), 0, 0))]

    fw = lambda i: i
    bw = lambda i: nt - 1 - i
    return pl.pallas_call(
        _gdn_kernel,
        out_shape=(jax.ShapeDtypeStruct((bsz, length, GROUP_W), F32),) * 2,
        grid=(bsz, ng, nt),
        in_specs=specs(fw) + specs(bw),
        out_specs=(pl.BlockSpec((None, tl, w), lambda b, g, i: (b, fw(i), g)),
                   pl.BlockSpec((None, tl, w), lambda b, g, i: (b, bw(i), g))),
        scratch_shapes=[pltpu.VMEM((2, REC_HEADS, HEAD_DIM, HEAD_DIM), F32)],
        compiler_params=_params(("arbitrary", "arbitrary", "arbitrary")),
        name="gdn_scan",
    )(qkv, qkv, qkv, gates, gates_t, qkv, qkv, qkv, gates, gates_t)


HG_TILE = 64
HG_SUB = 8
HG_NSUB = HG_TILE // HG_SUB


def _tri_cumsum(x, rev):
    n = x.shape[0]
    tri = jnp.where(_tri(n, rev), 1.0, 0.0).astype(BF16)
    hi = x.astype(BF16)
    r1 = x - hi.astype(F32)
    mid = r1.astype(BF16)
    lo = (r1 - mid.astype(F32)).astype(BF16)
    parts = jnp.dot(tri, jnp.concatenate([hi, mid, lo], axis=1), preferred_element_type=F32)
    w = x.shape[1]
    return (parts[:, 0:w] + parts[:, w:2 * w]) + parts[:, 2 * w:]


def _hg_chunks(chains):
    n, sub, nsub = HG_TILE, HG_SUB, HG_NSUB
    row = lax.broadcasted_iota(jnp.int32, (n, HEAD_DIM), 0)
    pos = row % sub
    tn_dims = (((0,), (0,)), ((), ()))
    for c in chains:
        z, loglb, log1mlb = c["z"], c["loglb"], c["log1mlb"]
        c["q"] = c["q_raw"] * jax.nn.sigmoid(c["q_raw"])
        log_sig = jnp.minimum(z, 0.0) - jnp.log1p(jnp.exp(-jnp.abs(z)))
        bb = log1mlb + log_sig
        lf = jnp.maximum(loglb, bb) + jnp.log1p(jnp.exp(-jnp.abs(loglb - bb)))
        b = _tri_cumsum(lf, c["rev"])
        c["b2"] = b * LOG2_E
        c["c2"] = (b - (bb - z)) * LOG2_E
        c["inp16"] = c["inp"].astype(BF16)

    for c in chains:
        b2, c2, rev = c["b2"], c["c2"], c["rev"]
        c["order"] = list(range(nsub - 1, -1, -1)) if rev else list(range(nsub))
        c["ends"], c["contrib"] = [], []
        for j in c["order"]:
            lo = j * sub
            bend = b2[lo:lo + 1, :] if rev else b2[lo + sub - 1:lo + sub, :]
            c["ends"].append(bend)
            if len(c["contrib"]) < nsub - 1:
                ks = jnp.exp2(bend - c2[lo:lo + sub, :])
                c["contrib"].append(lax.dot_general(c["inp16"][lo:lo + sub, :], ks.astype(BF16),
                                                    tn_dims, preferred_element_type=F32))

    for c in chains:
        b2, c2, q, inp, rev = c["b2"], c["c2"], c["q"], c["inp"], c["rev"]
        o = jnp.sum(q * jnp.exp2(b2 - c2), axis=-1, keepdims=True) * inp
        for d in range(1, sub):
            shift = (sub - d) % sub if rev else d
            valid = (pos <= sub - 1 - d) if rev else (pos >= d)
            sh = lambda x: pltpu.roll(x.reshape(nsub, sub, HEAD_DIM), shift, axis=1).reshape(
                n, HEAD_DIM)
            e = jnp.where(valid, jnp.exp2(b2 - sh(c2)), 0.0)
            o = o + jnp.sum(q * e, axis=-1, keepdims=True) * sh(inp)
        c["o"] = o

    for c in chains:
        b2, q, rev = c["b2"], c["q"], c["rev"]
        c["s"] = c["s_ref"][...]
        qext = [q * jnp.exp2(b2)]
        for m in range(nsub - 1):
            j = c["order"][m]
            lo, hi = (0, j * sub) if rev else ((j + 1) * sub, n)
            part = q[lo:hi] * jnp.exp2(b2[lo:hi] - c["ends"][m])
            zeros = jnp.zeros((n - (hi - lo), HEAD_DIM), F32)
            qext.append(jnp.concatenate([part, zeros] if rev else [zeros, part], axis=0))
        ncat = jnp.concatenate([c["s"]] + c["contrib"], axis=1).astype(BF16)
        c["o"] = c["o"] + lax.dot_general(jnp.concatenate(qext, axis=1).astype(BF16), ncat,
                                          (((1,), (1,)), ((), ())), preferred_element_type=F32)
    for c in chains:
        btot = c["ends"][-1]
        kd = jnp.exp2(btot - c["c2"]).astype(BF16)
        c["s_ref"][...] = c["s"] * jnp.exp2(btot) + lax.dot_general(
            c["inp16"], kd, tn_dims, preferred_element_type=F32)


def _hgrn2_kernel(qf_ref, ff_ref, if_ref, qb_ref, fb_ref, ib_ref, lbf_ref, lbb_ref,
                  of_ref, ob_ref, s_ref):
    @pl.when(pl.program_id(2) == 0)
    def _():
        s_ref[...] = jnp.zeros_like(s_ref)

    nstep = qf_ref.shape[0] // HG_TILE

    def step(t, carry):
        chains = []
        for rev in (False, True):
            q_ref, f_ref, i_ref, lb_ref, o_ref = ((qb_ref, fb_ref, ib_ref, lbb_ref, ob_ref) if rev
                                                  else (qf_ref, ff_ref, if_ref, lbf_ref, of_ref))
            tt = (nstep - 1 - t) if rev else t
            rows = pl.ds(pl.multiple_of(tt * HG_TILE, HG_TILE), HG_TILE)
            for hl in range(REC_HEADS):
                cols = slice(hl * HEAD_DIM, (hl + 1) * HEAD_DIM)
                chains.append(dict(rev=rev, rows=rows, cols=cols, o_ref=o_ref,
                                   s_ref=s_ref.at[int(rev), hl], q_raw=q_ref[rows, cols],
                                   z=f_ref[rows, cols], inp=i_ref[rows, cols],
                                   loglb=lb_ref[0:1, cols], log1mlb=lb_ref[1:2, cols]))
        _hg_chunks(chains)
        for c in chains:
            c["o_ref"][c["rows"], c["cols"]] = c["o"]
        return carry

    lax.fori_loop(0, nstep, step, 0)


def _hgrn2_call(main, lb_fw, lb_bw):
    bsz, length, _ = main.shape
    tl = min(REC_TILE, length)
    nt = length // tl
    w = REC_HEADS * HEAD_DIM
    ng = H_GROUP // REC_HEADS
    fw = lambda i: i
    bw = lambda i: nt - 1 - i
    col = lambda group, tile: pl.BlockSpec((None, tl, w), lambda b, g, i: (b, tile(i), group * ng + g))
    lbrow = pl.BlockSpec((2, w), lambda b, g, i: (0, g))
    logs = lambda lb: jnp.stack([jnp.log(lb), jnp.log1p(-lb)])
    return pl.pallas_call(
        _hgrn2_kernel,
        out_shape=(jax.ShapeDtypeStruct((bsz, length, GROUP_W), F32),) * 2,
        grid=(bsz, ng, nt),
        in_specs=[col(0, fw), col(1, fw), col(3, fw), col(0, bw), col(2, bw), col(3, bw), lbrow, lbrow],
        out_specs=(pl.BlockSpec((None, tl, w), lambda b, g, i: (b, fw(i), g)),
                   pl.BlockSpec((None, tl, w), lambda b, g, i: (b, bw(i), g))),
        scratch_shapes=[pltpu.VMEM((2, REC_HEADS, HEAD_DIM, HEAD_DIM), F32)],
        compiler_params=_params(("arbitrary", "arbitrary", "arbitrary")),
        name="hgrn2_scan",
    )(main, main, main, main, main, main, logs(lb_fw), logs(lb_bw))


MLA_QK = 2 * MLA_NOPE
MLA_KEY_BLOCK = 512
ROPE_SWAP = np.concatenate([np.arange(MLA_ROPE // 2, MLA_ROPE), np.arange(MLA_ROPE // 2)])


def _rope_table(length):
    half = MLA_ROPE // 2
    inv = ROPE_BASE ** (-jnp.arange(half, dtype=F32) / half)
    ang = jnp.arange(length, dtype=F32)[:, None] * inv[None, :]
    cos, sin = jnp.cos(ang), jnp.sin(ang)
    return jnp.concatenate([cos, cos, -sin, sin], axis=-1)


def _mla_q_weights(w_uq):
    w = w_uq.reshape(MLA_Q_LORA, H_GROUP, MLA_NOPE + MLA_ROPE)
    rope = w[:, :, MLA_NOPE:]
    w = jnp.concatenate([w[:, :, :MLA_NOPE], rope, rope[:, :, ROPE_SWAP]], axis=-1)
    return w.reshape(MLA_Q_LORA, H_GROUP * MLA_QK).astype(BF16)


def _plain_rmsnorm(x, g):
    return x * lax.rsqrt(jnp.mean(x * x, axis=-1, keepdims=True) + EPS) * g


def _mla_proj_kernel(cq_ref, ckv_ref, side_ref, tab_ref, qn_ref, kvn_ref, wq_ref, wkv_ref,
                     q_out, k_out, v_out):
    scale = (MLA_NOPE + MLA_ROPE) ** -0.5 * LOG2_E
    cqn = _plain_rmsnorm(cq_ref[...], qn_ref[...]).astype(BF16)
    ckvn = _plain_rmsnorm(ckv_ref[...], kvn_ref[...]).astype(BF16)
    q = jnp.dot(cqn, wq_ref[...], preferred_element_type=F32)
    kv = jnp.dot(ckvn, wkv_ref[...], preferred_element_type=F32)
    tab = tab_ref[...]
    kr = (side_ref[...] * tab).astype(BF16)
    for h in range(H_GROUP):
        c0 = h * MLA_QK
        qr = q[:, c0 + MLA_NOPE:c0 + MLA_QK] * tab
        qr = qr + pltpu.roll(qr, MLA_ROPE, axis=1)
        q_out[h, :, 0:MLA_NOPE] = (q[:, c0:c0 + MLA_NOPE] * scale).astype(BF16)
        q_out[h, :, MLA_NOPE:MLA_QK] = (qr * scale).astype(BF16)
        k_out[h, :, 0:MLA_NOPE] = kv[:, c0:c0 + MLA_NOPE].astype(BF16)
        k_out[h, :, MLA_NOPE:MLA_QK] = kr
        v_out[h] = kv[:, c0 + MLA_NOPE:c0 + MLA_QK].astype(BF16)


def _mla_attn_kernel(q_ref, k_ref, v_ref, o_ref):
    length = k_ref.shape[0]
    tk = MLA_KEY_BLOCK
    nk = length // tk
    q = q_ref[...]

    def scores(kb):
        return lax.dot_general(q, k_ref[kb * tk:(kb + 1) * tk, :], (((1,), (1,)), ((), ())),
                               preferred_element_type=F32)

    s_next = scores(0)
    m = l = acc = None
    for kb in range(nk):
        s = s_next
        if kb + 1 < nk:
            s_next = scores(kb + 1)
        m_blk = jnp.max(s, axis=-1, keepdims=True)
        if kb == 0:
            m = m_blk
            p = jnp.exp2(s - m)
            l = jnp.sum(p, axis=-1, keepdims=True)
            acc = jnp.dot(p.astype(BF16), v_ref[0:tk, :], preferred_element_type=F32)
        else:
            m_new = jnp.maximum(m, m_blk)
            alpha = jnp.exp2(m - m_new)
            p = jnp.exp2(s - m_new)
            l = alpha * l + jnp.sum(p, axis=-1, keepdims=True)
            acc = alpha * acc + jnp.dot(p.astype(BF16), v_ref[kb * tk:(kb + 1) * tk, :],
                                        preferred_element_type=F32)
            m = m_new
    o_ref[...] = (acc / l).astype(o_ref.dtype)


def _mla_call(main, side, q_norm, w_uq, kv_norm, w_ukv):
    bsz, length, _ = main.shape
    tm = min(512, length)
    lora = MLA_Q_LORA
    q_all, k_all, v_all = pl.pallas_call(
        _mla_proj_kernel,
        out_shape=(jax.ShapeDtypeStruct((bsz, H_GROUP, length, MLA_QK), BF16),
                   jax.ShapeDtypeStruct((bsz, H_GROUP, length, MLA_QK), BF16),
                   jax.ShapeDtypeStruct((bsz, H_GROUP, length, MLA_V), BF16)),
        grid=(bsz, length // tm),
        in_specs=[
            pl.BlockSpec((None, tm, lora), lambda b, i: (b, i, 10)),
            pl.BlockSpec((None, tm, lora), lambda b, i: (b, i, 11)),
            pl.BlockSpec((None, tm, SIDE_W), lambda b, i: (b, i, 0)),
            pl.BlockSpec((tm, SIDE_W), lambda b, i: (i, 0)),
            pl.BlockSpec((1, lora), lambda b, i: (0, 0)),
            pl.BlockSpec((1, lora), lambda b, i: (0, 0)),
            pl.BlockSpec((lora, H_GROUP * MLA_QK), lambda b, i: (0, 0)),
            pl.BlockSpec((lora, H_GROUP * MLA_QK), lambda b, i: (0, 0)),
        ],
        out_specs=(pl.BlockSpec((None, H_GROUP, tm, MLA_QK), lambda b, i: (b, 0, i, 0)),
                   pl.BlockSpec((None, H_GROUP, tm, MLA_QK), lambda b, i: (b, 0, i, 0)),
                   pl.BlockSpec((None, H_GROUP, tm, MLA_V), lambda b, i: (b, 0, i, 0))),
        compiler_params=_params(("arbitrary", "arbitrary")),
        name="mla_proj",
    )(main, main, side, _rope_table(length), q_norm.reshape(1, lora), kv_norm.reshape(1, lora),
      _mla_q_weights(w_uq), w_ukv.astype(BF16))

    tq = 512
    return pl.pallas_call(
        _mla_attn_kernel,
        out_shape=jax.ShapeDtypeStruct((bsz, length, H_GROUP * MLA_V), BF16),
        grid=(bsz, H_GROUP, length // tq),
        in_specs=[
            pl.BlockSpec((None, None, tq, MLA_QK), lambda b, h, i: (b, h, i, 0)),
            pl.BlockSpec((None, None, length, MLA_QK), lambda b, h, i: (b, h, 0, 0)),
            pl.BlockSpec((None, None, length, MLA_V), lambda b, h, i: (b, h, 0, 0)),
        ],
        out_specs=pl.BlockSpec((None, tq, MLA_V), lambda b, h, i: (b, i, h)),
        compiler_params=_params(("arbitrary", "arbitrary", "arbitrary")),
        name="mla_attn",
    )(q_all, k_all, v_all)


def _pad_side(w):
    return jnp.pad(w, ((0, 0), (0, SIDE_W - w.shape[1]))).astype(BF16)


def _trunk(x, mods, lb, p):
    for l in range(DEPTH):
        mod3 = mods[l]
        if l % 2 == 0:
            e = l // 2
            main, side = _inproj_call(x, mod3, p["norm_mix"][l], p["w_in_ab_main"][e],
                                      p["w_in_ab_side"][e])
            att = _na_call(main, p["na_rpb"][e])
            o_fw, o_bw = _gdn_call(main, side, p["dn_conv"][e], p["dn_a_log"][e], p["dn_dt_bias"][e])
            x = _outproj_call(att, o_fw, o_bw, main, 6, p["dn_norm"][e], False, p["w_out_ab"][e],
                              x, mod3)
        else:
            o = l // 2
            main, side = _inproj_call(x, mod3, p["norm_mix"][l], p["w_in_cd_main"][o],
                                      p["w_in_cd_side"][o])
            o_fw, o_bw = _hgrn2_call(main, lb[0, l], lb[1, l])
            att = _mla_call(main, side, p["mla_q_norm"][o], p["mla_w_uq"][o], p["mla_kv_norm"][o],
                            p["mla_w_ukv"][o])
            x = _outproj_call(att, o_fw, o_bw, main, 4, p["hg_norm"][o], True, p["w_out_cd"][o],
                              x, mod3)
        x = _ffn_call(x, mod3, p["norm_ffn"][l], p["ffn_w_gate"][l], p["ffn_w_up"][l],
                      p["ffn_conv"][l], p["ffn_w_down"][l], p["final_norm"], l == DEPTH - 1)
    return x


def kernel(x_prompt, x_sample, c_prompt, c_sample, norm_mix, norm_ffn, w_ada, b_ada, w_in_ab, w_out_ab, na_rpb, dn_conv, dn_a_log, dn_dt_bias, dn_norm, w_in_cd, w_out_cd, hg_lower_bounds, hg_norm, mla_q_norm, mla_w_uq, mla_kv_norm, mla_w_ukv, ffn_w_gate, ffn_w_up, ffn_conv, ffn_w_down, final_norm):
    n_p, n_s = c_prompt.shape[0], c_sample.shape[0]
    rows = -(-(n_p + n_s) // 8) * 8
    c_all = jnp.concatenate([c_prompt, c_sample, jnp.zeros((rows - n_p - n_s, D_MODEL), F32)], axis=0)
    mod_all = _ada_call(c_all, w_ada, b_ada)
    mods_p = [mod_all[l, :n_p].reshape(n_p, 1, 6 * D_MODEL) for l in range(DEPTH)]
    mods_s = [mod_all[l, n_p:n_p + n_s].reshape(n_s, 1, 6 * D_MODEL) for l in range(DEPTH)]

    lb = jnp.cumsum(jax.nn.softmax(hg_lower_bounds.astype(F32), axis=1), axis=1)
    lb = lb - lb[:, :1]

    p = dict(
        norm_mix=norm_mix, norm_ffn=norm_ffn, na_rpb=na_rpb, dn_conv=dn_conv, dn_a_log=dn_a_log,
        dn_dt_bias=dn_dt_bias, dn_norm=dn_norm, hg_norm=hg_norm, mla_q_norm=mla_q_norm,
        mla_w_uq=mla_w_uq, mla_kv_norm=mla_kv_norm, mla_w_ukv=mla_w_ukv, ffn_conv=ffn_conv,
        final_norm=final_norm,
        w_in_ab_main=w_in_ab[:, :, :AB_MAIN].astype(BF16),
        w_in_ab_side=jax.vmap(_pad_side)(w_in_ab[:, :, AB_MAIN:]),
        w_in_cd_main=w_in_cd[:, :, :CD_MAIN].astype(BF16),
        w_in_cd_side=jnp.concatenate([w_in_cd[:, :, CD_MAIN:],
                                      w_in_cd[:, :, CD_MAIN:][:, :, ROPE_SWAP]], axis=-1).astype(BF16),
        w_out_ab=w_out_ab.astype(BF16), w_out_cd=w_out_cd.astype(BF16),
        ffn_w_gate=ffn_w_gate.astype(BF16), ffn_w_up=ffn_w_up.astype(BF16),
        ffn_w_down=ffn_w_down.astype(BF16),
    )
    y_prompt = _trunk(x_prompt, mods_p, lb, p)
    y_sample = _trunk(x_sample, mods_s, lb, p)
    return (y_prompt, y_sample)
```

```python
import functools
import math

import numpy as np
import jax
import jax.numpy as jnp
from jax import lax
from jax.experimental import pallas as pl
from jax.experimental.pallas import tpu as pltpu

D_MODEL = 2048
DEPTH = 4
HEAD_DIM = 128
H_GROUP = 8
GROUP_W = H_GROUP * HEAD_DIM
GRID_W = 64
NA_WIN_R = 8
NA_WIN_C = 16
NA_QCB = NA_WIN_C
NA_BAND = 2 * NA_WIN_C
DN_CONV_K = 5
DN_CHUNK = 64
HG_CHUNK = 32
MLA_Q_LORA = 512
MLA_KV_LORA = 512
MLA_NOPE = 128
MLA_ROPE = 64
MLA_V = 128
MLA_BLOCK_Q = 128
ROPE_BASE = 10000.0
D_FF = 5632
FFN_CONV_K = 3
EPS = 1e-6
AB_SPLIT = [GROUP_W, GROUP_W, GROUP_W, 3 * GROUP_W, GROUP_W, H_GROUP, H_GROUP, H_GROUP, H_GROUP]
CD_SPLIT = [GROUP_W, GROUP_W, GROUP_W, GROUP_W, GROUP_W, MLA_Q_LORA, MLA_KV_LORA, MLA_ROPE]
AB_MAIN = 7 * GROUP_W
CD_MAIN = 6 * GROUP_W
SIDE_W = 128

F32 = jnp.float32
BF16 = jnp.bfloat16
LOG2_E = 1.4426950408889634

VMEM_LIMIT_BYTES = 56 * 1024 * 1024
HALO = 16

def _params(sem):
    return pltpu.CompilerParams(dimension_semantics=sem, vmem_limit_bytes=VMEM_LIMIT_BYTES)


def _ada_kernel(c_ref, w_ref, b_ref, o_ref):
    c = c_ref[...]
    cond = c * jax.nn.sigmoid(c)
    o_ref[...] = jnp.dot(cond, w_ref[...], precision=lax.Precision.HIGHEST,
                         preferred_element_type=F32) + b_ref[...]


def _ada_call(c_all, w_ada, b_ada):
    rows = c_all.shape[0]
    tn = 1024
    return pl.pallas_call(
        _ada_kernel,
        out_shape=jax.ShapeDtypeStruct((DEPTH, rows, 6 * D_MODEL), F32),
        grid=(DEPTH, 6 * D_MODEL // tn),
        in_specs=[
            pl.BlockSpec((rows, D_MODEL), lambda l, j: (0, 0)),
            pl.BlockSpec((None, D_MODEL, tn), lambda l, j: (l, 0, j)),
            pl.BlockSpec((None, 1, tn), lambda l, j: (l, 0, j)),
        ],
        out_specs=pl.BlockSpec((None, rows, tn), lambda l, j: (l, 0, j)),
        compiler_params=_params(("arbitrary", "arbitrary")),
        name="ada_mod",
    )(c_all, w_ada, b_ada.reshape(DEPTH, 1, 6 * D_MODEL))


def _norm_mod(x, g, sh, sc):
    y = x * lax.rsqrt(jnp.mean(x * x, axis=-1, keepdims=True) + EPS)
    return (y * g) * (1.0 + sc) + sh


def _inproj_kernel(x_ref, g_ref, sh_ref, sc_ref, w_ref, ws_ref, o_ref, os_ref, h_ref):
    @pl.when(pl.program_id(2) == 0)
    def _():
        hb = _norm_mod(x_ref[...], g_ref[...], sh_ref[...], sc_ref[...]).astype(BF16)
        h_ref[...] = hb
        os_ref[...] = jnp.dot(hb, ws_ref[...], preferred_element_type=F32)

    o_ref[...] = jnp.dot(h_ref[...], w_ref[...], preferred_element_type=F32)


def _inproj_call(x, mod3, norm_g, w_main, w_side):
    bsz, length, d = x.shape
    n_main = w_main.shape[1]
    tm = min(1024, length)
    tn = 1024
    return pl.pallas_call(
        _inproj_kernel,
        out_shape=(jax.ShapeDtypeStruct((bsz, length, n_main), F32),
                   jax.ShapeDtypeStruct((bsz, length, SIDE_W), F32)),
        grid=(bsz, length // tm, n_main // tn),
        in_specs=[
            pl.BlockSpec((None, tm, d), lambda b, i, j: (b, i, 0)),
            pl.BlockSpec((1, d), lambda b, i, j: (0, 0)),
            pl.BlockSpec((None, 1, d), lambda b, i, j: (b, 0, 0)),
            pl.BlockSpec((None, 1, d), lambda b, i, j: (b, 0, 1)),
            pl.BlockSpec((d, tn), lambda b, i, j: (0, j)),
            pl.BlockSpec((d, SIDE_W), lambda b, i, j: (0, 0)),
        ],
        out_specs=(pl.BlockSpec((None, tm, tn), lambda b, i, j: (b, i, j)),
                   pl.BlockSpec((None, tm, SIDE_W), lambda b, i, j: (b, i, 0))),
        scratch_shapes=[pltpu.VMEM((tm, d), BF16)],
        compiler_params=_params(("arbitrary", "arbitrary", "arbitrary")),
        name="inproj",
    )(x, norm_g.reshape(1, d), mod3, mod3, w_main, w_side)


def _outproj_kernel(rec_first, att_ref, of_ref, ob_ref, z_ref, ng_ref, w_ref, x_ref, gate_ref,
                    o_ref, y_ref):
    half = att_ref.shape[-1]
    ng = ng_ref[...]
    for h in range(H_GROUP):
        sl = slice(h * HEAD_DIM, (h + 1) * HEAD_DIM)
        o = of_ref[:, sl] + ob_ref[:, sl]
        o = o * lax.rsqrt(jnp.mean(o * o, axis=-1, keepdims=True) + EPS) * ng
        z = z_ref[:, sl]
        y_ref[:, sl] = (o * (z * jax.nn.sigmoid(z))).astype(BF16)
    rec0, att0 = (0, half) if rec_first else (half, 0)
    y = jnp.dot(y_ref[...], w_ref[rec0:rec0 + half, :], preferred_element_type=F32)
    y = y + jnp.dot(att_ref[...], w_ref[att0:att0 + half, :], preferred_element_type=F32)
    o_ref[...] = x_ref[...] + gate_ref[...] * y


def _outproj_call(att, o_fw, o_bw, main, gate_group, norm_g, rec_first, w_out, x, mod3):
    bsz, length, d = x.shape
    half = att.shape[-1]
    tm = min(512, length)
    blk = lambda col: pl.BlockSpec((None, tm, half), lambda b, i: (b, i, col))
    return pl.pallas_call(
        functools.partial(_outproj_kernel, rec_first),
        out_shape=jax.ShapeDtypeStruct(x.shape, F32),
        grid=(bsz, length // tm),
        in_specs=[
            blk(0), blk(0), blk(0), blk(gate_group),
            pl.BlockSpec((1, HEAD_DIM), lambda b, i: (0, 0)),
            pl.BlockSpec((2 * half, d), lambda b, i: (0, 0)),
            pl.BlockSpec((None, tm, d), lambda b, i: (b, i, 0)),
            pl.BlockSpec((None, 1, d), lambda b, i: (b, 0, 2)),
        ],
        out_specs=pl.BlockSpec((None, tm, d), lambda b, i: (b, i, 0)),
        scratch_shapes=[pltpu.VMEM((tm, half), BF16)],
        compiler_params=_params(("arbitrary", "arbitrary")),
        name="outproj",
    )(att, o_fw, o_bw, main, norm_g.reshape(1, HEAD_DIM), w_out, x, mod3)


def _ffn_up_kernel(xp_ref, x_ref, xn_ref, g_ref, sh_ref, sc_ref, wg_ref, wu_ref, cw_ref,
                   act_ref, h_ref, gp_ref):
    i = pl.program_id(1)
    tm = x_ref.shape[0]

    @pl.when(pl.program_id(2) == 0)
    def _():
        g, sh, sc = g_ref[...], sh_ref[...], sc_ref[...]
        h_ref[HALO:HALO + tm, :] = _norm_mod(x_ref[...], g, sh, sc).astype(BF16)
        hp = _norm_mod(xp_ref[...], g, sh, sc)
        h_ref[0:HALO, :] = jnp.where(i == 0, 0.0, hp).astype(BF16)
        hn = _norm_mod(xn_ref[...], g, sh, sc)
        h_ref[HALO + tm:, :] = jnp.where(i == pl.num_programs(1) - 1, 0.0, hn).astype(BF16)

    gp_ref[...] = jnp.dot(h_ref[...], wg_ref[...], preferred_element_type=F32)
    u = jnp.dot(h_ref[HALO:HALO + tm, :], wu_ref[...], preferred_element_type=F32)
    cw = cw_ref[...]
    a = gp_ref[pl.ds(HALO - 1, tm), :] * cw[0:1, :]
    a = a + gp_ref[pl.ds(HALO, tm), :] * cw[1:2, :]
    a = a + gp_ref[pl.ds(HALO + 1, tm), :] * cw[2:3, :]
    act_ref[...] = ((a * jax.nn.sigmoid(a)) * u).astype(BF16)


def _ffn_down_kernel(act_ref, wd_ref, x_ref, gate_ref, o_ref):
    o_ref[...] = x_ref[...] + gate_ref[...] * jnp.dot(act_ref[...], wd_ref[...],
                                                     preferred_element_type=F32)


def _ffn_split_call(x, mod3, norm_g, w_gate, w_up, conv_w, w_down):
    bsz, length, d = x.shape
    tm = min(1024, length)
    tf = 512
    nh = tm // HALO
    last_h = length // HALO - 1
    act = pl.pallas_call(
        _ffn_up_kernel,
        out_shape=jax.ShapeDtypeStruct((bsz, length, D_FF), BF16),
        grid=(bsz, length // tm, D_FF // tf),
        in_specs=[
            pl.BlockSpec((None, HALO, d), lambda b, i, j: (b, jnp.maximum(i * nh - 1, 0), 0)),
            pl.BlockSpec((None, tm, d), lambda b, i, j: (b, i, 0)),
            pl.BlockSpec((None, HALO, d), lambda b, i, j: (b, jnp.minimum((i + 1) * nh, last_h), 0)),
            pl.BlockSpec((1, d), lambda b, i, j: (0, 0)),
            pl.BlockSpec((None, 1, d), lambda b, i, j: (b, 0, 3)),
            pl.BlockSpec((None, 1, d), lambda b, i, j: (b, 0, 4)),
            pl.BlockSpec((d, tf), lambda b, i, j: (0, j)),
            pl.BlockSpec((d, tf), lambda b, i, j: (0, j)),
            pl.BlockSpec((FFN_CONV_K, tf), lambda b, i, j: (0, j)),
        ],
        out_specs=pl.BlockSpec((None, tm, tf), lambda b, i, j: (b, i, j)),
        scratch_shapes=[pltpu.VMEM((tm + 2 * HALO, d), BF16),
                        pltpu.VMEM((tm + 2 * HALO, tf), F32)],
        compiler_params=_params(("arbitrary", "arbitrary", "arbitrary")),
        name="ffn_up",
    )(x, x, x, norm_g.reshape(1, d), mod3, mod3, w_gate, w_up, conv_w)

    tn = 512
    gate_blk = 5 * (d // tn)
    return pl.pallas_call(
        _ffn_down_kernel,
        out_shape=jax.ShapeDtypeStruct(x.shape, F32),
        grid=(bsz, length // tm, d // tn),
        in_specs=[
            pl.BlockSpec((None, tm, D_FF), lambda b, i, n: (b, i, 0)),
            pl.BlockSpec((D_FF, tn), lambda b, i, n: (0, n)),
            pl.BlockSpec((None, tm, tn), lambda b, i, n: (b, i, n)),
            pl.BlockSpec((None, 1, tn), lambda b, i, n: (b, 0, gate_blk + n)),
        ],
        out_specs=pl.BlockSpec((None, tm, tn), lambda b, i, n: (b, i, n)),
        compiler_params=_params(("arbitrary", "arbitrary", "arbitrary")),
        name="ffn_down",
    )(act, w_down, x, mod3)


def _final_norm_kernel(x_ref, g_ref, o_ref):
    o_ref[...] = _plain_rmsnorm(x_ref[...], g_ref[...])


def _final_norm_call(x, g):
    bsz, length, d = x.shape
    tm = min(1024, length)
    return pl.pallas_call(
        _final_norm_kernel,
        out_shape=jax.ShapeDtypeStruct(x.shape, F32),
        grid=(bsz, length // tm),
        in_specs=[pl.BlockSpec((None, tm, d), lambda b, i: (b, i, 0)),
                  pl.BlockSpec((1, d), lambda b, i: (0, 0))],
        out_specs=pl.BlockSpec((None, tm, d), lambda b, i: (b, i, 0)),
        compiler_params=_params(("arbitrary", "arbitrary")),
        name="final_norm",
    )(x, g.reshape(1, d))


NA_KEYS = NA_WIN_R * GRID_W
MASKED = -1e30
NA_ROWS_PER_STEP = 8


def _na_bias_table(rpb):
    qc = np.arange(GRID_W)[:, None]
    kc = np.arange(GRID_W)[None, :]
    cs = np.clip(qc - NA_WIN_C // 2, 0, GRID_W - NA_WIN_C)
    valid = (kc >= cs) & (kc < cs + NA_WIN_C)
    dc = np.clip(kc - qc + NA_WIN_C - 1, 0, 2 * NA_WIN_C - 2)
    dr = np.arange(NA_WIN_R)[None, :] - np.arange(NA_WIN_R)[:, None] + NA_WIN_R - 1
    t = rpb.astype(F32)[:, dr][:, :, :, dc]
    t = jnp.where(valid[None, None, None], t * LOG2_E, MASKED)
    return t.transpose(0, 1, 3, 2, 4).reshape(rpb.shape[0], NA_WIN_R, GRID_W, NA_KEYS)


def _na_kernel(q_ref, k_ref, v_ref, bias_ref, o_ref, kb_ref, vb_ref):
    rows = q_ref.shape[0] // GRID_W
    kb_ref[...] = k_ref[...].astype(BF16)
    vb_ref[...] = v_ref[...].astype(BF16)

    def row_group(g, carry):
        todo = []
        for u in range(NA_ROWS_PER_STEP):
            r = g * NA_ROWS_PER_STEP + u
            rs = jnp.clip(r - NA_WIN_R // 2, 0, rows - NA_WIN_R)
            q0 = pl.multiple_of(r * GRID_W, GRID_W)
            k0 = pl.multiple_of(rs * GRID_W, GRID_W)
            q = (q_ref[pl.ds(q0, GRID_W), :] * (HEAD_DIM ** -0.5 * LOG2_E)).astype(BF16)
            s = lax.dot_general(q, kb_ref[pl.ds(k0, NA_KEYS), :], (((1,), (1,)), ((), ())),
                                preferred_element_type=F32)
            todo.append((q0, k0, s, r - rs))
        for u, (q0, k0, s, off) in enumerate(todo):
            s = s + bias_ref[off]
            p = jnp.exp2(s - jnp.max(s, axis=-1, keepdims=True))
            todo[u] = (q0, k0, p.astype(BF16), jnp.sum(p, axis=-1, keepdims=True))
        for u, (q0, k0, p, l) in enumerate(todo):
            todo[u] = (q0, jnp.dot(p, vb_ref[pl.ds(k0, NA_KEYS), :], preferred_element_type=F32), l)
        for q0, o, l in todo:
            o_ref[pl.ds(q0, GRID_W), :] = (o / l).astype(o_ref.dtype)
        return carry

    lax.fori_loop(0, rows // NA_ROWS_PER_STEP, row_group, 0)


def _na_call(main, rpb):
    bsz, length, _ = main.shape
    assert length % (GRID_W * NA_ROWS_PER_STEP) == 0 and length // GRID_W >= NA_WIN_R
    hd = HEAD_DIM
    return pl.pallas_call(
        _na_kernel,
        out_shape=jax.ShapeDtypeStruct((bsz, length, GROUP_W), BF16),
        grid=(bsz, H_GROUP),
        in_specs=[
            pl.BlockSpec((None, length, hd), lambda b, h: (b, 0, h)),
            pl.BlockSpec((None, length, hd), lambda b, h: (b, 0, H_GROUP + h)),
            pl.BlockSpec((None, length, hd), lambda b, h: (b, 0, 2 * H_GROUP + h)),
            pl.BlockSpec((None, NA_WIN_R, GRID_W, NA_KEYS), lambda b, h: (h, 0, 0, 0)),
        ],
        out_specs=pl.BlockSpec((None, length, hd), lambda b, h: (b, 0, h)),
        scratch_shapes=[pltpu.VMEM((length, hd), BF16), pltpu.VMEM((length, hd), BF16)],
        compiler_params=_params(("arbitrary", "arbitrary")),
        name="nbr_attn",
    )(main, main, main, _na_bias_table(rpb))


REC_TILE = 512
REC_HEADS = 8
HIGHEST = lax.Precision.HIGHEST


def _tri(n, upper):
    r = lax.broadcasted_iota(jnp.int32, (n, n), 0)
    c = lax.broadcasted_iota(jnp.int32, (n, n), 1)
    return (c >= r) if upper else (c <= r)


def _gdn_conv_kernel(x_ref, w_ref, o_ref, xp_ref):
    length, cw = x_ref.shape
    col0 = pl.program_id(1) * cw
    pad = 8
    xp_ref[0:pad, :] = jnp.zeros((pad, cw), F32)
    xp_ref[pad + length:, :] = jnp.zeros((pad, cw), F32)
    xp_ref[pad:pad + length, :] = x_ref[...]
    w = w_ref[...]
    y = xp_ref[pl.ds(pad - DN_CONV_K // 2, length), :] * w[0:1, :]
    for t in range(1, DN_CONV_K):
        y = y + xp_ref[pl.ds(pad - DN_CONV_K // 2 + t, length), :] * w[t:t + 1, :]
    y = y * jax.nn.sigmoid(y)
    qscale = jnp.where(col0 < GROUP_W, HEAD_DIM ** -0.5, 1.0)
    for h in range(cw // HEAD_DIM):
        yh = y[:, h * HEAD_DIM:(h + 1) * HEAD_DIM]
        nh = yh * (lax.rsqrt(jnp.sum(yh * yh, axis=-1, keepdims=True) + EPS) * qscale)
        o_ref[:, h * HEAD_DIM:(h + 1) * HEAD_DIM] = jnp.where(col0 < 2 * GROUP_W, nh, yh)


def _gdn_conv_call(main, conv_w):
    bsz, length, _ = main.shape
    cw = HEAD_DIM
    nblk = 3 * GROUP_W // cw
    return pl.pallas_call(
        _gdn_conv_kernel,
        out_shape=jax.ShapeDtypeStruct((bsz, length, 3 * GROUP_W), F32),
        grid=(bsz, nblk),
        in_specs=[pl.BlockSpec((None, length, cw), lambda b, j: (b, 0, nblk + j)),
                  pl.BlockSpec((DN_CONV_K, cw), lambda b, j: (0, j))],
        out_specs=pl.BlockSpec((None, length, cw), lambda b, j: (b, 0, j)),
        scratch_shapes=[pltpu.VMEM((length + 16, cw), F32)],
        compiler_params=_params(("arbitrary", "arbitrary")),
        name="gdn_conv",
    )(main, conv_w)


def _gdn_gates_kernel(s_ref, alog_ref, dt_ref, g_ref, gt_ref):
    tm = s_ref.shape[0]
    s = s_ref[...]
    x = s + dt_ref[...]
    softplus = jnp.maximum(x, 0.0) + jnp.log1p(jnp.exp(-jnp.abs(x)))
    g = -jnp.exp(alog_ref[...]) * softplus
    r = lax.broadcasted_iota(jnp.int32, (tm, tm), 0)
    c = lax.broadcasted_iota(jnp.int32, (tm, tm), 1)
    same = (r // DN_CHUNK) == (c // DN_CHUNK)
    lo = jnp.where(same & (c <= r), 1.0, 0.0)
    up = jnp.where(same & (c >= r), 1.0, 0.0)
    gcf = jnp.dot(lo, g, precision=HIGHEST, preferred_element_type=F32)
    gcb = jnp.dot(up, g, precision=HIGHEST, preferred_element_type=F32)
    lane = lax.broadcasted_iota(jnp.int32, s.shape, 1)
    out = jnp.where(lane < H_GROUP, gcf, jnp.where(lane < 2 * H_GROUP, gcb, jax.nn.sigmoid(s)))
    g_ref[...] = out
    for p in range(tm // 128):
        gt_ref[p] = out[p * 128:(p + 1) * 128, :].T[0:4 * H_GROUP, :]


def _gdn_gates_call(side, a_log, dt_bias):
    bsz, length, _ = side.shape
    tm = min(512, length)
    row = lambda t: jnp.pad(t.reshape(1, 2 * H_GROUP).astype(F32), ((0, 0), (0, SIDE_W - 2 * H_GROUP)))
    return pl.pallas_call(
        _gdn_gates_kernel,
        out_shape=(jax.ShapeDtypeStruct((bsz, length, SIDE_W), F32),
                   jax.ShapeDtypeStruct((bsz, length // 128, 4 * H_GROUP, 128), F32)),
        grid=(bsz, length // tm),
        in_specs=[pl.BlockSpec((None, tm, SIDE_W), lambda b, i: (b, i, 0)),
                  pl.BlockSpec((1, SIDE_W), lambda b, i: (0, 0)),
                  pl.BlockSpec((1, SIDE_W), lambda b, i: (0, 0))],
        out_specs=(pl.BlockSpec((None, tm, SIDE_W), lambda b, i: (b, i, 0)),
                   pl.BlockSpec((None, tm // 128, 4 * H_GROUP, 128), lambda b, i: (b, i, 0, 0))),
        compiler_params=_params(("arbitrary", "arbitrary")),
        name="gdn_gates",
    )(side, row(a_log), row(dt_bias))


def _bdot(a, b):
    return jnp.dot(a.astype(BF16), b.astype(BF16), preferred_element_type=F32)


def _gdn_chunks_local(chunks):
    c = DN_CHUNK
    r = lax.broadcasted_iota(jnp.int32, (c, c), 0)
    cc = lax.broadcasted_iota(jnp.int32, (c, c), 1)
    for d in chunks:
        incl = (cc >= r) if d["rev"] else (cc <= r)
        d["gam"] = jnp.exp(jnp.where(incl, d["gci"] - d["gcj"], MASKED))
        d["kb"] = d["k"] * d["beta"]
    for d in chunks:
        kq = jnp.concatenate([d["kb"], d["q"]], axis=0).astype(BF16)
        d["prod"] = lax.dot_general(kq, d["k"].astype(BF16), (((1,), (1,)), ((), ())),
                                    preferred_element_type=F32)
    for d in chunks:
        strict = (cc > r) if d["rev"] else (cc < r)
        d["aqk"] = d["prod"][c:] * d["gam"]
        d["p"] = -jnp.where(strict, d["prod"][0:c] * d["gam"], 0.0)
        d["t"] = jnp.where(r == cc, 1.0, 0.0) + d["p"]
    for _ in range(int(math.log2(c)) - 1):
        for d in chunks:
            d["p"] = _bdot(d["p"], d["p"])
        for d in chunks:
            d["t"] = d["t"] + _bdot(d["t"], d["p"])
    for d in chunks:
        egc = jnp.exp(d["gci"])
        rhs = jnp.concatenate([d["v"] * d["beta"], d["kb"] * egc], axis=1)
        uw = _bdot(d["t"], rhs)
        d["u"], d["w"] = uw[:, 0:HEAD_DIM], uw[:, HEAD_DIM:]
        glast = d["gci"][0:1, :] if d["rev"] else d["gci"][c - 1:c, :]
        d["qd"] = d["q"] * egc
        d["kd"] = d["k"] * jnp.exp(glast - d["gci"])
        d["dlast"] = jnp.exp(glast)


def _gdn_chunks_scan(chunks):
    c = DN_CHUNK
    for d in chunks:
        d["s"] = d["s_ref"][...]
        d["ws"] = _bdot(jnp.concatenate([d["w"], d["qd"]], axis=0), d["s"])
    for d in chunks:
        d["vn"] = (d["u"] - d["ws"][0:c]).astype(BF16)
        d["o"] = d["ws"][c:] + _bdot(d["aqk"], d["vn"])
    for d in chunks:
        d["s_ref"][...] = d["s"] * d["dlast"] + lax.dot_general(
            d["kd"].astype(BF16), d["vn"], (((0,), (0,)), ((), ())), preferred_element_type=F32)


def _gdn_kernel(qf_ref, kf_ref, vf_ref, gf_ref, gtf_ref, qb_ref, kb_ref, vb_ref, gb_ref, gtb_ref,
                of_ref, ob_ref, s_ref):
    hg = pl.program_id(1)

    @pl.when(pl.program_id(2) == 0)
    def _():
        s_ref[...] = jnp.zeros_like(s_ref)

    npair = qf_ref.shape[0] // 128
    c = DN_CHUNK

    def pair(p, carry):
        chunks = []
        for rev in (False, True):
            q_ref, k_ref, v_ref, g_ref, gt_ref, o_ref = (
                (qb_ref, kb_ref, vb_ref, gb_ref, gtb_ref, ob_ref) if rev
                else (qf_ref, kf_ref, vf_ref, gf_ref, gtf_ref, of_ref))
            pp = (npair - 1 - p) if rev else p
            base = pl.multiple_of(pp * 128, 128)
            gt = gt_ref[pp]
            sub = lax.broadcasted_iota(jnp.int32, gt.shape, 0)
            for step, half in enumerate((1, 0) if rev else (0, 1)):
                rows = pl.ds(base + half * c, c)
                g = g_ref[rows, :]
                lane = lax.broadcasted_iota(jnp.int32, g.shape, 1)
                for hl in range(REC_HEADS):
                    cols = slice(hl * HEAD_DIM, (hl + 1) * HEAD_DIM)
                    gcol = hg * REC_HEADS + hl + (H_GROUP if rev else 0)
                    chunks.append(dict(
                        rev=rev, step=step, rows=rows, cols=cols, o_ref=o_ref,
                        s_ref=s_ref.at[int(rev), hl],
                        q=q_ref[rows, cols], k=k_ref[rows, cols], v=v_ref[rows, cols],
                        gci=jnp.sum(jnp.where(lane == gcol, g, 0.0), axis=-1, keepdims=True),
                        beta=jnp.sum(jnp.where(lane == gcol + 2 * H_GROUP, g, 0.0), axis=-1,
                                     keepdims=True),
                        gcj=jnp.sum(jnp.where(sub == gcol, gt, 0.0), axis=0,
                                    keepdims=True)[:, half * c:(half + 1) * c]))
        _gdn_chunks_local(chunks)
        for step in (0, 1):
            now = [d for d in chunks if d["step"] == step]
            _gdn_chunks_scan(now)
            for d in now:
                d["o_ref"][d["rows"], d["cols"]] = d["o"]
        return carry

    lax.fori_loop(0, npair, pair, 0)


def _gdn_call(main, side, conv_w, a_log, dt_bias):
    bsz, length, _ = main.shape
    tl = min(REC_TILE, length)
    nt = length // tl
    w = REC_HEADS * HEAD_DIM
    ng = H_GROUP // REC_HEADS
    qkv = _gdn_conv_call(main, conv_w)
    gates, gates_t = _gdn_gates_call(side, a_log, dt_bias)

    def specs(tile):
        return [pl.BlockSpec((None, tl, w), lambda b, g, i: (b, tile(i), g)),
                pl.BlockSpec((None, tl, w), lambda b, g, i: (b, tile(i), ng + g)),
                pl.BlockSpec((None, tl, w), lambda b, g, i: (b, tile(i), 2 * ng + g)),
                pl.BlockSpec((None, tl, SIDE_W), lambda b, g, i: (b, tile(i), 0)),
                pl.BlockSpec((None, tl // 128, 4 * H_GROUP, 128), lambda b, g, i: (b, tile(i), 0, 0))]

    fw = lambda i: i
    bw = lambda i: nt - 1 - i
    return pl.pallas_call(
        _gdn_kernel,
        out_shape=(jax.ShapeDtypeStruct((bsz, length, GROUP_W), F32),) * 2,
        grid=(bsz, ng, nt),
        in_specs=specs(fw) + specs(bw),
        out_specs=(pl.BlockSpec((None, tl, w), lambda b, g, i: (b, fw(i), g)),
                   pl.BlockSpec((None, tl, w), lambda b, g, i: (b, bw(i), g))),
        scratch_shapes=[pltpu.VMEM((2, REC_HEADS, HEAD_DIM, HEAD_DIM), F32)],
        compiler_params=_params(("arbitrary", "arbitrary", "arbitrary")),
        name="gdn_scan",
    )(qkv, qkv, qkv, gates, gates_t, qkv, qkv, qkv, gates, gates_t)


HG_TILE = 64
HG_SUB = 8
HG_NSUB = HG_TILE // HG_SUB


def _tri_cumsum(x, rev):
    n = x.shape[0]
    tri = jnp.where(_tri(n, rev), 1.0, 0.0).astype(BF16)
    hi = x.astype(BF16)
    r1 = x - hi.astype(F32)
    mid = r1.astype(BF16)
    lo = (r1 - mid.astype(F32)).astype(BF16)
    parts = jnp.dot(tri, jnp.concatenate([hi, mid, lo], axis=1), preferred_element_type=F32)
    w = x.shape[1]
    return (parts[:, 0:w] + parts[:, w:2 * w]) + parts[:, 2 * w:]


def _hg_chunks(chains):
    n, sub, nsub = HG_TILE, HG_SUB, HG_NSUB
    row = lax.broadcasted_iota(jnp.int32, (n, HEAD_DIM), 0)
    pos = row % sub
    tn_dims = (((0,), (0,)), ((), ()))
    for c in chains:
        z, loglb, log1mlb = c["z"], c["loglb"], c["log1mlb"]
        c["q"] = c["q_raw"] * jax.nn.sigmoid(c["q_raw"])
        log_sig = jnp.minimum(z, 0.0) - jnp.log1p(jnp.exp(-jnp.abs(z)))
        bb = log1mlb + log_sig
        lf = jnp.maximum(loglb, bb) + jnp.log1p(jnp.exp(-jnp.abs(loglb - bb)))
        b = _tri_cumsum(lf, c["rev"])
        c["b2"] = b * LOG2_E
        c["c2"] = (b - (bb - z)) * LOG2_E
        c["inp16"] = c["inp"].astype(BF16)

    for c in chains:
        b2, c2, rev = c["b2"], c["c2"], c["rev"]
        c["order"] = list(range(nsub - 1, -1, -1)) if rev else list(range(nsub))
        c["ends"], c["contrib"] = [], []
        for j in c["order"]:
            lo = j * sub
            bend = b2[lo:lo + 1, :] if rev else b2[lo + sub - 1:lo + sub, :]
            c["ends"].append(bend)
            if len(c["contrib"]) < nsub - 1:
                ks = jnp.exp2(bend - c2[lo:lo + sub, :])
                c["contrib"].append(lax.dot_general(c["inp16"][lo:lo + sub, :], ks.astype(BF16),
                                                    tn_dims, preferred_element_type=F32))

    for c in chains:
        b2, c2, q, inp, rev = c["b2"], c["c2"], c["q"], c["inp"], c["rev"]
        o = jnp.sum(q * jnp.exp2(b2 - c2), axis=-1, keepdims=True) * inp
        for d in range(1, sub):
            shift = (sub - d) % sub if rev else d
            valid = (pos <= sub - 1 - d) if rev else (pos >= d)
            sh = lambda x: pltpu.roll(x.reshape(nsub, sub, HEAD_DIM), shift, axis=1).reshape(
                n, HEAD_DIM)
            e = jnp.where(valid, jnp.exp2(b2 - sh(c2)), 0.0)
            o = o + jnp.sum(q * e, axis=-1, keepdims=True) * sh(inp)
        c["o"] = o

    for c in chains:
        b2, q, rev = c["b2"], c["q"], c["rev"]
        c["s"] = c["s_ref"][...]
        qext = [q * jnp.exp2(b2)]
        for m in range(nsub - 1):
            j = c["order"][m]
            lo, hi = (0, j * sub) if rev else ((j + 1) * sub, n)
            part = q[lo:hi] * jnp.exp2(b2[lo:hi] - c["ends"][m])
            zeros = jnp.zeros((n - (hi - lo), HEAD_DIM), F32)
            qext.append(jnp.concatenate([part, zeros] if rev else [zeros, part], axis=0))
        ncat = jnp.concatenate([c["s"]] + c["contrib"], axis=1).astype(BF16)
        c["o"] = c["o"] + lax.dot_general(jnp.concatenate(qext, axis=1).astype(BF16), ncat,
                                          (((1,), (1,)), ((), ())), preferred_element_type=F32)
    for c in chains:
        btot = c["ends"][-1]
        kd = jnp.exp2(btot - c["c2"]).astype(BF16)
        c["s_ref"][...] = c["s"] * jnp.exp2(btot) + lax.dot_general(
            c["inp16"], kd, tn_dims, preferred_element_type=F32)


def _hgrn2_kernel(qf_ref, ff_ref, if_ref, qb_ref, fb_ref, ib_ref, lbf_ref, lbb_ref,
                  of_ref, ob_ref, s_ref):
    @pl.when(pl.program_id(2) == 0)
    def _():
        s_ref[...] = jnp.zeros_like(s_ref)

    nstep = qf_ref.shape[0] // HG_TILE

    def step(t, carry):
        chains = []
        for rev in (False, True):
            q_ref, f_ref, i_ref, lb_ref, o_ref = ((qb_ref, fb_ref, ib_ref, lbb_ref, ob_ref) if rev
                                                  else (qf_ref, ff_ref, if_ref, lbf_ref, of_ref))
            tt = (nstep - 1 - t) if rev else t
            rows = pl.ds(pl.multiple_of(tt * HG_TILE, HG_TILE), HG_TILE)
            for hl in range(REC_HEADS):
                cols = slice(hl * HEAD_DIM, (hl + 1) * HEAD_DIM)
                chains.append(dict(rev=rev, rows=rows, cols=cols, o_ref=o_ref,
                                   s_ref=s_ref.at[int(rev), hl], q_raw=q_ref[rows, cols],
                                   z=f_ref[rows, cols], inp=i_ref[rows, cols],
                                   loglb=lb_ref[0:1, cols], log1mlb=lb_ref[1:2, cols]))
        _hg_chunks(chains)
        for c in chains:
            c["o_ref"][c["rows"], c["cols"]] = c["o"]
        return carry

    lax.fori_loop(0, nstep, step, 0)


def _hgrn2_call(main, lb_fw, lb_bw):
    bsz, length, _ = main.shape
    tl = min(REC_TILE, length)
    nt = length // tl
    w = REC_HEADS * HEAD_DIM
    ng = H_GROUP // REC_HEADS
    fw = lambda i: i
    bw = lambda i: nt - 1 - i
    col = lambda group, tile: pl.BlockSpec((None, tl, w), lambda b, g, i: (b, tile(i), group * ng + g))
    lbrow = pl.BlockSpec((2, w), lambda b, g, i: (0, g))
    logs = lambda lb: jnp.stack([jnp.log(lb), jnp.log1p(-lb)])
    return pl.pallas_call(
        _hgrn2_kernel,
        out_shape=(jax.ShapeDtypeStruct((bsz, length, GROUP_W), F32),) * 2,
        grid=(bsz, ng, nt),
        in_specs=[col(0, fw), col(1, fw), col(3, fw), col(0, bw), col(2, bw), col(3, bw), lbrow, lbrow],
        out_specs=(pl.BlockSpec((None, tl, w), lambda b, g, i: (b, fw(i), g)),
                   pl.BlockSpec((None, tl, w), lambda b, g, i: (b, bw(i), g))),
        scratch_shapes=[pltpu.VMEM((2, REC_HEADS, HEAD_DIM, HEAD_DIM), F32)],
        compiler_params=_params(("arbitrary", "arbitrary", "arbitrary")),
        name="hgrn2_scan",
    )(main, main, main, main, main, main, logs(lb_fw), logs(lb_bw))


MLA_QK = 2 * MLA_NOPE
MLA_KEY_BLOCK = 512
ROPE_SWAP = np.concatenate([np.arange(MLA_ROPE // 2, MLA_ROPE), np.arange(MLA_ROPE // 2)])


def _rope_table(length):
    half = MLA_ROPE // 2
    inv = ROPE_BASE ** (-jnp.arange(half, dtype=F32) / half)
    ang = jnp.arange(length, dtype=F32)[:, None] * inv[None, :]
    cos, sin = jnp.cos(ang), jnp.sin(ang)
    return jnp.concatenate([cos, cos, -sin, sin], axis=-1)


def _mla_q_weights(w_uq):
    w = w_uq.reshape(MLA_Q_LORA, H_GROUP, MLA_NOPE + MLA_ROPE)
    rope = w[:, :, MLA_NOPE:]
    w = jnp.concatenate([w[:, :, :MLA_NOPE], rope, rope[:, :, ROPE_SWAP]], axis=-1)
    return w.reshape(MLA_Q_LORA, H_GROUP * MLA_QK).astype(BF16)


def _plain_rmsnorm(x, g):
    return x * lax.rsqrt(jnp.mean(x * x, axis=-1, keepdims=True) + EPS) * g


def _mla_proj_kernel(cq_ref, ckv_ref, side_ref, tab_ref, qn_ref, kvn_ref, wq_ref, wkv_ref,
                     q_out, k_out, v_out):
    scale = (MLA_NOPE + MLA_ROPE) ** -0.5 * LOG2_E
    cqn = _plain_rmsnorm(cq_ref[...], qn_ref[...]).astype(BF16)
    ckvn = _plain_rmsnorm(ckv_ref[...], kvn_ref[...]).astype(BF16)
    q = jnp.dot(cqn, wq_ref[...], preferred_element_type=F32)
    kv = jnp.dot(ckvn, wkv_ref[...], preferred_element_type=F32)
    tab = tab_ref[...]
    kr = (side_ref[...] * tab).astype(BF16)
    for h in range(H_GROUP):
        c0 = h * MLA_QK
        qr = q[:, c0 + MLA_NOPE:c0 + MLA_QK] * tab
        qr = qr + pltpu.roll(qr, MLA_ROPE, axis=1)
        q_out[h, :, 0:MLA_NOPE] = (q[:, c0:c0 + MLA_NOPE] * scale).astype(BF16)
        q_out[h, :, MLA_NOPE:MLA_QK] = (qr * scale).astype(BF16)
        k_out[h, :, 0:MLA_NOPE] = kv[:, c0:c0 + MLA_NOPE].astype(BF16)
        k_out[h, :, MLA_NOPE:MLA_QK] = kr
        v_out[h] = kv[:, c0 + MLA_NOPE:c0 + MLA_QK].astype(BF16)


def _mla_attn_kernel(q_ref, k_ref, v_ref, o_ref):
    length = k_ref.shape[0]
    tk = MLA_KEY_BLOCK
    nk = length // tk
    q = q_ref[...]

    def scores(kb):
        return lax.dot_general(q, k_ref[kb * tk:(kb + 1) * tk, :], (((1,), (1,)), ((), ())),
                               preferred_element_type=F32)

    s_next = scores(0)
    m = l = acc = None
    for kb in range(nk):
        s = s_next
        if kb + 1 < nk:
            s_next = scores(kb + 1)
        m_blk = jnp.max(s, axis=-1, keepdims=True)
        if kb == 0:
            m = m_blk
            p = jnp.exp2(s - m)
            l = jnp.sum(p, axis=-1, keepdims=True)
            acc = jnp.dot(p.astype(BF16), v_ref[0:tk, :], preferred_element_type=F32)
        else:
            m_new = jnp.maximum(m, m_blk)
            alpha = jnp.exp2(m - m_new)
            p = jnp.exp2(s - m_new)
            l = alpha * l + jnp.sum(p, axis=-1, keepdims=True)
            acc = alpha * acc + jnp.dot(p.astype(BF16), v_ref[kb * tk:(kb + 1) * tk, :],
                                        preferred_element_type=F32)
            m = m_new
    o_ref[...] = (acc / l).astype(o_ref.dtype)


def _mla_call(main, side, q_norm, w_uq, kv_norm, w_ukv):
    bsz, length, _ = main.shape
    tm = min(512, length)
    lora = MLA_Q_LORA
    q_all, k_all, v_all = pl.pallas_call(
        _mla_proj_kernel,
        out_shape=(jax.ShapeDtypeStruct((bsz, H_GROUP, length, MLA_QK), BF16),
                   jax.ShapeDtypeStruct((bsz, H_GROUP, length, MLA_QK), BF16),
                   jax.ShapeDtypeStruct((bsz, H_GROUP, length, MLA_V), BF16)),
        grid=(bsz, length // tm),
        in_specs=[
            pl.BlockSpec((None, tm, lora), lambda b, i: (b, i, 10)),
            pl.BlockSpec((None, tm, lora), lambda b, i: (b, i, 11)),
            pl.BlockSpec((None, tm, SIDE_W), lambda b, i: (b, i, 0)),
            pl.BlockSpec((tm, SIDE_W), lambda b, i: (i, 0)),
            pl.BlockSpec((1, lora), lambda b, i: (0, 0)),
            pl.BlockSpec((1, lora), lambda b, i: (0, 0)),
            pl.BlockSpec((lora, H_GROUP * MLA_QK), lambda b, i: (0, 0)),
            pl.BlockSpec((lora, H_GROUP * MLA_QK), lambda b, i: (0, 0)),
        ],
        out_specs=(pl.BlockSpec((None, H_GROUP, tm, MLA_QK), lambda b, i: (b, 0, i, 0)),
                   pl.BlockSpec((None, H_GROUP, tm, MLA_QK), lambda b, i: (b, 0, i, 0)),
                   pl.BlockSpec((None, H_GROUP, tm, MLA_V), lambda b, i: (b, 0, i, 0))),
        compiler_params=_params(("arbitrary", "arbitrary")),
        name="mla_proj",
    )(main, main, side, _rope_table(length), q_norm.reshape(1, lora), kv_norm.reshape(1, lora),
      _mla_q_weights(w_uq), w_ukv.astype(BF16))

    tq = 512
    return pl.pallas_call(
        _mla_attn_kernel,
        out_shape=jax.ShapeDtypeStruct((bsz, length, H_GROUP * MLA_V), BF16),
        grid=(bsz, H_GROUP, length // tq),
        in_specs=[
            pl.BlockSpec((None, None, tq, MLA_QK), lambda b, h, i: (b, h, i, 0)),
            pl.BlockSpec((None, None, length, MLA_QK), lambda b, h, i: (b, h, 0, 0)),
            pl.BlockSpec((None, None, length, MLA_V), lambda b, h, i: (b, h, 0, 0)),
        ],
        out_specs=pl.BlockSpec((None, tq, MLA_V), lambda b, h, i: (b, i, h)),
        compiler_params=_params(("arbitrary", "arbitrary", "arbitrary")),
        name="mla_attn",
    )(q_all, k_all, v_all)


def _pad_side(w):
    return jnp.pad(w, ((0, 0), (0, SIDE_W - w.shape[1]))).astype(BF16)


def _trunk(x, mods, lb, p):
    for l in range(DEPTH):
        mod3 = mods[l]
        if l % 2 == 0:
            e = l // 2
            main, side = _inproj_call(x, mod3, p["norm_mix"][l], p["w_in_ab_main"][e],
                                      p["w_in_ab_side"][e])
            att = _na_call(main, p["na_rpb"][e])
            o_fw, o_bw = _gdn_call(main, side, p["dn_conv"][e], p["dn_a_log"][e], p["dn_dt_bias"][e])
            x = _outproj_call(att, o_fw, o_bw, main, 6, p["dn_norm"][e], False, p["w_out_ab"][e],
                              x, mod3)
        else:
            o = l // 2
            main, side = _inproj_call(x, mod3, p["norm_mix"][l], p["w_in_cd_main"][o],
                                      p["w_in_cd_side"][o])
            o_fw, o_bw = _hgrn2_call(main, lb[0, l], lb[1, l])
            att = _mla_call(main, side, p["mla_q_norm"][o], p["mla_w_uq"][o], p["mla_kv_norm"][o],
                            p["mla_w_ukv"][o])
            x = _outproj_call(att, o_fw, o_bw, main, 4, p["hg_norm"][o], True, p["w_out_cd"][o],
                              x, mod3)
        x = _ffn_split_call(x, mod3, p["norm_ffn"][l], p["ffn_w_gate"][l], p["ffn_w_up"][l],
                            p["ffn_conv"][l], p["ffn_w_down"][l])
    return _final_norm_call(x, p["final_norm"])


def kernel(x_prompt, x_sample, c_prompt, c_sample, norm_mix, norm_ffn, w_ada, b_ada, w_in_ab, w_out_ab, na_rpb, dn_conv, dn_a_log, dn_dt_bias, dn_norm, w_in_cd, w_out_cd, hg_lower_bounds, hg_norm, mla_q_norm, mla_w_uq, mla_kv_norm, mla_w_ukv, ffn_w_gate, ffn_w_up, ffn_conv, ffn_w_down, final_norm):
    n_p, n_s = c_prompt.shape[0], c_sample.shape[0]
    rows = -(-(n_p + n_s) // 8) * 8
    c_all = jnp.concatenate([c_prompt, c_sample, jnp.zeros((rows - n_p - n_s, D_MODEL), F32)], axis=0)
    mod_all = _ada_call(c_all, w_ada, b_ada)
    mods_p = [mod_all[l, :n_p].reshape(n_p, 1, 6 * D_MODEL) for l in range(DEPTH)]
    mods_s = [mod_all[l, n_p:n_p + n_s].reshape(n_s, 1, 6 * D_MODEL) for l in range(DEPTH)]

    lb = jnp.cumsum(jax.nn.softmax(hg_lower_bounds.astype(F32), axis=1), axis=1)
    lb = lb - lb[:, :1]

    p = dict(
        norm_mix=norm_mix, norm_ffn=norm_ffn, na_rpb=na_rpb, dn_conv=dn_conv, dn_a_log=dn_a_log,
        dn_dt_bias=dn_dt_bias, dn_norm=dn_norm, hg_norm=hg_norm, mla_q_norm=mla_q_norm,
        mla_w_uq=mla_w_uq, mla_kv_norm=mla_kv_norm, mla_w_ukv=mla_w_ukv, ffn_conv=ffn_conv,
        final_norm=final_norm,
        w_in_ab_main=w_in_ab[:, :, :AB_MAIN].astype(BF16),
        w_in_ab_side=jax.vmap(_pad_side)(w_in_ab[:, :, AB_MAIN:]),
        w_in_cd_main=w_in_cd[:, :, :CD_MAIN].astype(BF16),
        w_in_cd_side=jnp.concatenate([w_in_cd[:, :, CD_MAIN:],
                                      w_in_cd[:, :, CD_MAIN:][:, :, ROPE_SWAP]], axis=-1).astype(BF16),
        w_out_ab=w_out_ab.astype(BF16), w_out_cd=w_out_cd.astype(BF16),
        ffn_w_gate=ffn_w_gate.astype(BF16), ffn_w_up=ffn_w_up.astype(BF16),
        ffn_w_down=ffn_w_down.astype(BF16),
    )
    y_prompt = _trunk(x_prompt, mods_p, lb, p)
    y_sample = _trunk(x_sample, mods_s, lb, p)
    return (y_prompt, y_sample)
```

```python
import functools
import math

import numpy as np
import jax
import jax.numpy as jnp
from jax import lax
from jax.experimental import pallas as pl
from jax.experimental.pallas import tpu as pltpu

D_MODEL = 2048
DEPTH = 4
HEAD_DIM = 128
H_GROUP = 8
GROUP_W = H_GROUP * HEAD_DIM
GRID_W = 64
NA_WIN_R = 8
NA_WIN_C = 16
DN_CONV_K = 5
DN_CHUNK = 64
MLA_Q_LORA = 512
MLA_KV_LORA = 512
MLA_NOPE = 128
MLA_ROPE = 64
MLA_V = 128
ROPE_BASE = 10000.0
D_FF = 5632
FFN_CONV_K = 3
EPS = 1e-6
AB_MAIN = 7 * GROUP_W
CD_MAIN = 6 * GROUP_W
SIDE_W = 128

F32 = jnp.float32
BF16 = jnp.bfloat16
LOG2_E = 1.4426950408889634

VMEM_LIMIT_BYTES = 56 * 1024 * 1024
HALO = 16
def _params(sem):
    return pltpu.CompilerParams(dimension_semantics=sem, vmem_limit_bytes=VMEM_LIMIT_BYTES)


def _ada_kernel(c_ref, w_ref, b_ref, o_ref):
    c = c_ref[...]
    cond = c * jax.nn.sigmoid(c)
    o_ref[...] = jnp.dot(cond, w_ref[...], precision=lax.Precision.HIGHEST,
                         preferred_element_type=F32) + b_ref[...]


def _ada_call(c_all, w_ada, b_ada):
    rows = c_all.shape[0]
    tn = 1024
    return pl.pallas_call(
        _ada_kernel,
        out_shape=jax.ShapeDtypeStruct((DEPTH, rows, 6 * D_MODEL), F32),
        grid=(DEPTH, 6 * D_MODEL // tn),
        in_specs=[
            pl.BlockSpec((rows, D_MODEL), lambda l, j: (0, 0)),
            pl.BlockSpec((None, D_MODEL, tn), lambda l, j: (l, 0, j)),
            pl.BlockSpec((None, 1, tn), lambda l, j: (l, 0, j)),
        ],
        out_specs=pl.BlockSpec((None, rows, tn), lambda l, j: (l, 0, j)),
        compiler_params=_params(("arbitrary", "arbitrary")),
        name="ada_mod",
    )(c_all, w_ada, b_ada.reshape(DEPTH, 1, 6 * D_MODEL))


def _norm_mod(x, g, sh, sc):
    y = x * lax.rsqrt(jnp.mean(x * x, axis=-1, keepdims=True) + EPS)
    return (y * g) * (1.0 + sc) + sh


def _inproj_kernel(x_ref, g_ref, sh_ref, sc_ref, w_ref, ws_ref, o_ref, os_ref, h_ref):
    @pl.when(pl.program_id(2) == 0)
    def _():
        hb = _norm_mod(x_ref[...], g_ref[...], sh_ref[...], sc_ref[...]).astype(BF16)
        h_ref[...] = hb
        os_ref[...] = jnp.dot(hb, ws_ref[...], preferred_element_type=F32)

    o_ref[...] = jnp.dot(h_ref[...], w_ref[...], preferred_element_type=F32)


def _inproj_call(x, mod3, norm_g, w_main, w_side):
    bsz, length, d = x.shape
    n_main = w_main.shape[1]
    tm = min(1024, length)
    tn = 1024
    return pl.pallas_call(
        _inproj_kernel,
        out_shape=(jax.ShapeDtypeStruct((bsz, length, n_main), F32),
                   jax.ShapeDtypeStruct((bsz, length, SIDE_W), F32)),
        grid=(bsz, length // tm, n_main // tn),
        in_specs=[
            pl.BlockSpec((None, tm, d), lambda b, i, j: (b, i, 0)),
            pl.BlockSpec((1, d), lambda b, i, j: (0, 0)),
            pl.BlockSpec((None, 1, d), lambda b, i, j: (b, 0, 0)),
            pl.BlockSpec((None, 1, d), lambda b, i, j: (b, 0, 1)),
            pl.BlockSpec((d, tn), lambda b, i, j: (0, j)),
            pl.BlockSpec((d, SIDE_W), lambda b, i, j: (0, 0)),
        ],
        out_specs=(pl.BlockSpec((None, tm, tn), lambda b, i, j: (b, i, j)),
                   pl.BlockSpec((None, tm, SIDE_W), lambda b, i, j: (b, i, 0))),
        scratch_shapes=[pltpu.VMEM((tm, d), BF16)],
        compiler_params=_params(("arbitrary", "arbitrary", "arbitrary")),
        name="inproj",
    )(x, norm_g.reshape(1, d), mod3, mod3, w_main, w_side)


def _outproj_kernel(rec_first, att_ref, of_ref, ob_ref, z_ref, ng_ref, w_ref, x_ref, gate_ref,
                    g2_ref, sh2_ref, sc2_ref, o_ref, h2_ref, y_ref):
    half = att_ref.shape[-1]
    ng = ng_ref[...]
    for h in range(H_GROUP):
        sl = slice(h * HEAD_DIM, (h + 1) * HEAD_DIM)
        o = of_ref[:, sl] + ob_ref[:, sl]
        o = o * lax.rsqrt(jnp.mean(o * o, axis=-1, keepdims=True) + EPS) * ng
        z = z_ref[:, sl]
        y_ref[:, sl] = (o * (z * jax.nn.sigmoid(z))).astype(BF16)
    rec0, att0 = (0, half) if rec_first else (half, 0)
    y = jnp.dot(y_ref[...], w_ref[rec0:rec0 + half, :], preferred_element_type=F32)
    y = y + jnp.dot(att_ref[...], w_ref[att0:att0 + half, :], preferred_element_type=F32)
    xo = x_ref[...] + gate_ref[...] * y
    o_ref[...] = xo
    h2_ref[...] = _norm_mod(xo, g2_ref[...], sh2_ref[...], sc2_ref[...]).astype(BF16)


def _outproj_call(att, o_fw, o_bw, main, gate_group, norm_g, rec_first, w_out, x, mod3, norm_ffn_g):
    bsz, length, d = x.shape
    half = att.shape[-1]
    tm = min(512, length)
    blk = lambda col: pl.BlockSpec((None, tm, half), lambda b, i: (b, i, col))
    row = pl.BlockSpec((None, tm, d), lambda b, i: (b, i, 0))
    mod = lambda k: pl.BlockSpec((None, 1, d), lambda b, i: (b, 0, k))
    return pl.pallas_call(
        functools.partial(_outproj_kernel, rec_first),
        out_shape=(jax.ShapeDtypeStruct(x.shape, F32), jax.ShapeDtypeStruct(x.shape, BF16)),
        grid=(bsz, length // tm),
        in_specs=[
            blk(0), blk(0), blk(0), blk(gate_group),
            pl.BlockSpec((1, HEAD_DIM), lambda b, i: (0, 0)),
            pl.BlockSpec((2 * half, d), lambda b, i: (0, 0)),
            row,
            mod(2),
            pl.BlockSpec((1, d), lambda b, i: (0, 0)),
            mod(3), mod(4),
        ],
        out_specs=(row, row),
        scratch_shapes=[pltpu.VMEM((tm, half), BF16)],
        compiler_params=_params(("arbitrary", "arbitrary")),
        name="outproj",
    )(att, o_fw, o_bw, main, norm_g.reshape(1, HEAD_DIM), w_out, x, mod3,
      norm_ffn_g.reshape(1, d), mod3, mod3)


def _ffn_up_kernel(hp_ref, hm_ref, hn_ref, wg_ref, wu_ref, cw_ref, act_ref, h_ref, gp_ref):
    i = pl.program_id(1)
    tm = hm_ref.shape[0]

    @pl.when(pl.program_id(2) == 0)
    def _():
        h_ref[HALO:HALO + tm, :] = hm_ref[...]
        h_ref[0:HALO, :] = jnp.where(i == 0, 0.0, hp_ref[...]).astype(BF16)
        h_ref[HALO + tm:, :] = jnp.where(i == pl.num_programs(1) - 1, 0.0, hn_ref[...]).astype(BF16)

    gp_ref[...] = jnp.dot(h_ref[...], wg_ref[...], preferred_element_type=F32)
    u = jnp.dot(hm_ref[...], wu_ref[...], preferred_element_type=F32)
    cw = cw_ref[...]
    a = gp_ref[pl.ds(HALO - 1, tm), :] * cw[0:1, :]
    a = a + gp_ref[pl.ds(HALO, tm), :] * cw[1:2, :]
    a = a + gp_ref[pl.ds(HALO + 1, tm), :] * cw[2:3, :]
    act_ref[...] = ((a * jax.nn.sigmoid(a)) * u).astype(BF16)


def _ffn_down_kernel(act_ref, wd_ref, x_ref, gate_ref, o_ref):
    o_ref[...] = x_ref[...] + gate_ref[...] * jnp.dot(act_ref[...], wd_ref[...],
                                                     preferred_element_type=F32)


def _ffn_split_call(x, h2, mod3, w_gate, w_up, conv_w, w_down):
    bsz, length, d = x.shape
    tm = min(1024, length)
    tf = 512
    nh = tm // HALO
    last_h = length // HALO - 1
    act = pl.pallas_call(
        _ffn_up_kernel,
        out_shape=jax.ShapeDtypeStruct((bsz, length, D_FF), BF16),
        grid=(bsz, length // tm, D_FF // tf),
        in_specs=[
            pl.BlockSpec((None, HALO, d), lambda b, i, j: (b, jnp.maximum(i * nh - 1, 0), 0)),
            pl.BlockSpec((None, tm, d), lambda b, i, j: (b, i, 0)),
            pl.BlockSpec((None, HALO, d), lambda b, i, j: (b, jnp.minimum((i + 1) * nh, last_h), 0)),
            pl.BlockSpec((d, tf), lambda b, i, j: (0, j)),
            pl.BlockSpec((d, tf), lambda b, i, j: (0, j)),
            pl.BlockSpec((FFN_CONV_K, tf), lambda b, i, j: (0, j)),
        ],
        out_specs=pl.BlockSpec((None, tm, tf), lambda b, i, j: (b, i, j)),
        scratch_shapes=[pltpu.VMEM((tm + 2 * HALO, d), BF16),
                        pltpu.VMEM((tm + 2 * HALO, tf), F32)],
        compiler_params=_params(("arbitrary", "arbitrary", "arbitrary")),
        name="ffn_up",
    )(h2, h2, h2, w_gate, w_up, conv_w)

    tn = 512
    gate_blk = 5 * (d // tn)
    return pl.pallas_call(
        _ffn_down_kernel,
        out_shape=jax.ShapeDtypeStruct(x.shape, F32),
        grid=(bsz, length // tm, d // tn),
        in_specs=[
            pl.BlockSpec((None, tm, D_FF), lambda b, i, n: (b, i, 0)),
            pl.BlockSpec((D_FF, tn), lambda b, i, n: (0, n)),
            pl.BlockSpec((None, tm, tn), lambda b, i, n: (b, i, n)),
            pl.BlockSpec((None, 1, tn), lambda b, i, n: (b, 0, gate_blk + n)),
        ],
        out_specs=pl.BlockSpec((None, tm, tn), lambda b, i, n: (b, i, n)),
        compiler_params=_params(("arbitrary", "arbitrary", "arbitrary")),
        name="ffn_down",
    )(act, w_down, x, mod3)


def _final_norm_kernel(x_ref, g_ref, o_ref):
    o_ref[...] = _plain_rmsnorm(x_ref[...], g_ref[...])


def _final_norm_call(x, g):
    bsz, length, d = x.shape
    tm = min(1024, length)
    return pl.pallas_call(
        _final_norm_kernel,
        out_shape=jax.ShapeDtypeStruct(x.shape, F32),
        grid=(bsz, length // tm),
        in_specs=[pl.BlockSpec((None, tm, d), lambda b, i: (b, i, 0)),
                  pl.BlockSpec((1, d), lambda b, i: (0, 0))],
        out_specs=pl.BlockSpec((None, tm, d), lambda b, i: (b, i, 0)),
        compiler_params=_params(("arbitrary", "arbitrary")),
        name="final_norm",
    )(x, g.reshape(1, d))


NA_KEYS = NA_WIN_R * GRID_W
MASKED = -1e30
NA_ROWS_PER_STEP = 8


def _na_bias_table(rpb):
    qc = np.arange(GRID_W)[:, None]
    kc = np.arange(GRID_W)[None, :]
    cs = np.clip(qc - NA_WIN_C // 2, 0, GRID_W - NA_WIN_C)
    valid = (kc >= cs) & (kc < cs + NA_WIN_C)
    dc = np.clip(kc - qc + NA_WIN_C - 1, 0, 2 * NA_WIN_C - 2)
    dr = np.arange(NA_WIN_R)[None, :] - np.arange(NA_WIN_R)[:, None] + NA_WIN_R - 1
    t = rpb.astype(F32)[:, dr][:, :, :, dc]
    t = jnp.where(valid[None, None, None], t * LOG2_E, MASKED)
    return t.transpose(0, 1, 3, 2, 4).reshape(rpb.shape[0], NA_WIN_R, GRID_W, NA_KEYS)


def _na_kernel(q_ref, k_ref, v_ref, bias_ref, o_ref, kb_ref, vb_ref):
    rows = q_ref.shape[0] // GRID_W
    kb_ref[...] = k_ref[...].astype(BF16)
    vb_ref[...] = v_ref[...].astype(BF16)

    def row_group(g, carry):
        todo = []
        for u in range(NA_ROWS_PER_STEP):
            r = g * NA_ROWS_PER_STEP + u
            rs = jnp.clip(r - NA_WIN_R // 2, 0, rows - NA_WIN_R)
            q0 = pl.multiple_of(r * GRID_W, GRID_W)
            k0 = pl.multiple_of(rs * GRID_W, GRID_W)
            q = (q_ref[pl.ds(q0, GRID_W), :] * (HEAD_DIM ** -0.5 * LOG2_E)).astype(BF16)
            s = lax.dot_general(q, kb_ref[pl.ds(k0, NA_KEYS), :], (((1,), (1,)), ((), ())),
                                preferred_element_type=F32)
            todo.append((q0, k0, s, r - rs))
        for u, (q0, k0, s, off) in enumerate(todo):
            s = s + bias_ref[off]
            p = jnp.exp2(s - jnp.max(s, axis=-1, keepdims=True))
            todo[u] = (q0, k0, p.astype(BF16), jnp.sum(p, axis=-1, keepdims=True))
        for u, (q0, k0, p, l) in enumerate(todo):
            todo[u] = (q0, jnp.dot(p, vb_ref[pl.ds(k0, NA_KEYS), :], preferred_element_type=F32), l)
        for q0, o, l in todo:
            o_ref[pl.ds(q0, GRID_W), :] = (o / l).astype(o_ref.dtype)
        return carry

    lax.fori_loop(0, rows // NA_ROWS_PER_STEP, row_group, 0)


def _na_call(main, rpb):
    bsz, length, _ = main.shape
    assert length % (GRID_W * NA_ROWS_PER_STEP) == 0 and length // GRID_W >= NA_WIN_R
    hd = HEAD_DIM
    return pl.pallas_call(
        _na_kernel,
        out_shape=jax.ShapeDtypeStruct((bsz, length, GROUP_W), BF16),
        grid=(bsz, H_GROUP),
        in_specs=[
            pl.BlockSpec((None, length, hd), lambda b, h: (b, 0, h)),
            pl.BlockSpec((None, length, hd), lambda b, h: (b, 0, H_GROUP + h)),
            pl.BlockSpec((None, length, hd), lambda b, h: (b, 0, 2 * H_GROUP + h)),
            pl.BlockSpec((None, NA_WIN_R, GRID_W, NA_KEYS), lambda b, h: (h, 0, 0, 0)),
        ],
        out_specs=pl.BlockSpec((None, length, hd), lambda b, h: (b, 0, h)),
        scratch_shapes=[pltpu.VMEM((length, hd), BF16), pltpu.VMEM((length, hd), BF16)],
        compiler_params=_params(("arbitrary", "arbitrary")),
        name="nbr_attn",
    )(main, main, main, _na_bias_table(rpb))


REC_TILE = 512
REC_HEADS = 8
HIGHEST = lax.Precision.HIGHEST


def _tri(n, upper):
    r = lax.broadcasted_iota(jnp.int32, (n, n), 0)
    c = lax.broadcasted_iota(jnp.int32, (n, n), 1)
    return (c >= r) if upper else (c <= r)


def _gdn_conv_kernel(x_ref, w_ref, o_ref, xp_ref):
    length, cw = x_ref.shape
    col0 = pl.program_id(1) * cw
    pad = 8
    xp_ref[0:pad, :] = jnp.zeros((pad, cw), F32)
    xp_ref[pad + length:, :] = jnp.zeros((pad, cw), F32)
    xp_ref[pad:pad + length, :] = x_ref[...]
    w = w_ref[...]
    y = xp_ref[pl.ds(pad - DN_CONV_K // 2, length), :] * w[0:1, :]
    for t in range(1, DN_CONV_K):
        y = y + xp_ref[pl.ds(pad - DN_CONV_K // 2 + t, length), :] * w[t:t + 1, :]
    y = y * jax.nn.sigmoid(y)
    qscale = jnp.where(col0 < GROUP_W, HEAD_DIM ** -0.5, 1.0)
    for h in range(cw // HEAD_DIM):
        yh = y[:, h * HEAD_DIM:(h + 1) * HEAD_DIM]
        nh = yh * (lax.rsqrt(jnp.sum(yh * yh, axis=-1, keepdims=True) + EPS) * qscale)
        o_ref[:, h * HEAD_DIM:(h + 1) * HEAD_DIM] = jnp.where(col0 < 2 * GROUP_W, nh, yh)


def _gdn_conv_call(main, conv_w):
    bsz, length, _ = main.shape
    cw = HEAD_DIM
    nblk = 3 * GROUP_W // cw
    return pl.pallas_call(
        _gdn_conv_kernel,
        out_shape=jax.ShapeDtypeStruct((bsz, length, 3 * GROUP_W), F32),
        grid=(bsz, nblk),
        in_specs=[pl.BlockSpec((None, length, cw), lambda b, j: (b, 0, nblk + j)),
                  pl.BlockSpec((DN_CONV_K, cw), lambda b, j: (0, j))],
        out_specs=pl.BlockSpec((None, length, cw), lambda b, j: (b, 0, j)),
        scratch_shapes=[pltpu.VMEM((length + 16, cw), F32)],
        compiler_params=_params(("arbitrary", "arbitrary")),
        name="gdn_conv",
    )(main, conv_w)


def _gdn_gates_kernel(s_ref, alog_ref, dt_ref, g_ref, gt_ref):
    tm = s_ref.shape[0]
    s = s_ref[...]
    x = s + dt_ref[...]
    softplus = jnp.maximum(x, 0.0) + jnp.log1p(jnp.exp(-jnp.abs(x)))
    g = -jnp.exp(alog_ref[...]) * softplus
    r = lax.broadcasted_iota(jnp.int32, (tm, tm), 0)
    c = lax.broadcasted_iota(jnp.int32, (tm, tm), 1)
    same = (r // DN_CHUNK) == (c // DN_CHUNK)
    lo = jnp.where(same & (c <= r), 1.0, 0.0)
    up = jnp.where(same & (c >= r), 1.0, 0.0)
    gcf = jnp.dot(lo, g, precision=HIGHEST, preferred_element_type=F32)
    gcb = jnp.dot(up, g, precision=HIGHEST, preferred_element_type=F32)
    lane = lax.broadcasted_iota(jnp.int32, s.shape, 1)
    out = jnp.where(lane < H_GROUP, gcf, jnp.where(lane < 2 * H_GROUP, gcb, jax.nn.sigmoid(s)))
    g_ref[...] = out
    for p in range(tm // 128):
        gt_ref[p] = out[p * 128:(p + 1) * 128, :].T[0:4 * H_GROUP, :]


def _gdn_gates_call(side, a_log, dt_bias):
    bsz, length, _ = side.shape
    tm = min(512, length)
    row = lambda t: jnp.pad(t.reshape(1, 2 * H_GROUP).astype(F32), ((0, 0), (0, SIDE_W - 2 * H_GROUP)))
    return pl.pallas_call(
        _gdn_gates_kernel,
        out_shape=(jax.ShapeDtypeStruct((bsz, length, SIDE_W), F32),
                   jax.ShapeDtypeStruct((bsz, length // 128, 4 * H_GROUP, 128), F32)),
        grid=(bsz, length // tm),
        in_specs=[pl.BlockSpec((None, tm, SIDE_W), lambda b, i: (b, i, 0)),
                  pl.BlockSpec((1, SIDE_W), lambda b, i: (0, 0)),
                  pl.BlockSpec((1, SIDE_W), lambda b, i: (0, 0))],
        out_specs=(pl.BlockSpec((None, tm, SIDE_W), lambda b, i: (b, i, 0)),
                   pl.BlockSpec((None, tm // 128, 4 * H_GROUP, 128), lambda b, i: (b, i, 0, 0))),
        compiler_params=_params(("arbitrary", "arbitrary")),
        name="gdn_gates",
    )(side, row(a_log), row(dt_bias))


def _bdot(a, b):
    return jnp.dot(a.astype(BF16), b.astype(BF16), preferred_element_type=F32)


def _gdn_chunks_local(chunks):
    c = DN_CHUNK
    r = lax.broadcasted_iota(jnp.int32, (c, c), 0)
    cc = lax.broadcasted_iota(jnp.int32, (c, c), 1)
    for d in chunks:
        incl = (cc >= r) if d["rev"] else (cc <= r)
        d["gam"] = jnp.exp(jnp.where(incl, d["gci"] - d["gcj"], MASKED))
        d["kb"] = d["k"] * d["beta"]
    for d in chunks:
        kq = jnp.concatenate([d["kb"], d["q"]], axis=0).astype(BF16)
        d["prod"] = lax.dot_general(kq, d["k"].astype(BF16), (((1,), (1,)), ((), ())),
                                    preferred_element_type=F32)
    for d in chunks:
        strict = (cc > r) if d["rev"] else (cc < r)
        d["aqk"] = d["prod"][c:] * d["gam"]
        d["p"] = -jnp.where(strict, d["prod"][0:c] * d["gam"], 0.0)
        d["t"] = jnp.where(r == cc, 1.0, 0.0) + d["p"]
    for _ in range(int(math.log2(c)) - 1):
        for d in chunks:
            d["p"] = _bdot(d["p"], d["p"])
        for d in chunks:
            d["t"] = d["t"] + _bdot(d["t"], d["p"])
    for d in chunks:
        egc = jnp.exp(d["gci"])
        rhs = jnp.concatenate([d["v"] * d["beta"], d["kb"] * egc], axis=1)
        uw = _bdot(d["t"], rhs)
        d["u"], d["w"] = uw[:, 0:HEAD_DIM], uw[:, HEAD_DIM:]
        glast = d["gci"][0:1, :] if d["rev"] else d["gci"][c - 1:c, :]
        d["qd"] = d["q"] * egc
        d["kd"] = d["k"] * jnp.exp(glast - d["gci"])
        d["dlast"] = jnp.exp(glast)


def _gdn_chunks_scan(chunks):
    c = DN_CHUNK
    for d in chunks:
        d["s"] = d["s_ref"][...]
        d["ws"] = _bdot(jnp.concatenate([d["w"], d["qd"]], axis=0), d["s"])
    for d in chunks:
        d["vn"] = (d["u"] - d["ws"][0:c]).astype(BF16)
        d["o"] = d["ws"][c:] + _bdot(d["aqk"], d["vn"])
    for d in chunks:
        d["s_ref"][...] = d["s"] * d["dlast"] + lax.dot_general(
            d["kd"].astype(BF16), d["vn"], (((0,), (0,)), ((), ())), preferred_element_type=F32)


def _gdn_kernel(qf_ref, kf_ref, vf_ref, gf_ref, gtf_ref, qb_ref, kb_ref, vb_ref, gb_ref, gtb_ref,
                of_ref, ob_ref, s_ref):
    hg = pl.program_id(1)

    @pl.when(pl.program_id(2) == 0)
    def _():
        s_ref[...] = jnp.zeros_like(s_ref)

    npair = qf_ref.shape[0] // 128
    c = DN_CHUNK

    def pair(p, carry):
        chunks = []
        for rev in (False, True):
            q_ref, k_ref, v_ref, g_ref, gt_ref, o_ref = (
                (qb_ref, kb_ref, vb_ref, gb_ref, gtb_ref, ob_ref) if rev
                else (qf_ref, kf_ref, vf_ref, gf_ref, gtf_ref, of_ref))
            pp = (npair - 1 - p) if rev else p
            base = pl.multiple_of(pp * 128, 128)
            gt = gt_ref[pp]
            sub = lax.broadcasted_iota(jnp.int32, gt.shape, 0)
            for step, half in enumerate((1, 0) if rev else (0, 1)):
                rows = pl.ds(base + half * c, c)
                g = g_ref[rows, :]
                lane = lax.broadcasted_iota(jnp.int32, g.shape, 1)
                for hl in range(REC_HEADS):
                    cols = slice(hl * HEAD_DIM, (hl + 1) * HEAD_DIM)
                    gcol = hg * REC_HEADS + hl + (H_GROUP if rev else 0)
                    chunks.append(dict(
                        rev=rev, step=step, rows=rows, cols=cols, o_ref=o_ref,
                        s_ref=s_ref.at[int(rev), hl],
                        q=q_ref[rows, cols], k=k_ref[rows, cols], v=v_ref[rows, cols],
                        gci=jnp.sum(jnp.where(lane == gcol, g, 0.0), axis=-1, keepdims=True),
                        beta=jnp.sum(jnp.where(lane == gcol + 2 * H_GROUP, g, 0.0), axis=-1,
                                     keepdims=True),
                        gcj=jnp.sum(jnp.where(sub == gcol, gt, 0.0), axis=0,
                                    keepdims=True)[:, half * c:(half + 1) * c]))
        _gdn_chunks_local(chunks)
        for step in (0, 1):
            now = [d for d in chunks if d["step"] == step]
            _gdn_chunks_scan(now)
            for d in now:
                d["o_ref"][d["rows"], d["cols"]] = d["o"]
        return carry

    lax.fori_loop(0, npair, pair, 0)


def _gdn_call(main, side, conv_w, a_log, dt_bias):
    bsz, length, _ = main.shape
    tl = min(REC_TILE, length)
    nt = length // tl
    w = REC_HEADS * HEAD_DIM
    ng = H_GROUP // REC_HEADS
    qkv = _gdn_conv_call(main, conv_w)
    gates, gates_t = _gdn_gates_call(side, a_log, dt_bias)

    def specs(tile):
        return [pl.BlockSpec((None, tl, w), lambda b, g, i: (b, tile(i), g)),
                pl.BlockSpec((None, tl, w), lambda b, g, i: (b, tile(i), ng + g)),
                pl.BlockSpec((None, tl, w), lambda b, g, i: (b, tile(i), 2 * ng + g)),
                pl.BlockSpec((None, tl, SIDE_W), lambda b, g, i: (b, tile(i), 0)),
                pl.BlockSpec((None, tl // 128, 4 * H_GROUP, 128), lambda b, g, i: (b, tile(i), 0, 0))]

    fw = lambda i: i
    bw = lambda i: nt - 1 - i
    return pl.pallas_call(
        _gdn_kernel,
        out_shape=(jax.ShapeDtypeStruct((bsz, length, GROUP_W), F32),) * 2,
        grid=(bsz, ng, nt),
        in_specs=specs(fw) + specs(bw),
        out_specs=(pl.BlockSpec((None, tl, w), lambda b, g, i: (b, fw(i), g)),
                   pl.BlockSpec((None, tl, w), lambda b, g, i: (b, bw(i), g))),
        scratch_shapes=[pltpu.VMEM((2, REC_HEADS, HEAD_DIM, HEAD_DIM), F32)],
        compiler_params=_params(("arbitrary", "arbitrary", "arbitrary")),
        name="gdn_scan",
    )(qkv, qkv, qkv, gates, gates_t, qkv, qkv, qkv, gates, gates_t)


HG_TILE = 64
HG_SUB = 8
HG_NSUB = HG_TILE // HG_SUB


def _tri_cumsum(x, rev):
    n = x.shape[0]
    tri = jnp.where(_tri(n, rev), 1.0, 0.0).astype(BF16)
    hi = x.astype(BF16)
    r1 = x - hi.astype(F32)
    mid = r1.astype(BF16)
    lo = (r1 - mid.astype(F32)).astype(BF16)
    parts = jnp.dot(tri, jnp.concatenate([hi, mid, lo], axis=1), preferred_element_type=F32)
    w = x.shape[1]
    return (parts[:, 0:w] + parts[:, w:2 * w]) + parts[:, 2 * w:]


def _hg_chunks(chains):
    n, sub, nsub = HG_TILE, HG_SUB, HG_NSUB
    row = lax.broadcasted_iota(jnp.int32, (n, HEAD_DIM), 0)
    pos = row % sub
    tn_dims = (((0,), (0,)), ((), ()))
    for c in chains:
        z, loglb, log1mlb = c["z"], c["loglb"], c["log1mlb"]
        c["q"] = c["q_raw"] * jax.nn.sigmoid(c["q_raw"])
        log_sig = jnp.minimum(z, 0.0) - jnp.log1p(jnp.exp(-jnp.abs(z)))
        bb = log1mlb + log_sig
        lf = jnp.maximum(loglb, bb) + jnp.log1p(jnp.exp(-jnp.abs(loglb - bb)))
        b = _tri_cumsum(lf, c["rev"])
        c["b2"] = b * LOG2_E
        c["c2"] = (b - (bb - z)) * LOG2_E
        c["inp16"] = c["inp"].astype(BF16)

    for c in chains:
        b2, c2, rev = c["b2"], c["c2"], c["rev"]
        c["order"] = list(range(nsub - 1, -1, -1)) if rev else list(range(nsub))
        c["ends"], c["contrib"] = [], []
        for j in c["order"]:
            lo = j * sub
            bend = b2[lo:lo + 1, :] if rev else b2[lo + sub - 1:lo + sub, :]
            c["ends"].append(bend)
            if len(c["contrib"]) < nsub - 1:
                ks = jnp.exp2(bend - c2[lo:lo + sub, :])
                c["contrib"].append(lax.dot_general(c["inp16"][lo:lo + sub, :], ks.astype(BF16),
                                                    tn_dims, preferred_element_type=F32))

    for c in chains:
        b2, c2, q, inp, rev = c["b2"], c["c2"], c["q"], c["inp"], c["rev"]
        o = jnp.sum(q * jnp.exp2(b2 - c2), axis=-1, keepdims=True) * inp
        for d in range(1, sub):
            shift = (sub - d) % sub if rev else d
            valid = (pos <= sub - 1 - d) if rev else (pos >= d)
            sh = lambda x: pltpu.roll(x.reshape(nsub, sub, HEAD_DIM), shift, axis=1).reshape(
                n, HEAD_DIM)
            e = jnp.where(valid, jnp.exp2(b2 - sh(c2)), 0.0)
            o = o + jnp.sum(q * e, axis=-1, keepdims=True) * sh(inp)
        c["o"] = o

    for c in chains:
        b2, q, rev = c["b2"], c["q"], c["rev"]
        c["s"] = c["s_ref"][...]
        qext = [q * jnp.exp2(b2)]
        for m in range(nsub - 1):
            j = c["order"][m]
            lo, hi = (0, j * sub) if rev else ((j + 1) * sub, n)
            part = q[lo:hi] * jnp.exp2(b2[lo:hi] - c["ends"][m])
            zeros = jnp.zeros((n - (hi - lo), HEAD_DIM), F32)
            qext.append(jnp.concatenate([part, zeros] if rev else [zeros, part], axis=0))
        ncat = jnp.concatenate([c["s"]] + c["contrib"], axis=1).astype(BF16)
        c["o"] = c["o"] + lax.dot_general(jnp.concatenate(qext, axis=1).astype(BF16), ncat,
                                          (((1,), (1,)), ((), ())), preferred_element_type=F32)
    for c in chains:
        btot = c["ends"][-1]
        kd = jnp.exp2(btot - c["c2"]).astype(BF16)
        c["s_ref"][...] = c["s"] * jnp.exp2(btot) + lax.dot_general(
            c["inp16"], kd, tn_dims, preferred_element_type=F32)


def _hgrn2_kernel(qf_ref, ff_ref, if_ref, qb_ref, fb_ref, ib_ref, lbf_ref, lbb_ref,
                  of_ref, ob_ref, s_ref):
    @pl.when(pl.program_id(2) == 0)
    def _():
        s_ref[...] = jnp.zeros_like(s_ref)

    nstep = qf_ref.shape[0] // HG_TILE

    def step(t, carry):
        chains = []
        for rev in (False, True):
            q_ref, f_ref, i_ref, lb_ref, o_ref = ((qb_ref, fb_ref, ib_ref, lbb_ref, ob_ref) if rev
                                                  else (qf_ref, ff_ref, if_ref, lbf_ref, of_ref))
            tt = (nstep - 1 - t) if rev else t
            rows = pl.ds(pl.multiple_of(tt * HG_TILE, HG_TILE), HG_TILE)
            for hl in range(REC_HEADS):
                cols = slice(hl * HEAD_DIM, (hl + 1) * HEAD_DIM)
                chains.append(dict(rev=rev, rows=rows, cols=cols, o_ref=o_ref,
                                   s_ref=s_ref.at[int(rev), hl], q_raw=q_ref[rows, cols],
                                   z=f_ref[rows, cols], inp=i_ref[rows, cols],
                                   loglb=lb_ref[0:1, cols], log1mlb=lb_ref[1:2, cols]))
        _hg_chunks(chains)
        for c in chains:
            c["o_ref"][c["rows"], c["cols"]] = c["o"]
        return carry

    lax.fori_loop(0, nstep, step, 0)


def _hgrn2_call(main, lb_fw, lb_bw):
    bsz, length, _ = main.shape
    tl = min(REC_TILE, length)
    nt = length // tl
    w = REC_HEADS * HEAD_DIM
    ng = H_GROUP // REC_HEADS
    fw = lambda i: i
    bw = lambda i: nt - 1 - i
    col = lambda group, tile: pl.BlockSpec((None, tl, w), lambda b, g, i: (b, tile(i), group * ng + g))
    lbrow = pl.BlockSpec((2, w), lambda b, g, i: (0, g))
    logs = lambda lb: jnp.stack([jnp.log(lb), jnp.log1p(-lb)])
    return pl.pallas_call(
        _hgrn2_kernel,
        out_shape=(jax.ShapeDtypeStruct((bsz, length, GROUP_W), F32),) * 2,
        grid=(bsz, ng, nt),
        in_specs=[col(0, fw), col(1, fw), col(3, fw), col(0, bw), col(2, bw), col(3, bw), lbrow, lbrow],
        out_specs=(pl.BlockSpec((None, tl, w), lambda b, g, i: (b, fw(i), g)),
                   pl.BlockSpec((None, tl, w), lambda b, g, i: (b, bw(i), g))),
        scratch_shapes=[pltpu.VMEM((2, REC_HEADS, HEAD_DIM, HEAD_DIM), F32)],
        compiler_params=_params(("arbitrary", "arbitrary", "arbitrary")),
        name="hgrn2_scan",
    )(main, main, main, main, main, main, logs(lb_fw), logs(lb_bw))


MLA_QK = 2 * MLA_NOPE
MLA_KEY_BLOCK = 512
ROPE_SWAP = np.concatenate([np.arange(MLA_ROPE // 2, MLA_ROPE), np.arange(MLA_ROPE // 2)])


def _rope_table(length):
    half = MLA_ROPE // 2
    inv = ROPE_BASE ** (-jnp.arange(half, dtype=F32) / half)
    ang = jnp.arange(length, dtype=F32)[:, None] * inv[None, :]
    cos, sin = jnp.cos(ang), jnp.sin(ang)
    return jnp.concatenate([cos, cos, -sin, sin], axis=-1)


def _mla_q_weights(w_uq):
    w = w_uq.reshape(MLA_Q_LORA, H_GROUP, MLA_NOPE + MLA_ROPE)
    rope = w[:, :, MLA_NOPE:]
    w = jnp.concatenate([w[:, :, :MLA_NOPE], rope, rope[:, :, ROPE_SWAP]], axis=-1)
    return w.reshape(MLA_Q_LORA, H_GROUP * MLA_QK).astype(BF16)


def _plain_rmsnorm(x, g):
    return x * lax.rsqrt(jnp.mean(x * x, axis=-1, keepdims=True) + EPS) * g


def _mla_proj_kernel(cq_ref, ckv_ref, side_ref, tab_ref, qn_ref, kvn_ref, wq_ref, wkv_ref,
                     q_out, k_out, v_out):
    scale = (MLA_NOPE + MLA_ROPE) ** -0.5 * LOG2_E
    cqn = _plain_rmsnorm(cq_ref[...], qn_ref[...]).astype(BF16)
    ckvn = _plain_rmsnorm(ckv_ref[...], kvn_ref[...]).astype(BF16)
    q = jnp.dot(cqn, wq_ref[...], preferred_element_type=F32)
    kv = jnp.dot(ckvn, wkv_ref[...], preferred_element_type=F32)
    tab = tab_ref[...]
    kr = (side_ref[...] * tab).astype(BF16)
    for h in range(H_GROUP):
        c0 = h * MLA_QK
        qr = q[:, c0 + MLA_NOPE:c0 + MLA_QK] * tab
        qr = qr + pltpu.roll(qr, MLA_ROPE, axis=1)
        q_out[h, :, 0:MLA_NOPE] = (q[:, c0:c0 + MLA_NOPE] * scale).astype(BF16)
        q_out[h, :, MLA_NOPE:MLA_QK] = (qr * scale).astype(BF16)
        k_out[h, :, 0:MLA_NOPE] = kv[:, c0:c0 + MLA_NOPE].astype(BF16)
        k_out[h, :, MLA_NOPE:MLA_QK] = kr
        v_out[h] = kv[:, c0 + MLA_NOPE:c0 + MLA_QK].astype(BF16)


def _mla_attn_kernel(q_ref, k_ref, v_ref, o_ref):
    length = k_ref.shape[0]
    tk = MLA_KEY_BLOCK
    nk = length // tk
    q = q_ref[...]

    def scores(kb):
        return lax.dot_general(q, k_ref[kb * tk:(kb + 1) * tk, :], (((1,), (1,)), ((), ())),
                               preferred_element_type=F32)

    s_next = scores(0)
    m = l = acc = None
    for kb in range(nk):
        s = s_next
        if kb + 1 < nk:
            s_next = scores(kb + 1)
        m_blk = jnp.max(s, axis=-1, keepdims=True)
        if kb == 0:
            m = m_blk
            p = jnp.exp2(s - m)
            l = jnp.sum(p, axis=-1, keepdims=True)
            acc = jnp.dot(p.astype(BF16), v_ref[0:tk, :], preferred_element_type=F32)
        else:
            m_new = jnp.maximum(m, m_blk)
            alpha = jnp.exp2(m - m_new)
            p = jnp.exp2(s - m_new)
            l = alpha * l + jnp.sum(p, axis=-1, keepdims=True)
            acc = alpha * acc + jnp.dot(p.astype(BF16), v_ref[kb * tk:(kb + 1) * tk, :],
                                        preferred_element_type=F32)
            m = m_new
    o_ref[...] = (acc / l).astype(o_ref.dtype)


def _mla_call(main, side, q_norm, w_uq, kv_norm, w_ukv):
    bsz, length, _ = main.shape
    tm = min(512, length)
    lora = MLA_Q_LORA
    q_all, k_all, v_all = pl.pallas_call(
        _mla_proj_kernel,
        out_shape=(jax.ShapeDtypeStruct((bsz, H_GROUP, length, MLA_QK), BF16),
                   jax.ShapeDtypeStruct((bsz, H_GROUP, length, MLA_QK), BF16),
                   jax.ShapeDtypeStruct((bsz, H_GROUP, length, MLA_V), BF16)),
        grid=(bsz, length // tm),
        in_specs=[
            pl.BlockSpec((None, tm, lora), lambda b, i: (b, i, 10)),
            pl.BlockSpec((None, tm, lora), lambda b, i: (b, i, 11)),
            pl.BlockSpec((None, tm, SIDE_W), lambda b, i: (b, i, 0)),
            pl.BlockSpec((tm, SIDE_W), lambda b, i: (i, 0)),
            pl.BlockSpec((1, lora), lambda b, i: (0, 0)),
            pl.BlockSpec((1, lora), lambda b, i: (0, 0)),
            pl.BlockSpec((lora, H_GROUP * MLA_QK), lambda b, i: (0, 0)),
            pl.BlockSpec((lora, H_GROUP * MLA_QK), lambda b, i: (0, 0)),
        ],
        out_specs=(pl.BlockSpec((None, H_GROUP, tm, MLA_QK), lambda b, i: (b, 0, i, 0)),
                   pl.BlockSpec((None, H_GROUP, tm, MLA_QK), lambda b, i: (b, 0, i, 0)),
                   pl.BlockSpec((None, H_GROUP, tm, MLA_V), lambda b, i: (b, 0, i, 0))),
        compiler_params=_params(("arbitrary", "arbitrary")),
        name="mla_proj",
    )(main, main, side, _rope_table(length), q_norm.reshape(1, lora), kv_norm.reshape(1, lora),
      _mla_q_weights(w_uq), w_ukv.astype(BF16))

    tq = 512
    return pl.pallas_call(
        _mla_attn_kernel,
        out_shape=jax.ShapeDtypeStruct((bsz, length, H_GROUP * MLA_V), BF16),
        grid=(bsz, H_GROUP, length // tq),
        in_specs=[
            pl.BlockSpec((None, None, tq, MLA_QK), lambda b, h, i: (b, h, i, 0)),
            pl.BlockSpec((None, None, length, MLA_QK), lambda b, h, i: (b, h, 0, 0)),
            pl.BlockSpec((None, None, length, MLA_V), lambda b, h, i: (b, h, 0, 0)),
        ],
        out_specs=pl.BlockSpec((None, tq, MLA_V), lambda b, h, i: (b, i, h)),
        compiler_params=_params(("arbitrary", "arbitrary", "arbitrary")),
        name="mla_attn",
    )(q_all, k_all, v_all)


def _pad_side(w):
    return jnp.pad(w, ((0, 0), (0, SIDE_W - w.shape[1]))).astype(BF16)


def _trunk(x, mods, lb, p):
    for l in range(DEPTH):
        mod3 = mods[l]
        if l % 2 == 0:
            e = l // 2
            main, side = _inproj_call(x, mod3, p["norm_mix"][l], p["w_in_ab_main"][e],
                                      p["w_in_ab_side"][e])
            att = _na_call(main, p["na_rpb"][e])
            o_fw, o_bw = _gdn_call(main, side, p["dn_conv"][e], p["dn_a_log"][e], p["dn_dt_bias"][e])
            x, h2 = _outproj_call(att, o_fw, o_bw, main, 6, p["dn_norm"][e], False,
                                  p["w_out_ab"][e], x, mod3, p["norm_ffn"][l])
        else:
            o = l // 2
            main, side = _inproj_call(x, mod3, p["norm_mix"][l], p["w_in_cd_main"][o],
                                      p["w_in_cd_side"][o])
            o_fw, o_bw = _hgrn2_call(main, lb[0, l], lb[1, l])
            att = _mla_call(main, side, p["mla_q_norm"][o], p["mla_w_uq"][o], p["mla_kv_norm"][o],
                            p["mla_w_ukv"][o])
            x, h2 = _outproj_call(att, o_fw, o_bw, main, 4, p["hg_norm"][o], True,
                                  p["w_out_cd"][o], x, mod3, p["norm_ffn"][l])
        x = _ffn_split_call(x, h2, mod3, p["ffn_w_gate"][l], p["ffn_w_up"][l], p["ffn_conv"][l],
                            p["ffn_w_down"][l])
    return _final_norm_call(x, p["final_norm"])


def kernel(x_prompt, x_sample, c_prompt, c_sample, norm_mix, norm_ffn, w_ada, b_ada, w_in_ab, w_out_ab, na_rpb, dn_conv, dn_a_log, dn_dt_bias, dn_norm, w_in_cd, w_out_cd, hg_lower_bounds, hg_norm, mla_q_norm, mla_w_uq, mla_kv_norm, mla_w_ukv, ffn_w_gate, ffn_w_up, ffn_conv, ffn_w_down, final_norm):
    n_p, n_s = c_prompt.shape[0], c_sample.shape[0]
    rows = -(-(n_p + n_s) // 8) * 8
    c_all = jnp.concatenate([c_prompt, c_sample, jnp.zeros((rows - n_p - n_s, D_MODEL), F32)], axis=0)
    mod_all = _ada_call(c_all, w_ada, b_ada)
    mods_p = [mod_all[l, :n_p].reshape(n_p, 1, 6 * D_MODEL) for l in range(DEPTH)]
    mods_s = [mod_all[l, n_p:n_p + n_s].reshape(n_s, 1, 6 * D_MODEL) for l in range(DEPTH)]

    lb = jnp.cumsum(jax.nn.softmax(hg_lower_bounds.astype(F32), axis=1), axis=1)
    lb = lb - lb[:, :1]

    p = dict(
        norm_mix=norm_mix, norm_ffn=norm_ffn, na_rpb=na_rpb, dn_conv=dn_conv, dn_a_log=dn_a_log,
        dn_dt_bias=dn_dt_bias, dn_norm=dn_norm, hg_norm=hg_norm, mla_q_norm=mla_q_norm,
        mla_w_uq=mla_w_uq, mla_kv_norm=mla_kv_norm, mla_w_ukv=mla_w_ukv, ffn_conv=ffn_conv,
        final_norm=final_norm,
        w_in_ab_main=w_in_ab[:, :, :AB_MAIN].astype(BF16),
        w_in_ab_side=jax.vmap(_pad_side)(w_in_ab[:, :, AB_MAIN:]),
        w_in_cd_main=w_in_cd[:, :, :CD_MAIN].astype(BF16),
        w_in_cd_side=jnp.concatenate([w_in_cd[:, :, CD_MAIN:],
                                      w_in_cd[:, :, CD_MAIN:][:, :, ROPE_SWAP]], axis=-1).astype(BF16),
        w_out_ab=w_out_ab.astype(BF16), w_out_cd=w_out_cd.astype(BF16),
        ffn_w_gate=ffn_w_gate.astype(BF16), ffn_w_up=ffn_w_up.astype(BF16),
        ffn_w_down=ffn_w_down.astype(BF16),
    )
    y_prompt = _trunk(x_prompt, mods_p, lb, p)
    y_sample = _trunk(x_sample, mods_s, lb, p)
    return (y_prompt, y_sample)
```

```python
import functools
import math

import numpy as np
import jax
import jax.numpy as jnp
from jax import lax
from jax.experimental import pallas as pl
from jax.experimental.pallas import tpu as pltpu

D_MODEL = 2048
DEPTH = 4
HEAD_DIM = 128
H_GROUP = 8
GROUP_W = H_GROUP * HEAD_DIM
GRID_W = 64
NA_WIN_R = 8
NA_WIN_C = 16
DN_CONV_K = 5
DN_CHUNK = 64
MLA_Q_LORA = 512
MLA_KV_LORA = 512
MLA_NOPE = 128
MLA_ROPE = 64
MLA_V = 128
ROPE_BASE = 10000.0
D_FF = 5632
FFN_CONV_K = 3
EPS = 1e-6
AB_MAIN = 7 * GROUP_W
CD_MAIN = 6 * GROUP_W
SIDE_W = 128

F32 = jnp.float32
BF16 = jnp.bfloat16
LOG2_E = 1.4426950408889634

VMEM_LIMIT_BYTES = 56 * 1024 * 1024
HALO = 16
def _params(sem):
    return pltpu.CompilerParams(dimension_semantics=sem, vmem_limit_bytes=VMEM_LIMIT_BYTES)


def _ada_kernel(c_ref, w_ref, b_ref, o_ref):
    c = c_ref[...]
    cond = c * jax.nn.sigmoid(c)
    o_ref[...] = jnp.dot(cond, w_ref[...], precision=lax.Precision.HIGHEST,
                         preferred_element_type=F32) + b_ref[...]


def _ada_call(c_all, w_ada, b_ada):
    rows = c_all.shape[0]
    tn = 1024
    return pl.pallas_call(
        _ada_kernel,
        out_shape=jax.ShapeDtypeStruct((DEPTH, rows, 6 * D_MODEL), F32),
        grid=(DEPTH, 6 * D_MODEL // tn),
        in_specs=[
            pl.BlockSpec((rows, D_MODEL), lambda l, j: (0, 0)),
            pl.BlockSpec((None, D_MODEL, tn), lambda l, j: (l, 0, j)),
            pl.BlockSpec((None, 1, tn), lambda l, j: (l, 0, j)),
        ],
        out_specs=pl.BlockSpec((None, rows, tn), lambda l, j: (l, 0, j)),
        compiler_params=_params(("arbitrary", "arbitrary")),
        name="ada_mod",
    )(c_all, w_ada, b_ada.reshape(DEPTH, 1, 6 * D_MODEL))


def _norm_mod(x, g, sh, sc):
    y = x * lax.rsqrt(jnp.mean(x * x, axis=-1, keepdims=True) + EPS)
    return (y * g) * (1.0 + sc) + sh


def _inproj_kernel(x_ref, g_ref, sh_ref, sc_ref, w_ref, ws_ref, o_ref, os_ref, h_ref):
    @pl.when(pl.program_id(2) == 0)
    def _():
        hb = _norm_mod(x_ref[...], g_ref[...], sh_ref[...], sc_ref[...]).astype(BF16)
        h_ref[...] = hb
        os_ref[...] = jnp.dot(hb, ws_ref[...], preferred_element_type=F32)

    o_ref[...] = jnp.dot(h_ref[...], w_ref[...], preferred_element_type=F32)


def _inproj_call(x, mod3, norm_g, w_main, w_side):
    bsz, length, d = x.shape
    n_main = w_main.shape[1]
    tm = min(1024, length)
    tn = 1024
    return pl.pallas_call(
        _inproj_kernel,
        out_shape=(jax.ShapeDtypeStruct((bsz, length, n_main), F32),
                   jax.ShapeDtypeStruct((bsz, length, SIDE_W), F32)),
        grid=(bsz, length // tm, n_main // tn),
        in_specs=[
            pl.BlockSpec((None, tm, d), lambda b, i, j: (b, i, 0)),
            pl.BlockSpec((1, d), lambda b, i, j: (0, 0)),
            pl.BlockSpec((None, 1, d), lambda b, i, j: (b, 0, 0)),
            pl.BlockSpec((None, 1, d), lambda b, i, j: (b, 0, 1)),
            pl.BlockSpec((d, tn), lambda b, i, j: (0, j)),
            pl.BlockSpec((d, SIDE_W), lambda b, i, j: (0, 0)),
        ],
        out_specs=(pl.BlockSpec((None, tm, tn), lambda b, i, j: (b, i, j)),
                   pl.BlockSpec((None, tm, SIDE_W), lambda b, i, j: (b, i, 0))),
        scratch_shapes=[pltpu.VMEM((tm, d), BF16)],
        compiler_params=_params(("arbitrary", "arbitrary", "arbitrary")),
        name="inproj",
    )(x, norm_g.reshape(1, d), mod3, mod3, w_main, w_side)


def _outproj_kernel(rec_first, att_ref, of_ref, ob_ref, z_ref, ng_ref, w_ref, x_ref, gate_ref,
                    g2_ref, sh2_ref, sc2_ref, o_ref, h2_ref, y_ref):
    half = att_ref.shape[-1]
    ng = ng_ref[...]
    for h in range(H_GROUP):
        sl = slice(h * HEAD_DIM, (h + 1) * HEAD_DIM)
        o = of_ref[:, sl] + ob_ref[:, sl]
        o = o * lax.rsqrt(jnp.mean(o * o, axis=-1, keepdims=True) + EPS) * ng
        z = z_ref[:, sl]
        y_ref[:, sl] = (o * (z * jax.nn.sigmoid(z))).astype(BF16)
    rec0, att0 = (0, half) if rec_first else (half, 0)
    y = jnp.dot(y_ref[...], w_ref[rec0:rec0 + half, :], preferred_element_type=F32)
    y = y + jnp.dot(att_ref[...], w_ref[att0:att0 + half, :], preferred_element_type=F32)
    xo = x_ref[...] + gate_ref[...] * y
    o_ref[...] = xo
    h2_ref[...] = _norm_mod(xo, g2_ref[...], sh2_ref[...], sc2_ref[...]).astype(BF16)


def _outproj_call(att, o_fw, o_bw, main, gate_group, norm_g, rec_first, w_out, x, mod3, norm_ffn_g):
    bsz, length, d = x.shape
    half = att.shape[-1]
    tm = min(512, length)
    blk = lambda col: pl.BlockSpec((None, tm, half), lambda b, i: (b, i, col))
    row = pl.BlockSpec((None, tm, d), lambda b, i: (b, i, 0))
    mod = lambda k: pl.BlockSpec((None, 1, d), lambda b, i: (b, 0, k))
    return pl.pallas_call(
        functools.partial(_outproj_kernel, rec_first),
        out_shape=(jax.ShapeDtypeStruct(x.shape, F32), jax.ShapeDtypeStruct(x.shape, BF16)),
        grid=(bsz, length // tm),
        in_specs=[
            blk(0), blk(0), blk(0), blk(gate_group),
            pl.BlockSpec((1, HEAD_DIM), lambda b, i: (0, 0)),
            pl.BlockSpec((2 * half, d), lambda b, i: (0, 0)),
            row,
            mod(2),
            pl.BlockSpec((1, d), lambda b, i: (0, 0)),
            mod(3), mod(4),
        ],
        out_specs=(row, row),
        scratch_shapes=[pltpu.VMEM((tm, half), BF16)],
        compiler_params=_params(("arbitrary", "arbitrary")),
        name="outproj",
    )(att, o_fw, o_bw, main, norm_g.reshape(1, HEAD_DIM), w_out, x, mod3,
      norm_ffn_g.reshape(1, d), mod3, mod3)


def _ffn_up_kernel(hp_ref, hm_ref, hn_ref, wg_ref, wu_ref, cw_ref, act_ref, h_ref, gp_ref):
    i = pl.program_id(1)
    tm = hm_ref.shape[0]

    @pl.when(pl.program_id(2) == 0)
    def _():
        h_ref[HALO:HALO + tm, :] = hm_ref[...]
        h_ref[0:HALO, :] = jnp.where(i == 0, 0.0, hp_ref[...]).astype(BF16)
        h_ref[HALO + tm:, :] = jnp.where(i == pl.num_programs(1) - 1, 0.0, hn_ref[...]).astype(BF16)

    gp_ref[...] = jnp.dot(h_ref[...], wg_ref[...], preferred_element_type=F32)
    u = jnp.dot(hm_ref[...], wu_ref[...], preferred_element_type=F32)
    cw = cw_ref[...]
    a = gp_ref[pl.ds(HALO - 1, tm), :] * cw[0:1, :]
    a = a + gp_ref[pl.ds(HALO, tm), :] * cw[1:2, :]
    a = a + gp_ref[pl.ds(HALO + 1, tm), :] * cw[2:3, :]
    act_ref[...] = ((a * jax.nn.sigmoid(a)) * u).astype(BF16)


def _ffn_down_kernel(act_ref, wd_ref, x_ref, gate_ref, o_ref):
    o_ref[...] = x_ref[...] + gate_ref[...] * jnp.dot(act_ref[...], wd_ref[...],
                                                     preferred_element_type=F32)


def _ffn_split_call(x, h2, mod3, w_gate, w_up, conv_w, w_down):
    bsz, length, d = x.shape
    tm = min(1024, length)
    tf = 512
    nh = tm // HALO
    last_h = length // HALO - 1
    act = pl.pallas_call(
        _ffn_up_kernel,
        out_shape=jax.ShapeDtypeStruct((bsz, length, D_FF), BF16),
        grid=(bsz, length // tm, D_FF // tf),
        in_specs=[
            pl.BlockSpec((None, HALO, d), lambda b, i, j: (b, jnp.maximum(i * nh - 1, 0), 0)),
            pl.BlockSpec((None, tm, d), lambda b, i, j: (b, i, 0)),
            pl.BlockSpec((None, HALO, d), lambda b, i, j: (b, jnp.minimum((i + 1) * nh, last_h), 0)),
            pl.BlockSpec((d, tf), lambda b, i, j: (0, j)),
            pl.BlockSpec((d, tf), lambda b, i, j: (0, j)),
            pl.BlockSpec((FFN_CONV_K, tf), lambda b, i, j: (0, j)),
        ],
        out_specs=pl.BlockSpec((None, tm, tf), lambda b, i, j: (b, i, j)),
        scratch_shapes=[pltpu.VMEM((tm + 2 * HALO, d), BF16),
                        pltpu.VMEM((tm + 2 * HALO, tf), F32)],
        compiler_params=_params(("arbitrary", "arbitrary", "arbitrary")),
        name="ffn_up",
    )(h2, h2, h2, w_gate, w_up, conv_w)

    tn = 512
    gate_blk = 5 * (d // tn)
    return pl.pallas_call(
        _ffn_down_kernel,
        out_shape=jax.ShapeDtypeStruct(x.shape, F32),
        grid=(bsz, length // tm, d // tn),
        in_specs=[
            pl.BlockSpec((None, tm, D_FF), lambda b, i, n: (b, i, 0)),
            pl.BlockSpec((D_FF, tn), lambda b, i, n: (0, n)),
            pl.BlockSpec((None, tm, tn), lambda b, i, n: (b, i, n)),
            pl.BlockSpec((None, 1, tn), lambda b, i, n: (b, 0, gate_blk + n)),
        ],
        out_specs=pl.BlockSpec((None, tm, tn), lambda b, i, n: (b, i, n)),
        compiler_params=_params(("arbitrary", "arbitrary", "arbitrary")),
        name="ffn_down",
    )(act, w_down, x, mod3)


def _final_norm_kernel(x_ref, g_ref, o_ref):
    o_ref[...] = _plain_rmsnorm(x_ref[...], g_ref[...])


def _final_norm_call(x, g):
    bsz, length, d = x.shape
    tm = min(1024, length)
    return pl.pallas_call(
        _final_norm_kernel,
        out_shape=jax.ShapeDtypeStruct(x.shape, F32),
        grid=(bsz, length // tm),
        in_specs=[pl.BlockSpec((None, tm, d), lambda b, i: (b, i, 0)),
                  pl.BlockSpec((1, d), lambda b, i: (0, 0))],
        out_specs=pl.BlockSpec((None, tm, d), lambda b, i: (b, i, 0)),
        compiler_params=_params(("arbitrary", "arbitrary")),
        name="final_norm",
    )(x, g.reshape(1, d))


NA_KEYS = NA_WIN_R * GRID_W
MASKED = -1e30
NA_ROWS_PER_STEP = 32


def _na_bias_table(rpb):
    qc = np.arange(GRID_W)[:, None]
    kc = np.arange(GRID_W)[None, :]
    cs = np.clip(qc - NA_WIN_C // 2, 0, GRID_W - NA_WIN_C)
    valid = (kc >= cs) & (kc < cs + NA_WIN_C)
    dc = np.clip(kc - qc + NA_WIN_C - 1, 0, 2 * NA_WIN_C - 2)
    dr = np.arange(NA_WIN_R)[None, :] - np.arange(NA_WIN_R)[:, None] + NA_WIN_R - 1
    t = rpb.astype(F32)[:, dr][:, :, :, dc]
    t = jnp.where(valid[None, None, None], t * LOG2_E, MASKED)
    return t.transpose(0, 1, 3, 2, 4).reshape(rpb.shape[0], NA_WIN_R, GRID_W, NA_KEYS)


def _na_kernel(q_ref, k_ref, v_ref, bias_ref, o_ref, kb_ref, vb_ref):
    rows = q_ref.shape[0] // GRID_W
    kb_ref[...] = k_ref[...].astype(BF16)
    vb_ref[...] = v_ref[...].astype(BF16)

    def row_group(g, carry):
        todo = []
        for u in range(NA_ROWS_PER_STEP):
            r = g * NA_ROWS_PER_STEP + u
            rs = jnp.clip(r - NA_WIN_R // 2, 0, rows - NA_WIN_R)
            q0 = pl.multiple_of(r * GRID_W, GRID_W)
            k0 = pl.multiple_of(rs * GRID_W, GRID_W)
            q = (q_ref[pl.ds(q0, GRID_W), :] * (HEAD_DIM ** -0.5 * LOG2_E)).astype(BF16)
            s = lax.dot_general(q, kb_ref[pl.ds(k0, NA_KEYS), :], (((1,), (1,)), ((), ())),
                                preferred_element_type=F32)
            todo.append((q0, k0, s, r - rs))
        for u, (q0, k0, s, off) in enumerate(todo):
            s = s + bias_ref[off]
            p = jnp.exp2(s - jnp.max(s, axis=-1, keepdims=True))
            todo[u] = (q0, k0, p.astype(BF16), jnp.sum(p, axis=-1, keepdims=True))
        for u, (q0, k0, p, l) in enumerate(todo):
            todo[u] = (q0, jnp.dot(p, vb_ref[pl.ds(k0, NA_KEYS), :], preferred_element_type=F32), l)
        for q0, o, l in todo:
            o_ref[pl.ds(q0, GRID_W), :] = (o / l).astype(o_ref.dtype)
        return carry

    lax.fori_loop(0, rows // NA_ROWS_PER_STEP, row_group, 0)


def _na_call(main, rpb):
    bsz, length, _ = main.shape
    assert length % (GRID_W * NA_ROWS_PER_STEP) == 0 and length // GRID_W >= NA_WIN_R
    hd = HEAD_DIM
    return pl.pallas_call(
        _na_kernel,
        out_shape=jax.ShapeDtypeStruct((bsz, length, GROUP_W), BF16),
        grid=(bsz, H_GROUP),
        in_specs=[
            pl.BlockSpec((None, length, hd), lambda b, h: (b, 0, h)),
            pl.BlockSpec((None, length, hd), lambda b, h: (b, 0, H_GROUP + h)),
            pl.BlockSpec((None, length, hd), lambda b, h: (b, 0, 2 * H_GROUP + h)),
            pl.BlockSpec((None, NA_WIN_R, GRID_W, NA_KEYS), lambda b, h: (h, 0, 0, 0)),
        ],
        out_specs=pl.BlockSpec((None, length, hd), lambda b, h: (b, 0, h)),
        scratch_shapes=[pltpu.VMEM((length, hd), BF16), pltpu.VMEM((length, hd), BF16)],
        compiler_params=_params(("arbitrary", "arbitrary")),
        name="nbr_attn",
    )(main, main, main, _na_bias_table(rpb))


REC_TILE = 512
REC_HEADS = 8
HIGHEST = lax.Precision.HIGHEST


def _tri(n, upper):
    r = lax.broadcasted_iota(jnp.int32, (n, n), 0)
    c = lax.broadcasted_iota(jnp.int32, (n, n), 1)
    return (c >= r) if upper else (c <= r)


def _gdn_conv_kernel(x_ref, w_ref, o_ref, xp_ref):
    length, cw = x_ref.shape
    col0 = pl.program_id(1) * cw
    pad = 8
    xp_ref[0:pad, :] = jnp.zeros((pad, cw), F32)
    xp_ref[pad + length:, :] = jnp.zeros((pad, cw), F32)
    xp_ref[pad:pad + length, :] = x_ref[...]
    w = w_ref[...]
    y = xp_ref[pl.ds(pad - DN_CONV_K // 2, length), :] * w[0:1, :]
    for t in range(1, DN_CONV_K):
        y = y + xp_ref[pl.ds(pad - DN_CONV_K // 2 + t, length), :] * w[t:t + 1, :]
    y = y * jax.nn.sigmoid(y)
    qscale = jnp.where(col0 < GROUP_W, HEAD_DIM ** -0.5, 1.0)
    for h in range(cw // HEAD_DIM):
        yh = y[:, h * HEAD_DIM:(h + 1) * HEAD_DIM]
        nh = yh * (lax.rsqrt(jnp.sum(yh * yh, axis=-1, keepdims=True) + EPS) * qscale)
        o_ref[:, h * HEAD_DIM:(h + 1) * HEAD_DIM] = jnp.where(col0 < 2 * GROUP_W, nh, yh)


def _gdn_conv_call(main, conv_w):
    bsz, length, _ = main.shape
    cw = HEAD_DIM
    nblk = 3 * GROUP_W // cw
    return pl.pallas_call(
        _gdn_conv_kernel,
        out_shape=jax.ShapeDtypeStruct((bsz, length, 3 * GROUP_W), F32),
        grid=(bsz, nblk),
        in_specs=[pl.BlockSpec((None, length, cw), lambda b, j: (b, 0, nblk + j)),
                  pl.BlockSpec((DN_CONV_K, cw), lambda b, j: (0, j))],
        out_specs=pl.BlockSpec((None, length, cw), lambda b, j: (b, 0, j)),
        scratch_shapes=[pltpu.VMEM((length + 16, cw), F32)],
        compiler_params=_params(("arbitrary", "arbitrary")),
        name="gdn_conv",
    )(main, conv_w)


def _gdn_gates_kernel(s_ref, alog_ref, dt_ref, g_ref, gt_ref):
    tm = s_ref.shape[0]
    s = s_ref[...]
    x = s + dt_ref[...]
    softplus = jnp.maximum(x, 0.0) + jnp.log1p(jnp.exp(-jnp.abs(x)))
    g = -jnp.exp(alog_ref[...]) * softplus
    r = lax.broadcasted_iota(jnp.int32, (tm, tm), 0)
    c = lax.broadcasted_iota(jnp.int32, (tm, tm), 1)
    same = (r // DN_CHUNK) == (c // DN_CHUNK)
    lo = jnp.where(same & (c <= r), 1.0, 0.0)
    up = jnp.where(same & (c >= r), 1.0, 0.0)
    gcf = jnp.dot(lo, g, precision=HIGHEST, preferred_element_type=F32)
    gcb = jnp.dot(up, g, precision=HIGHEST, preferred_element_type=F32)
    lane = lax.broadcasted_iota(jnp.int32, s.shape, 1)
    out = jnp.where(lane < H_GROUP, gcf, jnp.where(lane < 2 * H_GROUP, gcb, jax.nn.sigmoid(s)))
    g_ref[...] = out
    for p in range(tm // 128):
        gt_ref[p] = out[p * 128:(p + 1) * 128, :].T[0:4 * H_GROUP, :]


def _gdn_gates_call(side, a_log, dt_bias):
    bsz, length, _ = side.shape
    tm = min(512, length)
    row = lambda t: jnp.pad(t.reshape(1, 2 * H_GROUP).astype(F32), ((0, 0), (0, SIDE_W - 2 * H_GROUP)))
    return pl.pallas_call(
        _gdn_gates_kernel,
        out_shape=(jax.ShapeDtypeStruct((bsz, length, SIDE_W), F32),
                   jax.ShapeDtypeStruct((bsz, length // 128, 4 * H_GROUP, 128), F32)),
        grid=(bsz, length // tm),
        in_specs=[pl.BlockSpec((None, tm, SIDE_W), lambda b, i: (b, i, 0)),
                  pl.BlockSpec((1, SIDE_W), lambda b, i: (0, 0)),
                  pl.BlockSpec((1, SIDE_W), lambda b, i: (0, 0))],
        out_specs=(pl.BlockSpec((None, tm, SIDE_W), lambda b, i: (b, i, 0)),
                   pl.BlockSpec((None, tm // 128, 4 * H_GROUP, 128), lambda b, i: (b, i, 0, 0))),
        compiler_params=_params(("arbitrary", "arbitrary")),
        name="gdn_gates",
    )(side, row(a_log), row(dt_bias))


def _bdot(a, b):
    return jnp.dot(a.astype(BF16), b.astype(BF16), preferred_element_type=F32)


def _gdn_chunks_local(chunks):
    c = DN_CHUNK
    r = lax.broadcasted_iota(jnp.int32, (c, c), 0)
    cc = lax.broadcasted_iota(jnp.int32, (c, c), 1)
    for d in chunks:
        incl = (cc >= r) if d["rev"] else (cc <= r)
        d["gam"] = jnp.exp(jnp.where(incl, d["gci"] - d["gcj"], MASKED))
        d["kb"] = d["k"] * d["beta"]
    for d in chunks:
        kq = jnp.concatenate([d["kb"], d["q"]], axis=0).astype(BF16)
        d["prod"] = lax.dot_general(kq, d["k"].astype(BF16), (((1,), (1,)), ((), ())),
                                    preferred_element_type=F32)
    for d in chunks:
        strict = (cc > r) if d["rev"] else (cc < r)
        d["aqk"] = d["prod"][c:] * d["gam"]
        d["p"] = -jnp.where(strict, d["prod"][0:c] * d["gam"], 0.0)
        d["t"] = jnp.where(r == cc, 1.0, 0.0) + d["p"]
    for _ in range(int(math.log2(c)) - 1):
        for d in chunks:
            d["p"] = _bdot(d["p"], d["p"])
        for d in chunks:
            d["t"] = d["t"] + _bdot(d["t"], d["p"])
    for d in chunks:
        egc = jnp.exp(d["gci"])
        rhs = jnp.concatenate([d["v"] * d["beta"], d["kb"] * egc], axis=1)
        uw = _bdot(d["t"], rhs)
        d["u"], d["w"] = uw[:, 0:HEAD_DIM], uw[:, HEAD_DIM:]
        glast = d["gci"][0:1, :] if d["rev"] else d["gci"][c - 1:c, :]
        d["qd"] = d["q"] * egc
        d["kd"] = d["k"] * jnp.exp(glast - d["gci"])
        d["dlast"] = jnp.exp(glast)


def _gdn_chunks_scan(chunks):
    c = DN_CHUNK
    for d in chunks:
        d["s"] = d["s_ref"][...]
        d["ws"] = _bdot(jnp.concatenate([d["w"], d["qd"]], axis=0), d["s"])
    for d in chunks:
        d["vn"] = (d["u"] - d["ws"][0:c]).astype(BF16)
        d["o"] = d["ws"][c:] + _bdot(d["aqk"], d["vn"])
    for d in chunks:
        d["s_ref"][...] = d["s"] * d["dlast"] + lax.dot_general(
            d["kd"].astype(BF16), d["vn"], (((0,), (0,)), ((), ())), preferred_element_type=F32)


def _gdn_kernel(qf_ref, kf_ref, vf_ref, gf_ref, gtf_ref, qb_ref, kb_ref, vb_ref, gb_ref, gtb_ref,
                of_ref, ob_ref, s_ref):
    hg = pl.program_id(1)

    @pl.when(pl.program_id(2) == 0)
    def _():
        s_ref[...] = jnp.zeros_like(s_ref)

    npair = qf_ref.shape[0] // 128
    c = DN_CHUNK

    def pair(p, carry):
        chunks = []
        for rev in (False, True):
            q_ref, k_ref, v_ref, g_ref, gt_ref, o_ref = (
                (qb_ref, kb_ref, vb_ref, gb_ref, gtb_ref, ob_ref) if rev
                else (qf_ref, kf_ref, vf_ref, gf_ref, gtf_ref, of_ref))
            pp = (npair - 1 - p) if rev else p
            base = pl.multiple_of(pp * 128, 128)
            gt = gt_ref[pp]
            sub = lax.broadcasted_iota(jnp.int32, gt.shape, 0)
            for step, half in enumerate((1, 0) if rev else (0, 1)):
                rows = pl.ds(base + half * c, c)
                g = g_ref[rows, :]
                lane = lax.broadcasted_iota(jnp.int32, g.shape, 1)
                for hl in range(REC_HEADS):
                    cols = slice(hl * HEAD_DIM, (hl + 1) * HEAD_DIM)
                    gcol = hg * REC_HEADS + hl + (H_GROUP if rev else 0)
                    chunks.append(dict(
                        rev=rev, step=step, rows=rows, cols=cols, o_ref=o_ref,
                        s_ref=s_ref.at[int(rev), hl],
                        q=q_ref[rows, cols], k=k_ref[rows, cols], v=v_ref[rows, cols],
                        gci=jnp.sum(jnp.where(lane == gcol, g, 0.0), axis=-1, keepdims=True),
                        beta=jnp.sum(jnp.where(lane == gcol + 2 * H_GROUP, g, 0.0), axis=-1,
                                     keepdims=True),
                        gcj=jnp.sum(jnp.where(sub == gcol, gt, 0.0), axis=0,
                                    keepdims=True)[:, half * c:(half + 1) * c]))
        _gdn_chunks_local(chunks)
        for step in (0, 1):
            now = [d for d in chunks if d["step"] == step]
            _gdn_chunks_scan(now)
            for d in now:
                d["o_ref"][d["rows"], d["cols"]] = d["o"]
        return carry

    lax.fori_loop(0, npair, pair, 0)


def _gdn_call(main, side, conv_w, a_log, dt_bias):
    bsz, length, _ = main.shape
    tl = min(REC_TILE, length)
    nt = length // tl
    w = REC_HEADS * HEAD_DIM
    ng = H_GROUP // REC_HEADS
    qkv = _gdn_conv_call(main, conv_w)
    gates, gates_t = _gdn_gates_call(side, a_log, dt_bias)

    def specs(tile):
        return [pl.BlockSpec((None, tl, w), lambda b, g, i: (b, tile(i), g)),
                pl.BlockSpec((None, tl, w), lambda b, g, i: (b, tile(i), ng + g)),
                pl.BlockSpec((None, tl, w), lambda b, g, i: (b, tile(i), 2 * ng + g)),
                pl.BlockSpec((None, tl, SIDE_W), lambda b, g, i: (b, tile(i), 0)),
                pl.BlockSpec((None, tl // 128, 4 * H_GROUP, 128), lambda b, g, i: (b, tile(i), 0, 0))]

    fw = lambda i: i
    bw = lambda i: nt - 1 - i
    return pl.pallas_call(
        _gdn_kernel,
        out_shape=(jax.ShapeDtypeStruct((bsz, length, GROUP_W), F32),) * 2,
        grid=(bsz, ng, nt),
        in_specs=specs(fw) + specs(bw),
        out_specs=(pl.BlockSpec((None, tl, w), lambda b, g, i: (b, fw(i), g)),
                   pl.BlockSpec((None, tl, w), lambda b, g, i: (b, bw(i), g))),
        scratch_shapes=[pltpu.VMEM((2, REC_HEADS, HEAD_DIM, HEAD_DIM), F32)],
        compiler_params=_params(("arbitrary", "arbitrary", "arbitrary")),
        name="gdn_scan",
    )(qkv, qkv, qkv, gates, gates_t, qkv, qkv, qkv, gates, gates_t)


HG_TILE = 64
HG_SUB = 8
HG_NSUB = HG_TILE // HG_SUB


def _tri_cumsum(x, rev):
    n = x.shape[0]
    tri = jnp.where(_tri(n, rev), 1.0, 0.0).astype(BF16)
    hi = x.astype(BF16)
    r1 = x - hi.astype(F32)
    mid = r1.astype(BF16)
    lo = (r1 - mid.astype(F32)).astype(BF16)
    parts = jnp.dot(tri, jnp.concatenate([hi, mid, lo], axis=1), preferred_element_type=F32)
    w = x.shape[1]
    return (parts[:, 0:w] + parts[:, w:2 * w]) + parts[:, 2 * w:]


def _hg_chunks(chains):
    n, sub, nsub = HG_TILE, HG_SUB, HG_NSUB
    row = lax.broadcasted_iota(jnp.int32, (n, HEAD_DIM), 0)
    pos = row % sub
    tn_dims = (((0,), (0,)), ((), ()))
    for c in chains:
        z, loglb, log1mlb = c["z"], c["loglb"], c["log1mlb"]
        c["q"] = c["q_raw"] * jax.nn.sigmoid(c["q_raw"])
        log_sig = jnp.minimum(z, 0.0) - jnp.log1p(jnp.exp(-jnp.abs(z)))
        bb = log1mlb + log_sig
        lf = jnp.maximum(loglb, bb) + jnp.log1p(jnp.exp(-jnp.abs(loglb - bb)))
        b = _tri_cumsum(lf, c["rev"])
        c["b2"] = b * LOG2_E
        c["c2"] = (b - (bb - z)) * LOG2_E
        c["inp16"] = c["inp"].astype(BF16)

    for c in chains:
        b2, c2, rev = c["b2"], c["c2"], c["rev"]
        c["order"] = list(range(nsub - 1, -1, -1)) if rev else list(range(nsub))
        c["ends"], c["contrib"] = [], []
        for j in c["order"]:
            lo = j * sub
            bend = b2[lo:lo + 1, :] if rev else b2[lo + sub - 1:lo + sub, :]
            c["ends"].append(bend)
            if len(c["contrib"]) < nsub - 1:
                ks = jnp.exp2(bend - c2[lo:lo + sub, :])
                c["contrib"].append(lax.dot_general(c["inp16"][lo:lo + sub, :], ks.astype(BF16),
                                                    tn_dims, preferred_element_type=F32))

    for c in chains:
        b2, c2, q, inp, rev = c["b2"], c["c2"], c["q"], c["inp"], c["rev"]
        o = jnp.sum(q * jnp.exp2(b2 - c2), axis=-1, keepdims=True) * inp
        for d in range(1, sub):
            shift = (sub - d) % sub if rev else d
            valid = (pos <= sub - 1 - d) if rev else (pos >= d)
            sh = lambda x: pltpu.roll(x.reshape(nsub, sub, HEAD_DIM), shift, axis=1).reshape(
                n, HEAD_DIM)
            e = jnp.where(valid, jnp.exp2(b2 - sh(c2)), 0.0)
            o = o + jnp.sum(q * e, axis=-1, keepdims=True) * sh(inp)
        c["o"] = o

    for c in chains:
        b2, q, rev = c["b2"], c["q"], c["rev"]
        c["s"] = c["s_ref"][...]
        qext = [q * jnp.exp2(b2)]
        for m in range(nsub - 1):
            j = c["order"][m]
            lo, hi = (0, j * sub) if rev else ((j + 1) * sub, n)
            part = q[lo:hi] * jnp.exp2(b2[lo:hi] - c["ends"][m])
            zeros = jnp.zeros((n - (hi - lo), HEAD_DIM), F32)
            qext.append(jnp.concatenate([part, zeros] if rev else [zeros, part], axis=0))
        ncat = jnp.concatenate([c["s"]] + c["contrib"], axis=1).astype(BF16)
        c["o"] = c["o"] + lax.dot_general(jnp.concatenate(qext, axis=1).astype(BF16), ncat,
                                          (((1,), (1,)), ((), ())), preferred_element_type=F32)
    for c in chains:
        btot = c["ends"][-1]
        kd = jnp.exp2(btot - c["c2"]).astype(BF16)
        c["s_ref"][...] = c["s"] * jnp.exp2(btot) + lax.dot_general(
            c["inp16"], kd, tn_dims, preferred_element_type=F32)


def _hgrn2_kernel(qf_ref, ff_ref, if_ref, qb_ref, fb_ref, ib_ref, lbf_ref, lbb_ref,
                  of_ref, ob_ref, s_ref):
    @pl.when(pl.program_id(2) == 0)
    def _():
        s_ref[...] = jnp.zeros_like(s_ref)

    nstep = qf_ref.shape[0] // HG_TILE

    def step(t, carry):
        chains = []
        for rev in (False, True):
            q_ref, f_ref, i_ref, lb_ref, o_ref = ((qb_ref, fb_ref, ib_ref, lbb_ref, ob_ref) if rev
                                                  else (qf_ref, ff_ref, if_ref, lbf_ref, of_ref))
            tt = (nstep - 1 - t) if rev else t
            rows = pl.ds(pl.multiple_of(tt * HG_TILE, HG_TILE), HG_TILE)
            for hl in range(REC_HEADS):
                cols = slice(hl * HEAD_DIM, (hl + 1) * HEAD_DIM)
                chains.append(dict(rev=rev, rows=rows, cols=cols, o_ref=o_ref,
                                   s_ref=s_ref.at[int(rev), hl], q_raw=q_ref[rows, cols],
                                   z=f_ref[rows, cols], inp=i_ref[rows, cols],
                                   loglb=lb_ref[0:1, cols], log1mlb=lb_ref[1:2, cols]))
        _hg_chunks(chains)
        for c in chains:
            c["o_ref"][c["rows"], c["cols"]] = c["o"]
        return carry

    lax.fori_loop(0, nstep, step, 0)


def _hgrn2_call(main, lb_fw, lb_bw):
    bsz, length, _ = main.shape
    tl = min(REC_TILE, length)
    nt = length // tl
    w = REC_HEADS * HEAD_DIM
    ng = H_GROUP // REC_HEADS
    fw = lambda i: i
    bw = lambda i: nt - 1 - i
    col = lambda group, tile: pl.BlockSpec((None, tl, w), lambda b, g, i: (b, tile(i), group * ng + g))
    lbrow = pl.BlockSpec((2, w), lambda b, g, i: (0, g))
    logs = lambda lb: jnp.stack([jnp.log(lb), jnp.log1p(-lb)])
    return pl.pallas_call(
        _hgrn2_kernel,
        out_shape=(jax.ShapeDtypeStruct((bsz, length, GROUP_W), F32),) * 2,
        grid=(bsz, ng, nt),
        in_specs=[col(0, fw), col(1, fw), col(3, fw), col(0, bw), col(2, bw), col(3, bw), lbrow, lbrow],
        out_specs=(pl.BlockSpec((None, tl, w), lambda b, g, i: (b, fw(i), g)),
                   pl.BlockSpec((None, tl, w), lambda b, g, i: (b, bw(i), g))),
        scratch_shapes=[pltpu.VMEM((2, REC_HEADS, HEAD_DIM, HEAD_DIM), F32)],
        compiler_params=_params(("arbitrary", "arbitrary", "arbitrary")),
        name="hgrn2_scan",
    )(main, main, main, main, main, main, logs(lb_fw), logs(lb_bw))


MLA_QK = 2 * MLA_NOPE
MLA_KEY_BLOCK = 512
ROPE_SWAP = np.concatenate([np.arange(MLA_ROPE // 2, MLA_ROPE), np.arange(MLA_ROPE // 2)])


def _rope_table(length):
    half = MLA_ROPE // 2
    inv = ROPE_BASE ** (-jnp.arange(half, dtype=F32) / half)
    ang = jnp.arange(length, dtype=F32)[:, None] * inv[None, :]
    cos, sin = jnp.cos(ang), jnp.sin(ang)
    return jnp.concatenate([cos, cos, -sin, sin], axis=-1)


def _mla_q_weights(w_uq):
    w = w_uq.reshape(MLA_Q_LORA, H_GROUP, MLA_NOPE + MLA_ROPE)
    rope = w[:, :, MLA_NOPE:]
    w = jnp.concatenate([w[:, :, :MLA_NOPE], rope, rope[:, :, ROPE_SWAP]], axis=-1)
    return w.reshape(MLA_Q_LORA, H_GROUP * MLA_QK).astype(BF16)


def _plain_rmsnorm(x, g):
    return x * lax.rsqrt(jnp.mean(x * x, axis=-1, keepdims=True) + EPS) * g


def _mla_proj_kernel(cq_ref, ckv_ref, side_ref, tab_ref, qn_ref, kvn_ref, wq_ref, wkv_ref,
                     q_out, k_out, v_out):
    scale = (MLA_NOPE + MLA_ROPE) ** -0.5 * LOG2_E
    cqn = _plain_rmsnorm(cq_ref[...], qn_ref[...]).astype(BF16)
    ckvn = _plain_rmsnorm(ckv_ref[...], kvn_ref[...]).astype(BF16)
    q = jnp.dot(cqn, wq_ref[...], preferred_element_type=F32)
    kv = jnp.dot(ckvn, wkv_ref[...], preferred_element_type=F32)
    tab = tab_ref[...]
    kr = (side_ref[...] * tab).astype(BF16)
    for h in range(H_GROUP):
        c0 = h * MLA_QK
        qr = q[:, c0 + MLA_NOPE:c0 + MLA_QK] * tab
        qr = qr + pltpu.roll(qr, MLA_ROPE, axis=1)
        q_out[h, :, 0:MLA_NOPE] = (q[:, c0:c0 + MLA_NOPE] * scale).astype(BF16)
        q_out[h, :, MLA_NOPE:MLA_QK] = (qr * scale).astype(BF16)
        k_out[h, :, 0:MLA_NOPE] = kv[:, c0:c0 + MLA_NOPE].astype(BF16)
        k_out[h, :, MLA_NOPE:MLA_QK] = kr
        v_out[h] = kv[:, c0 + MLA_NOPE:c0 + MLA_QK].astype(BF16)


def _mla_attn_kernel(q_ref, k_ref, v_ref, o_ref):
    length = k_ref.shape[0]
    tk = MLA_KEY_BLOCK
    nk = length // tk
    q = q_ref[...]

    def scores(kb):
        return lax.dot_general(q, k_ref[kb * tk:(kb + 1) * tk, :], (((1,), (1,)), ((), ())),
                               preferred_element_type=F32)

    s_next = scores(0)
    m = l = acc = None
    for kb in range(nk):
        s = s_next
        if kb + 1 < nk:
            s_next = scores(kb + 1)
        m_blk = jnp.max(s, axis=-1, keepdims=True)
        if kb == 0:
            m = m_blk
            p = jnp.exp2(s - m)
            l = jnp.sum(p, axis=-1, keepdims=True)
            acc = jnp.dot(p.astype(BF16), v_ref[0:tk, :], preferred_element_type=F32)
        else:
            m_new = jnp.maximum(m, m_blk)
            alpha = jnp.exp2(m - m_new)
            p = jnp.exp2(s - m_new)
            l = alpha * l + jnp.sum(p, axis=-1, keepdims=True)
            acc = alpha * acc + jnp.dot(p.astype(BF16), v_ref[kb * tk:(kb + 1) * tk, :],
                                        preferred_element_type=F32)
            m = m_new
    o_ref[...] = (acc / l).astype(o_ref.dtype)


def _mla_call(main, side, q_norm, w_uq, kv_norm, w_ukv):
    bsz, length, _ = main.shape
    tm = min(512, length)
    lora = MLA_Q_LORA
    q_all, k_all, v_all = pl.pallas_call(
        _mla_proj_kernel,
        out_shape=(jax.ShapeDtypeStruct((bsz, H_GROUP, length, MLA_QK), BF16),
                   jax.ShapeDtypeStruct((bsz, H_GROUP, length, MLA_QK), BF16),
                   jax.ShapeDtypeStruct((bsz, H_GROUP, length, MLA_V), BF16)),
        grid=(bsz, length // tm),
        in_specs=[
            pl.BlockSpec((None, tm, lora), lambda b, i: (b, i, 10)),
            pl.BlockSpec((None, tm, lora), lambda b, i: (b, i, 11)),
            pl.BlockSpec((None, tm, SIDE_W), lambda b, i: (b, i, 0)),
            pl.BlockSpec((tm, SIDE_W), lambda b, i: (i, 0)),
            pl.BlockSpec((1, lora), lambda b, i: (0, 0)),
            pl.BlockSpec((1, lora), lambda b, i: (0, 0)),
            pl.BlockSpec((lora, H_GROUP * MLA_QK), lambda b, i: (0, 0)),
            pl.BlockSpec((lora, H_GROUP * MLA_QK), lambda b, i: (0, 0)),
        ],
        out_specs=(pl.BlockSpec((None, H_GROUP, tm, MLA_QK), lambda b, i: (b, 0, i, 0)),
                   pl.BlockSpec((None, H_GROUP, tm, MLA_QK), lambda b, i: (b, 0, i, 0)),
                   pl.BlockSpec((None, H_GROUP, tm, MLA_V), lambda b, i: (b, 0, i, 0))),
        compiler_params=_params(("arbitrary", "arbitrary")),
        name="mla_proj",
    )(main, main, side, _rope_table(length), q_norm.reshape(1, lora), kv_norm.reshape(1, lora),
      _mla_q_weights(w_uq), w_ukv.astype(BF16))

    tq = 512
    return pl.pallas_call(
        _mla_attn_kernel,
        out_shape=jax.ShapeDtypeStruct((bsz, length, H_GROUP * MLA_V), BF16),
        grid=(bsz, H_GROUP, length // tq),
        in_specs=[
            pl.BlockSpec((None, None, tq, MLA_QK), lambda b, h, i: (b, h, i, 0)),
            pl.BlockSpec((None, None, length, MLA_QK), lambda b, h, i: (b, h, 0, 0)),
            pl.BlockSpec((None, None, length, MLA_V), lambda b, h, i: (b, h, 0, 0)),
        ],
        out_specs=pl.BlockSpec((None, tq, MLA_V), lambda b, h, i: (b, i, h)),
        compiler_params=_params(("arbitrary", "arbitrary", "arbitrary")),
        name="mla_attn",
    )(q_all, k_all, v_all)


def _pad_side(w):
    return jnp.pad(w, ((0, 0), (0, SIDE_W - w.shape[1]))).astype(BF16)


def _trunk(x, mods, lb, p):
    for l in range(DEPTH):
        mod3 = mods[l]
        if l % 2 == 0:
            e = l // 2
            main, side = _inproj_call(x, mod3, p["norm_mix"][l], p["w_in_ab_main"][e],
                                      p["w_in_ab_side"][e])
            att = _na_call(main, p["na_rpb"][e])
            o_fw, o_bw = _gdn_call(main, side, p["dn_conv"][e], p["dn_a_log"][e], p["dn_dt_bias"][e])
            x, h2 = _outproj_call(att, o_fw, o_bw, main, 6, p["dn_norm"][e], False,
                                  p["w_out_ab"][e], x, mod3, p["norm_ffn"][l])
        else:
            o = l // 2
            main, side = _inproj_call(x, mod3, p["norm_mix"][l], p["w_in_cd_main"][o],
                                      p["w_in_cd_side"][o])
            o_fw, o_bw = _hgrn2_call(main, lb[0, l], lb[1, l])
            att = _mla_call(main, side, p["mla_q_norm"][o], p["mla_w_uq"][o], p["mla_kv_norm"][o],
                            p["mla_w_ukv"][o])
            x, h2 = _outproj_call(att, o_fw, o_bw, main, 4, p["hg_norm"][o], True,
                                  p["w_out_cd"][o], x, mod3, p["norm_ffn"][l])
        x = _ffn_split_call(x, h2, mod3, p["ffn_w_gate"][l], p["ffn_w_up"][l], p["ffn_conv"][l],
                            p["ffn_w_down"][l])
    return _final_norm_call(x, p["final_norm"])


def kernel(x_prompt, x_sample, c_prompt, c_sample, norm_mix, norm_ffn, w_ada, b_ada, w_in_ab, w_out_ab, na_rpb, dn_conv, dn_a_log, dn_dt_bias, dn_norm, w_in_cd, w_out_cd, hg_lower_bounds, hg_norm, mla_q_norm, mla_w_uq, mla_kv_norm, mla_w_ukv, ffn_w_gate, ffn_w_up, ffn_conv, ffn_w_down, final_norm):
    n_p, n_s = c_prompt.shape[0], c_sample.shape[0]
    rows = -(-(n_p + n_s) // 8) * 8
    c_all = jnp.concatenate([c_prompt, c_sample, jnp.zeros((rows - n_p - n_s, D_MODEL), F32)], axis=0)
    mod_all = _ada_call(c_all, w_ada, b_ada)
    mods_p = [mod_all[l, :n_p].reshape(n_p, 1, 6 * D_MODEL) for l in range(DEPTH)]
    mods_s = [mod_all[l, n_p:n_p + n_s].reshape(n_s, 1, 6 * D_MODEL) for l in range(DEPTH)]

    lb = jnp.cumsum(jax.nn.softmax(hg_lower_bounds.astype(F32), axis=1), axis=1)
    lb = lb - lb[:, :1]

    p = dict(
        norm_mix=norm_mix, norm_ffn=norm_ffn, na_rpb=na_rpb, dn_conv=dn_conv, dn_a_log=dn_a_log,
        dn_dt_bias=dn_dt_bias, dn_norm=dn_norm, hg_norm=hg_norm, mla_q_norm=mla_q_norm,
        mla_w_uq=mla_w_uq, mla_kv_norm=mla_kv_norm, mla_w_ukv=mla_w_ukv, ffn_conv=ffn_conv,
        final_norm=final_norm,
        w_in_ab_main=w_in_ab[:, :, :AB_MAIN].astype(BF16),
        w_in_ab_side=jax.vmap(_pad_side)(w_in_ab[:, :, AB_MAIN:]),
        w_in_cd_main=w_in_cd[:, :, :CD_MAIN].astype(BF16),
        w_in_cd_side=jnp.concatenate([w_in_cd[:, :, CD_MAIN:],
                                      w_in_cd[:, :, CD_MAIN:][:, :, ROPE_SWAP]], axis=-1).astype(BF16),
        w_out_ab=w_out_ab.astype(BF16), w_out_cd=w_out_cd.astype(BF16),
        ffn_w_gate=ffn_w_gate.astype(BF16), ffn_w_up=ffn_w_up.astype(BF16),
        ffn_w_down=ffn_w_down.astype(BF16),
    )
    y_prompt = _trunk(x_prompt, mods_p, lb, p)
    y_sample = _trunk(x_sample, mods_s, lb, p)
    return (y_prompt, y_sample)
```

```python
import functools
import math

import numpy as np
import jax
import jax.numpy as jnp
from jax import lax
from jax.experimental import pallas as pl
from jax.experimental.pallas import tpu as pltpu

D_MODEL = 2048
DEPTH = 4
HEAD_DIM = 128
H_GROUP = 8
GROUP_W = H_GROUP * HEAD_DIM
GRID_W = 64
NA_WIN_R = 8
NA_WIN_C = 16
DN_CONV_K = 5
DN_CHUNK = 64
MLA_Q_LORA = 512
MLA_KV_LORA = 512
MLA_NOPE = 128
MLA_ROPE = 64
MLA_V = 128
ROPE_BASE = 10000.0
D_FF = 5632
FFN_CONV_K = 3
EPS = 1e-6
AB_MAIN = 7 * GROUP_W
CD_MAIN = 6 * GROUP_W
SIDE_W = 128

F32 = jnp.float32
BF16 = jnp.bfloat16
LOG2_E = 1.4426950408889634

VMEM_LIMIT_BYTES = 56 * 1024 * 1024
HALO = 16
def _params(sem):
    return pltpu.CompilerParams(dimension_semantics=sem, vmem_limit_bytes=VMEM_LIMIT_BYTES)


def _ada_kernel(c_ref, w_ref, b_ref, o_ref):
    c = c_ref[...]
    cond = c * jax.nn.sigmoid(c)
    o_ref[...] = jnp.dot(cond, w_ref[...], precision=lax.Precision.HIGHEST,
                         preferred_element_type=F32) + b_ref[...]


def _ada_call(c_all, w_ada, b_ada):
    rows = c_all.shape[0]
    tn = 1024
    return pl.pallas_call(
        _ada_kernel,
        out_shape=jax.ShapeDtypeStruct((DEPTH, rows, 6 * D_MODEL), F32),
        grid=(DEPTH, 6 * D_MODEL // tn),
        in_specs=[
            pl.BlockSpec((rows, D_MODEL), lambda l, j: (0, 0)),
            pl.BlockSpec((None, D_MODEL, tn), lambda l, j: (l, 0, j)),
            pl.BlockSpec((None, 1, tn), lambda l, j: (l, 0, j)),
        ],
        out_specs=pl.BlockSpec((None, rows, tn), lambda l, j: (l, 0, j)),
        compiler_params=_params(("arbitrary", "arbitrary")),
        name="ada_mod",
    )(c_all, w_ada, b_ada.reshape(DEPTH, 1, 6 * D_MODEL))


def _norm_mod(x, g, sh, sc):
    y = x * lax.rsqrt(jnp.mean(x * x, axis=-1, keepdims=True) + EPS)
    return (y * g) * (1.0 + sc) + sh


def _inproj_kernel(x_ref, g_ref, sh_ref, sc_ref, w_ref, ws_ref, o_ref, os_ref, h_ref):
    @pl.when(pl.program_id(2) == 0)
    def _():
        hb = _norm_mod(x_ref[...], g_ref[...], sh_ref[...], sc_ref[...]).astype(BF16)
        h_ref[...] = hb
        os_ref[...] = jnp.dot(hb, ws_ref[...], preferred_element_type=F32)

    o_ref[...] = jnp.dot(h_ref[...], w_ref[...], preferred_element_type=F32)


def _inproj_call(x, mod3, norm_g, w_main, w_side, e):
    bsz, length, d = x.shape
    n_main = w_main.shape[2]
    tm = min(1024, length)
    tn = 1024
    return pl.pallas_call(
        _inproj_kernel,
        out_shape=(jax.ShapeDtypeStruct((bsz, length, n_main), F32),
                   jax.ShapeDtypeStruct((bsz, length, SIDE_W), F32)),
        grid=(bsz, length // tm, n_main // tn),
        in_specs=[
            pl.BlockSpec((None, tm, d), lambda b, i, j: (b, i, 0)),
            pl.BlockSpec((1, d), lambda b, i, j: (0, 0)),
            pl.BlockSpec((None, 1, d), lambda b, i, j: (b, 0, 0)),
            pl.BlockSpec((None, 1, d), lambda b, i, j: (b, 0, 1)),
            pl.BlockSpec((None, d, tn), lambda b, i, j: (e, 0, j)),
            pl.BlockSpec((None, d, SIDE_W), lambda b, i, j: (e, 0, 0)),
        ],
        out_specs=(pl.BlockSpec((None, tm, tn), lambda b, i, j: (b, i, j)),
                   pl.BlockSpec((None, tm, SIDE_W), lambda b, i, j: (b, i, 0))),
        scratch_shapes=[pltpu.VMEM((tm, d), BF16)],
        compiler_params=_params(("arbitrary", "arbitrary", "arbitrary")),
        name="inproj",
    )(x, norm_g.reshape(1, d), mod3, mod3, w_main, w_side)


def _outproj_kernel(rec_first, att_ref, of_ref, ob_ref, z_ref, ng_ref, w_ref, x_ref, gate_ref,
                    g2_ref, sh2_ref, sc2_ref, o_ref, h2_ref, y_ref):
    half = att_ref.shape[-1]
    ng = ng_ref[...]
    for h in range(H_GROUP):
        sl = slice(h * HEAD_DIM, (h + 1) * HEAD_DIM)
        o = of_ref[:, sl] + ob_ref[:, sl]
        o = o * lax.rsqrt(jnp.mean(o * o, axis=-1, keepdims=True) + EPS) * ng
        z = z_ref[:, sl]
        y_ref[:, sl] = (o * (z * jax.nn.sigmoid(z))).astype(BF16)
    rec0, att0 = (0, half) if rec_first else (half, 0)
    y = jnp.dot(y_ref[...], w_ref[rec0:rec0 + half, :], preferred_element_type=F32)
    y = y + jnp.dot(att_ref[...], w_ref[att0:att0 + half, :], preferred_element_type=F32)
    xo = x_ref[...] + gate_ref[...] * y
    o_ref[...] = xo
    h2_ref[...] = _norm_mod(xo, g2_ref[...], sh2_ref[...], sc2_ref[...]).astype(BF16)


def _outproj_call(att, o_fw, o_bw, main, gate_group, norm_g, rec_first, w_out, e, x, mod3, norm_ffn_g):
    bsz, length, d = x.shape
    half = att.shape[-1]
    tm = min(512, length)
    blk = lambda col: pl.BlockSpec((None, tm, half), lambda b, i: (b, i, col))
    row = pl.BlockSpec((None, tm, d), lambda b, i: (b, i, 0))
    mod = lambda k: pl.BlockSpec((None, 1, d), lambda b, i: (b, 0, k))
    return pl.pallas_call(
        functools.partial(_outproj_kernel, rec_first),
        out_shape=(jax.ShapeDtypeStruct(x.shape, F32), jax.ShapeDtypeStruct(x.shape, BF16)),
        grid=(bsz, length // tm),
        in_specs=[
            blk(0), blk(0), blk(0), blk(gate_group),
            pl.BlockSpec((1, HEAD_DIM), lambda b, i: (0, 0)),
            pl.BlockSpec((None, 2 * half, d), lambda b, i: (e, 0, 0)),
            row,
            mod(2),
            pl.BlockSpec((1, d), lambda b, i: (0, 0)),
            mod(3), mod(4),
        ],
        out_specs=(row, row),
        scratch_shapes=[pltpu.VMEM((tm, half), BF16)],
        compiler_params=_params(("arbitrary", "arbitrary")),
        name="outproj",
    )(att, o_fw, o_bw, main, norm_g.reshape(1, HEAD_DIM), w_out, x, mod3,
      norm_ffn_g.reshape(1, d), mod3, mod3)


def _ffn_up_kernel(hp_ref, hm_ref, hn_ref, wg_ref, wu_ref, cw_ref, act_ref, h_ref, gp_ref):
    i = pl.program_id(1)
    tm = hm_ref.shape[0]

    @pl.when(pl.program_id(2) == 0)
    def _():
        h_ref[HALO:HALO + tm, :] = hm_ref[...]
        h_ref[0:HALO, :] = jnp.where(i == 0, 0.0, hp_ref[...]).astype(BF16)
        h_ref[HALO + tm:, :] = jnp.where(i == pl.num_programs(1) - 1, 0.0, hn_ref[...]).astype(BF16)

    gp_ref[...] = jnp.dot(h_ref[...], wg_ref[...], preferred_element_type=F32)
    u = jnp.dot(hm_ref[...], wu_ref[...], preferred_element_type=F32)
    cw = cw_ref[...]
    a = gp_ref[pl.ds(HALO - 1, tm), :] * cw[0:1, :]
    a = a + gp_ref[pl.ds(HALO, tm), :] * cw[1:2, :]
    a = a + gp_ref[pl.ds(HALO + 1, tm), :] * cw[2:3, :]
    act_ref[...] = ((a * jax.nn.sigmoid(a)) * u).astype(BF16)


def _ffn_down_kernel(act_ref, wd_ref, x_ref, gate_ref, o_ref):
    o_ref[...] = x_ref[...] + gate_ref[...] * jnp.dot(act_ref[...], wd_ref[...],
                                                     preferred_element_type=F32)


def _ffn_split_call(x, h2, mod3, w_gate, w_up, conv_w, w_down, l):
    bsz, length, d = x.shape
    tm = min(1024, length)
    tf = 512
    nh = tm // HALO
    last_h = length // HALO - 1
    act = pl.pallas_call(
        _ffn_up_kernel,
        out_shape=jax.ShapeDtypeStruct((bsz, length, D_FF), BF16),
        grid=(bsz, length // tm, D_FF // tf),
        in_specs=[
            pl.BlockSpec((None, HALO, d), lambda b, i, j: (b, jnp.maximum(i * nh - 1, 0), 0)),
            pl.BlockSpec((None, tm, d), lambda b, i, j: (b, i, 0)),
            pl.BlockSpec((None, HALO, d), lambda b, i, j: (b, jnp.minimum((i + 1) * nh, last_h), 0)),
            pl.BlockSpec((None, d, tf), lambda b, i, j: (l, 0, j)),
            pl.BlockSpec((None, d, tf), lambda b, i, j: (l, 0, j)),
            pl.BlockSpec((None, FFN_CONV_K, tf), lambda b, i, j: (l, 0, j)),
        ],
        out_specs=pl.BlockSpec((None, tm, tf), lambda b, i, j: (b, i, j)),
        scratch_shapes=[pltpu.VMEM((tm + 2 * HALO, d), BF16),
                        pltpu.VMEM((tm + 2 * HALO, tf), F32)],
        compiler_params=_params(("arbitrary", "arbitrary", "arbitrary")),
        name="ffn_up",
    )(h2, h2, h2, w_gate, w_up, conv_w)

    tn = 512
    gate_blk = 5 * (d // tn)
    return pl.pallas_call(
        _ffn_down_kernel,
        out_shape=jax.ShapeDtypeStruct(x.shape, F32),
        grid=(bsz, length // tm, d // tn),
        in_specs=[
            pl.BlockSpec((None, tm, D_FF), lambda b, i, n: (b, i, 0)),
            pl.BlockSpec((None, D_FF, tn), lambda b, i, n: (l, 0, n)),
            pl.BlockSpec((None, tm, tn), lambda b, i, n: (b, i, n)),
            pl.BlockSpec((None, 1, tn), lambda b, i, n: (b, 0, gate_blk + n)),
        ],
        out_specs=pl.BlockSpec((None, tm, tn), lambda b, i, n: (b, i, n)),
        compiler_params=_params(("arbitrary", "arbitrary", "arbitrary")),
        name="ffn_down",
    )(act, w_down, x, mod3)


def _final_norm_kernel(x_ref, g_ref, o_ref):
    o_ref[...] = _plain_rmsnorm(x_ref[...], g_ref[...])


def _final_norm_call(x, g):
    bsz, length, d = x.shape
    tm = min(1024, length)
    return pl.pallas_call(
        _final_norm_kernel,
        out_shape=jax.ShapeDtypeStruct(x.shape, F32),
        grid=(bsz, length // tm),
        in_specs=[pl.BlockSpec((None, tm, d), lambda b, i: (b, i, 0)),
                  pl.BlockSpec((1, d), lambda b, i: (0, 0))],
        out_specs=pl.BlockSpec((None, tm, d), lambda b, i: (b, i, 0)),
        compiler_params=_params(("arbitrary", "arbitrary")),
        name="final_norm",
    )(x, g.reshape(1, d))


NA_KEYS = NA_WIN_R * GRID_W
MASKED = -1e30
NA_ROWS_PER_STEP = 32


def _na_bias_table(rpb):
    qc = np.arange(GRID_W)[:, None]
    kc = np.arange(GRID_W)[None, :]
    cs = np.clip(qc - NA_WIN_C // 2, 0, GRID_W - NA_WIN_C)
    valid = (kc >= cs) & (kc < cs + NA_WIN_C)
    dc = np.clip(kc - qc + NA_WIN_C - 1, 0, 2 * NA_WIN_C - 2)
    dr = np.arange(NA_WIN_R)[None, :] - np.arange(NA_WIN_R)[:, None] + NA_WIN_R - 1
    t = rpb.astype(F32)[:, dr][:, :, :, dc]
    t = jnp.where(valid[None, None, None], t * LOG2_E, MASKED)
    return t.transpose(0, 1, 3, 2, 4).reshape(rpb.shape[0], NA_WIN_R, GRID_W, NA_KEYS)


def _na_kernel(q_ref, k_ref, v_ref, bias_ref, o_ref, kb_ref, vb_ref):
    rows = q_ref.shape[0] // GRID_W
    kb_ref[...] = k_ref[...].astype(BF16)
    vb_ref[...] = v_ref[...].astype(BF16)

    def row_group(g, carry):
        todo = []
        for u in range(NA_ROWS_PER_STEP):
            r = g * NA_ROWS_PER_STEP + u
            rs = jnp.clip(r - NA_WIN_R // 2, 0, rows - NA_WIN_R)
            q0 = pl.multiple_of(r * GRID_W, GRID_W)
            k0 = pl.multiple_of(rs * GRID_W, GRID_W)
            q = (q_ref[pl.ds(q0, GRID_W), :] * (HEAD_DIM ** -0.5 * LOG2_E)).astype(BF16)
            s = lax.dot_general(q, kb_ref[pl.ds(k0, NA_KEYS), :], (((1,), (1,)), ((), ())),
                                preferred_element_type=F32)
            todo.append((q0, k0, s, r - rs))
        for u, (q0, k0, s, off) in enumerate(todo):
            s = s + bias_ref[off]
            p = jnp.exp2(s - jnp.max(s, axis=-1, keepdims=True))
            todo[u] = (q0, k0, p.astype(BF16), jnp.sum(p, axis=-1, keepdims=True))
        for u, (q0, k0, p, l) in enumerate(todo):
            todo[u] = (q0, jnp.dot(p, vb_ref[pl.ds(k0, NA_KEYS), :], preferred_element_type=F32), l)
        for q0, o, l in todo:
            o_ref[pl.ds(q0, GRID_W), :] = (o / l).astype(o_ref.dtype)
        return carry

    lax.fori_loop(0, rows // NA_ROWS_PER_STEP, row_group, 0)


def _na_call(main, rpb):
    bsz, length, _ = main.shape
    assert length % (GRID_W * NA_ROWS_PER_STEP) == 0 and length // GRID_W >= NA_WIN_R
    hd = HEAD_DIM
    return pl.pallas_call(
        _na_kernel,
        out_shape=jax.ShapeDtypeStruct((bsz, length, GROUP_W), BF16),
        grid=(bsz, H_GROUP),
        in_specs=[
            pl.BlockSpec((None, length, hd), lambda b, h: (b, 0, h)),
            pl.BlockSpec((None, length, hd), lambda b, h: (b, 0, H_GROUP + h)),
            pl.BlockSpec((None, length, hd), lambda b, h: (b, 0, 2 * H_GROUP + h)),
            pl.BlockSpec((None, NA_WIN_R, GRID_W, NA_KEYS), lambda b, h: (h, 0, 0, 0)),
        ],
        out_specs=pl.BlockSpec((None, length, hd), lambda b, h: (b, 0, h)),
        scratch_shapes=[pltpu.VMEM((length, hd), BF16), pltpu.VMEM((length, hd), BF16)],
        compiler_params=_params(("arbitrary", "arbitrary")),
        name="nbr_attn",
    )(main, main, main, _na_bias_table(rpb))


REC_TILE = 512
REC_HEADS = 8
HIGHEST = lax.Precision.HIGHEST


def _tri(n, upper):
    r = lax.broadcasted_iota(jnp.int32, (n, n), 0)
    c = lax.broadcasted_iota(jnp.int32, (n, n), 1)
    return (c >= r) if upper else (c <= r)


def _gdn_conv_kernel(x_ref, w_ref, o_ref, xp_ref):
    length, cw = x_ref.shape
    col0 = pl.program_id(1) * cw
    pad = 8
    xp_ref[0:pad, :] = jnp.zeros((pad, cw), F32)
    xp_ref[pad + length:, :] = jnp.zeros((pad, cw), F32)
    xp_ref[pad:pad + length, :] = x_ref[...]
    w = w_ref[...]
    y = xp_ref[pl.ds(pad - DN_CONV_K // 2, length), :] * w[0:1, :]
    for t in range(1, DN_CONV_K):
        y = y + xp_ref[pl.ds(pad - DN_CONV_K // 2 + t, length), :] * w[t:t + 1, :]
    y = y * jax.nn.sigmoid(y)
    qscale = jnp.where(col0 < GROUP_W, HEAD_DIM ** -0.5, 1.0)
    for h in range(cw // HEAD_DIM):
        yh = y[:, h * HEAD_DIM:(h + 1) * HEAD_DIM]
        nh = yh * (lax.rsqrt(jnp.sum(yh * yh, axis=-1, keepdims=True) + EPS) * qscale)
        o_ref[:, h * HEAD_DIM:(h + 1) * HEAD_DIM] = jnp.where(col0 < 2 * GROUP_W, nh, yh)


def _gdn_conv_call(main, conv_w):
    bsz, length, _ = main.shape
    cw = HEAD_DIM
    nblk = 3 * GROUP_W // cw
    return pl.pallas_call(
        _gdn_conv_kernel,
        out_shape=jax.ShapeDtypeStruct((bsz, length, 3 * GROUP_W), F32),
        grid=(bsz, nblk),
        in_specs=[pl.BlockSpec((None, length, cw), lambda b, j: (b, 0, nblk + j)),
                  pl.BlockSpec((DN_CONV_K, cw), lambda b, j: (0, j))],
        out_specs=pl.BlockSpec((None, length, cw), lambda b, j: (b, 0, j)),
        scratch_shapes=[pltpu.VMEM((length + 16, cw), F32)],
        compiler_params=_params(("arbitrary", "arbitrary")),
        name="gdn_conv",
    )(main, conv_w)


def _gdn_gates_kernel(s_ref, alog_ref, dt_ref, g_ref, gt_ref):
    tm = s_ref.shape[0]
    s = s_ref[...]
    x = s + dt_ref[...]
    softplus = jnp.maximum(x, 0.0) + jnp.log1p(jnp.exp(-jnp.abs(x)))
    g = -jnp.exp(alog_ref[...]) * softplus
    r = lax.broadcasted_iota(jnp.int32, (tm, tm), 0)
    c = lax.broadcasted_iota(jnp.int32, (tm, tm), 1)
    same = (r // DN_CHUNK) == (c // DN_CHUNK)
    lo = jnp.where(same & (c <= r), 1.0, 0.0)
    up = jnp.where(same & (c >= r), 1.0, 0.0)
    gcf = jnp.dot(lo, g, precision=HIGHEST, preferred_element_type=F32)
    gcb = jnp.dot(up, g, precision=HIGHEST, preferred_element_type=F32)
    lane = lax.broadcasted_iota(jnp.int32, s.shape, 1)
    out = jnp.where(lane < H_GROUP, gcf, jnp.where(lane < 2 * H_GROUP, gcb, jax.nn.sigmoid(s)))
    g_ref[...] = out
    for p in range(tm // 128):
        gt_ref[p] = out[p * 128:(p + 1) * 128, :].T[0:4 * H_GROUP, :]


def _gdn_gates_call(side, a_log, dt_bias):
    bsz, length, _ = side.shape
    tm = min(512, length)
    row = lambda t: jnp.pad(t.reshape(1, 2 * H_GROUP).astype(F32), ((0, 0), (0, SIDE_W - 2 * H_GROUP)))
    return pl.pallas_call(
        _gdn_gates_kernel,
        out_shape=(jax.ShapeDtypeStruct((bsz, length, SIDE_W), F32),
                   jax.ShapeDtypeStruct((bsz, length // 128, 4 * H_GROUP, 128), F32)),
        grid=(bsz, length // tm),
        in_specs=[pl.BlockSpec((None, tm, SIDE_W), lambda b, i: (b, i, 0)),
                  pl.BlockSpec((1, SIDE_W), lambda b, i: (0, 0)),
                  pl.BlockSpec((1, SIDE_W), lambda b, i: (0, 0))],
        out_specs=(pl.BlockSpec((None, tm, SIDE_W), lambda b, i: (b, i, 0)),
                   pl.BlockSpec((None, tm // 128, 4 * H_GROUP, 128), lambda b, i: (b, i, 0, 0))),
        compiler_params=_params(("arbitrary", "arbitrary")),
        name="gdn_gates",
    )(side, row(a_log), row(dt_bias))


def _bdot(a, b):
    return jnp.dot(a.astype(BF16), b.astype(BF16), preferred_element_type=F32)


def _gdn_chunks_local(chunks):
    c = DN_CHUNK
    r = lax.broadcasted_iota(jnp.int32, (c, c), 0)
    cc = lax.broadcasted_iota(jnp.int32, (c, c), 1)
    for d in chunks:
        incl = (cc >= r) if d["rev"] else (cc <= r)
        d["gam"] = jnp.exp(jnp.where(incl, d["gci"] - d["gcj"], MASKED))
        d["kb"] = d["k"] * d["beta"]
    for d in chunks:
        kq = jnp.concatenate([d["kb"], d["q"]], axis=0).astype(BF16)
        d["prod"] = lax.dot_general(kq, d["k"].astype(BF16), (((1,), (1,)), ((), ())),
                                    preferred_element_type=F32)
    for d in chunks:
        strict = (cc > r) if d["rev"] else (cc < r)
        d["aqk"] = d["prod"][c:] * d["gam"]
        d["p"] = -jnp.where(strict, d["prod"][0:c] * d["gam"], 0.0)
        d["t"] = jnp.where(r == cc, 1.0, 0.0) + d["p"]
    for _ in range(int(math.log2(c)) - 1):
        for d in chunks:
            d["p"] = _bdot(d["p"], d["p"])
        for d in chunks:
            d["t"] = d["t"] + _bdot(d["t"], d["p"])
    for d in chunks:
        egc = jnp.exp(d["gci"])
        rhs = jnp.concatenate([d["v"] * d["beta"], d["kb"] * egc], axis=1)
        uw = _bdot(d["t"], rhs)
        d["u"], d["w"] = uw[:, 0:HEAD_DIM], uw[:, HEAD_DIM:]
        glast = d["gci"][0:1, :] if d["rev"] else d["gci"][c - 1:c, :]
        d["qd"] = d["q"] * egc
        d["kd"] = d["k"] * jnp.exp(glast - d["gci"])
        d["dlast"] = jnp.exp(glast)


def _gdn_chunks_scan(chunks):
    c = DN_CHUNK
    for d in chunks:
        d["s"] = d["s_ref"][...]
        d["ws"] = _bdot(jnp.concatenate([d["w"], d["qd"]], axis=0), d["s"])
    for d in chunks:
        d["vn"] = (d["u"] - d["ws"][0:c]).astype(BF16)
        d["o"] = d["ws"][c:] + _bdot(d["aqk"], d["vn"])
    for d in chunks:
        d["s_ref"][...] = d["s"] * d["dlast"] + lax.dot_general(
            d["kd"].astype(BF16), d["vn"], (((0,), (0,)), ((), ())), preferred_element_type=F32)


def _gdn_kernel(qf_ref, kf_ref, vf_ref, gf_ref, gtf_ref, qb_ref, kb_ref, vb_ref, gb_ref, gtb_ref,
                of_ref, ob_ref, s_ref):
    hg = pl.program_id(1)

    @pl.when(pl.program_id(2) == 0)
    def _():
        s_ref[...] = jnp.zeros_like(s_ref)

    npair = qf_ref.shape[0] // 128
    c = DN_CHUNK

    def pair(p, carry):
        chunks = []
        for rev in (False, True):
            q_ref, k_ref, v_ref, g_ref, gt_ref, o_ref = (
                (qb_ref, kb_ref, vb_ref, gb_ref, gtb_ref, ob_ref) if rev
                else (qf_ref, kf_ref, vf_ref, gf_ref, gtf_ref, of_ref))
            pp = (npair - 1 - p) if rev else p
            base = pl.multiple_of(pp * 128, 128)
            gt = gt_ref[pp]
            sub = lax.broadcasted_iota(jnp.int32, gt.shape, 0)
            for step, half in enumerate((1, 0) if rev else (0, 1)):
                rows = pl.ds(base + half * c, c)
                g = g_ref[rows, :]
                lane = lax.broadcasted_iota(jnp.int32, g.shape, 1)
                for hl in range(REC_HEADS):
                    cols = slice(hl * HEAD_DIM, (hl + 1) * HEAD_DIM)
                    gcol = hg * REC_HEADS + hl + (H_GROUP if rev else 0)
                    chunks.append(dict(
                        rev=rev, step=step, rows=rows, cols=cols, o_ref=o_ref,
                        s_ref=s_ref.at[int(rev), hl],
                        q=q_ref[rows, cols], k=k_ref[rows, cols], v=v_ref[rows, cols],
                        gci=jnp.sum(jnp.where(lane == gcol, g, 0.0), axis=-1, keepdims=True),
                        beta=jnp.sum(jnp.where(lane == gcol + 2 * H_GROUP, g, 0.0), axis=-1,
                                     keepdims=True),
                        gcj=jnp.sum(jnp.where(sub == gcol, gt, 0.0), axis=0,
                                    keepdims=True)[:, half * c:(half + 1) * c]))
        _gdn_chunks_local(chunks)
        for step in (0, 1):
            now = [d for d in chunks if d["step"] == step]
            _gdn_chunks_scan(now)
            for d in now:
                d["o_ref"][d["rows"], d["cols"]] = d["o"]
        return carry

    lax.fori_loop(0, npair, pair, 0)


def _gdn_call(main, side, conv_w, a_log, dt_bias):
    bsz, length, _ = main.shape
    tl = min(REC_TILE, length)
    nt = length // tl
    w = REC_HEADS * HEAD_DIM
    ng = H_GROUP // REC_HEADS
    qkv = _gdn_conv_call(main, conv_w)
    gates, gates_t = _gdn_gates_call(side, a_log, dt_bias)

    def specs(tile):
        return [pl.BlockSpec((None, tl, w), lambda b, g, i: (b, tile(i), g)),
                pl.BlockSpec((None, tl, w), lambda b, g, i: (b, tile(i), ng + g)),
                pl.BlockSpec((None, tl, w), lambda b, g, i: (b, tile(i), 2 * ng + g)),
                pl.BlockSpec((None, tl, SIDE_W), lambda b, g, i: (b, tile(i), 0)),
                pl.BlockSpec((None, tl // 128, 4 * H_GROUP, 128), lambda b, g, i: (b, tile(i), 0, 0))]

    fw = lambda i: i
    bw = lambda i: nt - 1 - i
    return pl.pallas_call(
        _gdn_kernel,
        out_shape=(jax.ShapeDtypeStruct((bsz, length, GROUP_W), F32),) * 2,
        grid=(bsz, ng, nt),
        in_specs=specs(fw) + specs(bw),
        out_specs=(pl.BlockSpec((None, tl, w), lambda b, g, i: (b, fw(i), g)),
                   pl.BlockSpec((None, tl, w), lambda b, g, i: (b, bw(i), g))),
        scratch_shapes=[pltpu.VMEM((2, REC_HEADS, HEAD_DIM, HEAD_DIM), F32)],
        compiler_params=_params(("arbitrary", "arbitrary", "arbitrary")),
        name="gdn_scan",
    )(qkv, qkv, qkv, gates, gates_t, qkv, qkv, qkv, gates, gates_t)


HG_TILE = 64
HG_SUB = 8
HG_NSUB = HG_TILE // HG_SUB


def _tri_cumsum(x, rev):
    n = x.shape[0]
    tri = jnp.where(_tri(n, rev), 1.0, 0.0).astype(BF16)
    hi = x.astype(BF16)
    r1 = x - hi.astype(F32)
    mid = r1.astype(BF16)
    lo = (r1 - mid.astype(F32)).astype(BF16)
    parts = jnp.dot(tri, jnp.concatenate([hi, mid, lo], axis=1), preferred_element_type=F32)
    w = x.shape[1]
    return (parts[:, 0:w] + parts[:, w:2 * w]) + parts[:, 2 * w:]


def _hg_chunks(chains):
    n, sub, nsub = HG_TILE, HG_SUB, HG_NSUB
    row = lax.broadcasted_iota(jnp.int32, (n, HEAD_DIM), 0)
    pos = row % sub
    tn_dims = (((0,), (0,)), ((), ()))
    for c in chains:
        z, loglb, log1mlb = c["z"], c["loglb"], c["log1mlb"]
        c["q"] = c["q_raw"] * jax.nn.sigmoid(c["q_raw"])
        log_sig = jnp.minimum(z, 0.0) - jnp.log1p(jnp.exp(-jnp.abs(z)))
        bb = log1mlb + log_sig
        lf = jnp.maximum(loglb, bb) + jnp.log1p(jnp.exp(-jnp.abs(loglb - bb)))
        b = _tri_cumsum(lf, c["rev"])
        c["b2"] = b * LOG2_E
        c["c2"] = (b - (bb - z)) * LOG2_E
        c["inp16"] = c["inp"].astype(BF16)

    for c in chains:
        b2, c2, rev = c["b2"], c["c2"], c["rev"]
        c["order"] = list(range(nsub - 1, -1, -1)) if rev else list(range(nsub))
        c["ends"], c["contrib"] = [], []
        for j in c["order"]:
            lo = j * sub
            bend = b2[lo:lo + 1, :] if rev else b2[lo + sub - 1:lo + sub, :]
            c["ends"].append(bend)
            if len(c["contrib"]) < nsub - 1:
                ks = jnp.exp2(bend - c2[lo:lo + sub, :])
                c["contrib"].append(lax.dot_general(c["inp16"][lo:lo + sub, :], ks.astype(BF16),
                                                    tn_dims, preferred_element_type=F32))

    for c in chains:
        b2, c2, q, inp, rev = c["b2"], c["c2"], c["q"], c["inp"], c["rev"]
        o = jnp.sum(q * jnp.exp2(b2 - c2), axis=-1, keepdims=True) * inp
        for d in range(1, sub):
            shift = (sub - d) % sub if rev else d
            valid = (pos <= sub - 1 - d) if rev else (pos >= d)
            sh = lambda x: pltpu.roll(x.reshape(nsub, sub, HEAD_DIM), shift, axis=1).reshape(
                n, HEAD_DIM)
            e = jnp.where(valid, jnp.exp2(b2 - sh(c2)), 0.0)
            o = o + jnp.sum(q * e, axis=-1, keepdims=True) * sh(inp)
        c["o"] = o

    for c in chains:
        b2, q, rev = c["b2"], c["q"], c["rev"]
        c["s"] = c["s_ref"][...]
        qext = [q * jnp.exp2(b2)]
        for m in range(nsub - 1):
            j = c["order"][m]
            lo, hi = (0, j * sub) if rev else ((j + 1) * sub, n)
            part = q[lo:hi] * jnp.exp2(b2[lo:hi] - c["ends"][m])
            zeros = jnp.zeros((n - (hi - lo), HEAD_DIM), F32)
            qext.append(jnp.concatenate([part, zeros] if rev else [zeros, part], axis=0))
        ncat = jnp.concatenate([c["s"]] + c["contrib"], axis=1).astype(BF16)
        c["o"] = c["o"] + lax.dot_general(jnp.concatenate(qext, axis=1).astype(BF16), ncat,
                                          (((1,), (1,)), ((), ())), preferred_element_type=F32)
    for c in chains:
        btot = c["ends"][-1]
        kd = jnp.exp2(btot - c["c2"]).astype(BF16)
        c["s_ref"][...] = c["s"] * jnp.exp2(btot) + lax.dot_general(
            c["inp16"], kd, tn_dims, preferred_element_type=F32)


def _hgrn2_kernel(qf_ref, ff_ref, if_ref, qb_ref, fb_ref, ib_ref, lbf_ref, lbb_ref,
                  of_ref, ob_ref, s_ref):
    @pl.when(pl.program_id(2) == 0)
    def _():
        s_ref[...] = jnp.zeros_like(s_ref)

    nstep = qf_ref.shape[0] // HG_TILE

    def step(t, carry):
        chains = []
        for rev in (False, True):
            q_ref, f_ref, i_ref, lb_ref, o_ref = ((qb_ref, fb_ref, ib_ref, lbb_ref, ob_ref) if rev
                                                  else (qf_ref, ff_ref, if_ref, lbf_ref, of_ref))
            tt = (nstep - 1 - t) if rev else t
            rows = pl.ds(pl.multiple_of(tt * HG_TILE, HG_TILE), HG_TILE)
            for hl in range(REC_HEADS):
                cols = slice(hl * HEAD_DIM, (hl + 1) * HEAD_DIM)
                chains.append(dict(rev=rev, rows=rows, cols=cols, o_ref=o_ref,
                                   s_ref=s_ref.at[int(rev), hl], q_raw=q_ref[rows, cols],
                                   z=f_ref[rows, cols], inp=i_ref[rows, cols],
                                   loglb=lb_ref[0:1, cols], log1mlb=lb_ref[1:2, cols]))
        _hg_chunks(chains)
        for c in chains:
            c["o_ref"][c["rows"], c["cols"]] = c["o"]
        return carry

    lax.fori_loop(0, nstep, step, 0)


def _hgrn2_call(main, lb_fw, lb_bw):
    bsz, length, _ = main.shape
    tl = min(REC_TILE, length)
    nt = length // tl
    w = REC_HEADS * HEAD_DIM
    ng = H_GROUP // REC_HEADS
    fw = lambda i: i
    bw = lambda i: nt - 1 - i
    col = lambda group, tile: pl.BlockSpec((None, tl, w), lambda b, g, i: (b, tile(i), group * ng + g))
    lbrow = pl.BlockSpec((2, w), lambda b, g, i: (0, g))
    logs = lambda lb: jnp.stack([jnp.log(lb), jnp.log1p(-lb)])
    return pl.pallas_call(
        _hgrn2_kernel,
        out_shape=(jax.ShapeDtypeStruct((bsz, length, GROUP_W), F32),) * 2,
        grid=(bsz, ng, nt),
        in_specs=[col(0, fw), col(1, fw), col(3, fw), col(0, bw), col(2, bw), col(3, bw), lbrow, lbrow],
        out_specs=(pl.BlockSpec((None, tl, w), lambda b, g, i: (b, fw(i), g)),
                   pl.BlockSpec((None, tl, w), lambda b, g, i: (b, bw(i), g))),
        scratch_shapes=[pltpu.VMEM((2, REC_HEADS, HEAD_DIM, HEAD_DIM), F32)],
        compiler_params=_params(("arbitrary", "arbitrary", "arbitrary")),
        name="hgrn2_scan",
    )(main, main, main, main, main, main, logs(lb_fw), logs(lb_bw))


MLA_QK = 2 * MLA_NOPE
MLA_KEY_BLOCK = 512
ROPE_SWAP = np.concatenate([np.arange(MLA_ROPE // 2, MLA_ROPE), np.arange(MLA_ROPE // 2)])


def _rope_table(length):
    half = MLA_ROPE // 2
    inv = ROPE_BASE ** (-jnp.arange(half, dtype=F32) / half)
    ang = jnp.arange(length, dtype=F32)[:, None] * inv[None, :]
    cos, sin = jnp.cos(ang), jnp.sin(ang)
    return jnp.concatenate([cos, cos, -sin, sin], axis=-1)


def _mla_q_weights(w_uq):
    w = w_uq.reshape(MLA_Q_LORA, H_GROUP, MLA_NOPE + MLA_ROPE)
    rope = w[:, :, MLA_NOPE:]
    w = jnp.concatenate([w[:, :, :MLA_NOPE], rope, rope[:, :, ROPE_SWAP]], axis=-1)
    return w.reshape(MLA_Q_LORA, H_GROUP * MLA_QK).astype(BF16)


def _plain_rmsnorm(x, g):
    return x * lax.rsqrt(jnp.mean(x * x, axis=-1, keepdims=True) + EPS) * g


def _mla_proj_kernel(cq_ref, ckv_ref, side_ref, tab_ref, qn_ref, kvn_ref, wq_ref, wkv_ref,
                     q_out, k_out, v_out):
    scale = (MLA_NOPE + MLA_ROPE) ** -0.5 * LOG2_E
    cqn = _plain_rmsnorm(cq_ref[...], qn_ref[...]).astype(BF16)
    ckvn = _plain_rmsnorm(ckv_ref[...], kvn_ref[...]).astype(BF16)
    q = jnp.dot(cqn, wq_ref[...], preferred_element_type=F32)
    kv = jnp.dot(ckvn, wkv_ref[...], preferred_element_type=F32)
    tab = tab_ref[...]
    kr = (side_ref[...] * tab).astype(BF16)
    for h in range(H_GROUP):
        c0 = h * MLA_QK
        qr = q[:, c0 + MLA_NOPE:c0 + MLA_QK] * tab
        qr = qr + pltpu.roll(qr, MLA_ROPE, axis=1)
        q_out[h, :, 0:MLA_NOPE] = (q[:, c0:c0 + MLA_NOPE] * scale).astype(BF16)
        q_out[h, :, MLA_NOPE:MLA_QK] = (qr * scale).astype(BF16)
        k_out[h, :, 0:MLA_NOPE] = kv[:, c0:c0 + MLA_NOPE].astype(BF16)
        k_out[h, :, MLA_NOPE:MLA_QK] = kr
        v_out[h] = kv[:, c0 + MLA_NOPE:c0 + MLA_QK].astype(BF16)


def _mla_attn_kernel(q_ref, k_ref, v_ref, o_ref):
    length = k_ref.shape[0]
    tk = MLA_KEY_BLOCK
    nk = length // tk
    q = q_ref[...]

    def scores(kb):
        return lax.dot_general(q, k_ref[kb * tk:(kb + 1) * tk, :], (((1,), (1,)), ((), ())),
                               preferred_element_type=F32)

    s_next = scores(0)
    m = l = acc = None
    for kb in range(nk):
        s = s_next
        if kb + 1 < nk:
            s_next = scores(kb + 1)
        m_blk = jnp.max(s, axis=-1, keepdims=True)
        if kb == 0:
            m = m_blk
            p = jnp.exp2(s - m)
            l = jnp.sum(p, axis=-1, keepdims=True)
            acc = jnp.dot(p.astype(BF16), v_ref[0:tk, :], preferred_element_type=F32)
        else:
            m_new = jnp.maximum(m, m_blk)
            alpha = jnp.exp2(m - m_new)
            p = jnp.exp2(s - m_new)
            l = alpha * l + jnp.sum(p, axis=-1, keepdims=True)
            acc = alpha * acc + jnp.dot(p.astype(BF16), v_ref[kb * tk:(kb + 1) * tk, :],
                                        preferred_element_type=F32)
            m = m_new
    o_ref[...] = (acc / l).astype(o_ref.dtype)


def _mla_call(main, side, q_norm, w_uq, kv_norm, w_ukv):
    bsz, length, _ = main.shape
    tm = min(512, length)
    lora = MLA_Q_LORA
    q_all, k_all, v_all = pl.pallas_call(
        _mla_proj_kernel,
        out_shape=(jax.ShapeDtypeStruct((bsz, H_GROUP, length, MLA_QK), BF16),
                   jax.ShapeDtypeStruct((bsz, H_GROUP, length, MLA_QK), BF16),
                   jax.ShapeDtypeStruct((bsz, H_GROUP, length, MLA_V), BF16)),
        grid=(bsz, length // tm),
        in_specs=[
            pl.BlockSpec((None, tm, lora), lambda b, i: (b, i, 10)),
            pl.BlockSpec((None, tm, lora), lambda b, i: (b, i, 11)),
            pl.BlockSpec((None, tm, SIDE_W), lambda b, i: (b, i, 0)),
            pl.BlockSpec((tm, SIDE_W), lambda b, i: (i, 0)),
            pl.BlockSpec((1, lora), lambda b, i: (0, 0)),
            pl.BlockSpec((1, lora), lambda b, i: (0, 0)),
            pl.BlockSpec((lora, H_GROUP * MLA_QK), lambda b, i: (0, 0)),
            pl.BlockSpec((lora, H_GROUP * MLA_QK), lambda b, i: (0, 0)),
        ],
        out_specs=(pl.BlockSpec((None, H_GROUP, tm, MLA_QK), lambda b, i: (b, 0, i, 0)),
                   pl.BlockSpec((None, H_GROUP, tm, MLA_QK), lambda b, i: (b, 0, i, 0)),
                   pl.BlockSpec((None, H_GROUP, tm, MLA_V), lambda b, i: (b, 0, i, 0))),
        compiler_params=_params(("arbitrary", "arbitrary")),
        name="mla_proj",
    )(main, main, side, _rope_table(length), q_norm.reshape(1, lora), kv_norm.reshape(1, lora),
      _mla_q_weights(w_uq), w_ukv.astype(BF16))

    tq = 512
    return pl.pallas_call(
        _mla_attn_kernel,
        out_shape=jax.ShapeDtypeStruct((bsz, length, H_GROUP * MLA_V), BF16),
        grid=(bsz, H_GROUP, length // tq),
        in_specs=[
            pl.BlockSpec((None, None, tq, MLA_QK), lambda b, h, i: (b, h, i, 0)),
            pl.BlockSpec((None, None, length, MLA_QK), lambda b, h, i: (b, h, 0, 0)),
            pl.BlockSpec((None, None, length, MLA_V), lambda b, h, i: (b, h, 0, 0)),
        ],
        out_specs=pl.BlockSpec((None, tq, MLA_V), lambda b, h, i: (b, i, h)),
        compiler_params=_params(("arbitrary", "arbitrary", "arbitrary")),
        name="mla_attn",
    )(q_all, k_all, v_all)


def _pad_side(w):
    return jnp.pad(w, ((0, 0), (0, SIDE_W - w.shape[1]))).astype(BF16)


def _trunk(x, mods, lb, p):
    for l in range(DEPTH):
        mod3 = mods[l]
        if l % 2 == 0:
            e = l // 2
            main, side = _inproj_call(x, mod3, p["norm_mix"][l], p["w_in_ab_main"],
                                      p["w_in_ab_side"], e)
            att = _na_call(main, p["na_rpb"][e])
            o_fw, o_bw = _gdn_call(main, side, p["dn_conv"][e], p["dn_a_log"][e], p["dn_dt_bias"][e])
            x, h2 = _outproj_call(att, o_fw, o_bw, main, 6, p["dn_norm"][e], False,
                                  p["w_out_ab"], e, x, mod3, p["norm_ffn"][l])
        else:
            o = l // 2
            main, side = _inproj_call(x, mod3, p["norm_mix"][l], p["w_in_cd_main"],
                                      p["w_in_cd_side"], o)
            o_fw, o_bw = _hgrn2_call(main, lb[0, l], lb[1, l])
            att = _mla_call(main, side, p["mla_q_norm"][o], p["mla_w_uq"][o], p["mla_kv_norm"][o],
                            p["mla_w_ukv"][o])
            x, h2 = _outproj_call(att, o_fw, o_bw, main, 4, p["hg_norm"][o], True,
                                  p["w_out_cd"], o, x, mod3, p["norm_ffn"][l])
        x = _ffn_split_call(x, h2, mod3, p["ffn_w_gate"], p["ffn_w_up"], p["ffn_conv"],
                            p["ffn_w_down"], l)
    return _final_norm_call(x, p["final_norm"])


def kernel(x_prompt, x_sample, c_prompt, c_sample, norm_mix, norm_ffn, w_ada, b_ada, w_in_ab, w_out_ab, na_rpb, dn_conv, dn_a_log, dn_dt_bias, dn_norm, w_in_cd, w_out_cd, hg_lower_bounds, hg_norm, mla_q_norm, mla_w_uq, mla_kv_norm, mla_w_ukv, ffn_w_gate, ffn_w_up, ffn_conv, ffn_w_down, final_norm):
    n_p, n_s = c_prompt.shape[0], c_sample.shape[0]
    rows = -(-(n_p + n_s) // 8) * 8
    c_all = jnp.concatenate([c_prompt, c_sample, jnp.zeros((rows - n_p - n_s, D_MODEL), F32)], axis=0)
    mod_all = _ada_call(c_all, w_ada, b_ada)
    mods_p = [mod_all[l, :n_p].reshape(n_p, 1, 6 * D_MODEL) for l in range(DEPTH)]
    mods_s = [mod_all[l, n_p:n_p + n_s].reshape(n_s, 1, 6 * D_MODEL) for l in range(DEPTH)]

    lb = jnp.cumsum(jax.nn.softmax(hg_lower_bounds.astype(F32), axis=1), axis=1)
    lb = lb - lb[:, :1]

    p = dict(
        norm_mix=norm_mix, norm_ffn=norm_ffn, na_rpb=na_rpb, dn_conv=dn_conv, dn_a_log=dn_a_log,
        dn_dt_bias=dn_dt_bias, dn_norm=dn_norm, hg_norm=hg_norm, mla_q_norm=mla_q_norm,
        mla_w_uq=mla_w_uq, mla_kv_norm=mla_kv_norm, mla_w_ukv=mla_w_ukv, ffn_conv=ffn_conv,
        final_norm=final_norm,
        w_in_ab_main=w_in_ab[:, :, :AB_MAIN].astype(BF16),
        w_in_ab_side=jax.vmap(_pad_side)(w_in_ab[:, :, AB_MAIN:]),
        w_in_cd_main=w_in_cd[:, :, :CD_MAIN].astype(BF16),
        w_in_cd_side=jnp.concatenate([w_in_cd[:, :, CD_MAIN:],
                                      w_in_cd[:, :, CD_MAIN:][:, :, ROPE_SWAP]], axis=-1).astype(BF16),
        w_out_ab=w_out_ab.astype(BF16), w_out_cd=w_out_cd.astype(BF16),
        ffn_w_gate=ffn_w_gate.astype(BF16), ffn_w_up=ffn_w_up.astype(BF16),
        ffn_w_down=ffn_w_down.astype(BF16),
    )
    y_prompt = _trunk(x_prompt, mods_p, lb, p)
    y_sample = _trunk(x_sample, mods_s, lb, p)
    return (y_prompt, y_sample)
```

```python
import functools
import math

import numpy as np
import jax
import jax.numpy as jnp
from jax import lax
from jax.experimental import pallas as pl
from jax.experimental.pallas import tpu as pltpu

D_MODEL = 2048
DEPTH = 4
HEAD_DIM = 128
H_GROUP = 8
GROUP_W = H_GROUP * HEAD_DIM
GRID_W = 64
NA_WIN_R = 8
NA_WIN_C = 16
DN_CONV_K = 5
DN_CHUNK = 64
MLA_Q_LORA = 512
MLA_KV_LORA = 512
MLA_NOPE = 128
MLA_ROPE = 64
MLA_V = 128
ROPE_BASE = 10000.0
D_FF = 5632
FFN_CONV_K = 3
EPS = 1e-6
AB_MAIN = 7 * GROUP_W
CD_MAIN = 6 * GROUP_W
SIDE_W = 128

F32 = jnp.float32
BF16 = jnp.bfloat16
LOG2_E = 1.4426950408889634

VMEM_LIMIT_BYTES = 56 * 1024 * 1024
HALO = 16
def _params(sem):
    return pltpu.CompilerParams(dimension_semantics=sem, vmem_limit_bytes=VMEM_LIMIT_BYTES)


def _ada_kernel(c_ref, w_ref, b_ref, o_ref):
    c = c_ref[...]
    cond = c * jax.nn.sigmoid(c)
    o_ref[...] = jnp.dot(cond, w_ref[...], precision=lax.Precision.HIGHEST,
                         preferred_element_type=F32) + b_ref[...]


def _ada_call(c_all, w_ada, b_ada):
    rows = c_all.shape[0]
    tn = 1024
    return pl.pallas_call(
        _ada_kernel,
        out_shape=jax.ShapeDtypeStruct((DEPTH, rows, 6 * D_MODEL), F32),
        grid=(DEPTH, 6 * D_MODEL // tn),
        in_specs=[
            pl.BlockSpec((rows, D_MODEL), lambda l, j: (0, 0)),
            pl.BlockSpec((None, D_MODEL, tn), lambda l, j: (l, 0, j)),
            pl.BlockSpec((None, 1, tn), lambda l, j: (l, 0, j)),
        ],
        out_specs=pl.BlockSpec((None, rows, tn), lambda l, j: (l, 0, j)),
        compiler_params=_params(("arbitrary", "arbitrary")),
        name="ada_mod",
    )(c_all, w_ada, b_ada.reshape(DEPTH, 1, 6 * D_MODEL))


def _norm_mod(x, g, sh, sc):
    y = x * lax.rsqrt(jnp.mean(x * x, axis=-1, keepdims=True) + EPS)
    return (y * g) * (1.0 + sc) + sh


def _inproj_kernel(x_ref, g_ref, sh_ref, sc_ref, w_ref, ws_ref, o_ref, os_ref, h_ref):
    @pl.when(pl.program_id(2) == 0)
    def _():
        hb = _norm_mod(x_ref[...], g_ref[...], sh_ref[...], sc_ref[...]).astype(BF16)
        h_ref[...] = hb
        os_ref[...] = jnp.dot(hb, ws_ref[...], preferred_element_type=F32)

    o_ref[...] = jnp.dot(h_ref[...], w_ref[...], preferred_element_type=F32)


def _inproj_call(x, mod3, norm_g, w_main, w_side, e):
    bsz, length, d = x.shape
    n_main = w_main.shape[2] // GROUP_W * GROUP_W
    tm = min(1024, length)
    tn = 1024
    return pl.pallas_call(
        _inproj_kernel,
        out_shape=(jax.ShapeDtypeStruct((bsz, length, n_main), F32),
                   jax.ShapeDtypeStruct((bsz, length, SIDE_W), F32)),
        grid=(bsz, length // tm, n_main // tn),
        in_specs=[
            pl.BlockSpec((None, tm, d), lambda b, i, j: (b, i, 0)),
            pl.BlockSpec((1, d), lambda b, i, j: (0, 0)),
            pl.BlockSpec((None, 1, d), lambda b, i, j: (b, 0, 0)),
            pl.BlockSpec((None, 1, d), lambda b, i, j: (b, 0, 1)),
            pl.BlockSpec((None, d, tn), lambda b, i, j: (e, 0, j)),
            pl.BlockSpec((None, d, SIDE_W), lambda b, i, j: (e, 0, 0)),
        ],
        out_specs=(pl.BlockSpec((None, tm, tn), lambda b, i, j: (b, i, j)),
                   pl.BlockSpec((None, tm, SIDE_W), lambda b, i, j: (b, i, 0))),
        scratch_shapes=[pltpu.VMEM((tm, d), BF16)],
        compiler_params=_params(("arbitrary", "arbitrary", "arbitrary")),
        name="inproj",
    )(x, norm_g.reshape(1, d), mod3, mod3, w_main, w_side)


def _outproj_kernel(rec_first, att_ref, of_ref, ob_ref, z_ref, ng_ref, w_ref, x_ref, gate_ref,
                    g2_ref, sh2_ref, sc2_ref, o_ref, h2_ref, y_ref):
    half = att_ref.shape[-1]
    ng = ng_ref[...]
    for h in range(H_GROUP):
        sl = slice(h * HEAD_DIM, (h + 1) * HEAD_DIM)
        o = of_ref[:, sl] + ob_ref[:, sl]
        o = o * lax.rsqrt(jnp.mean(o * o, axis=-1, keepdims=True) + EPS) * ng
        z = z_ref[:, sl]
        y_ref[:, sl] = (o * (z * jax.nn.sigmoid(z))).astype(BF16)
    rec0, att0 = (0, half) if rec_first else (half, 0)
    y = jnp.dot(y_ref[...], w_ref[rec0:rec0 + half, :], preferred_element_type=F32)
    y = y + jnp.dot(att_ref[...], w_ref[att0:att0 + half, :], preferred_element_type=F32)
    xo = x_ref[...] + gate_ref[...] * y
    o_ref[...] = xo
    h2_ref[...] = _norm_mod(xo, g2_ref[...], sh2_ref[...], sc2_ref[...]).astype(BF16)


def _outproj_call(att, o_fw, o_bw, main, gate_group, norm_g, rec_first, w_out, e, x, mod3, norm_ffn_g):
    bsz, length, d = x.shape
    half = att.shape[-1]
    tm = min(512, length)
    blk = lambda col: pl.BlockSpec((None, tm, half), lambda b, i: (b, i, col))
    row = pl.BlockSpec((None, tm, d), lambda b, i: (b, i, 0))
    mod = lambda k: pl.BlockSpec((None, 1, d), lambda b, i: (b, 0, k))
    return pl.pallas_call(
        functools.partial(_outproj_kernel, rec_first),
        out_shape=(jax.ShapeDtypeStruct(x.shape, F32), jax.ShapeDtypeStruct(x.shape, BF16)),
        grid=(bsz, length // tm),
        in_specs=[
            blk(0), blk(0), blk(0), blk(gate_group),
            pl.BlockSpec((1, HEAD_DIM), lambda b, i: (0, 0)),
            pl.BlockSpec((None, 2 * half, d), lambda b, i: (e, 0, 0)),
            row,
            mod(2),
            pl.BlockSpec((1, d), lambda b, i: (0, 0)),
            mod(3), mod(4),
        ],
        out_specs=(row, row),
        scratch_shapes=[pltpu.VMEM((tm, half), BF16)],
        compiler_params=_params(("arbitrary", "arbitrary")),
        name="outproj",
    )(att, o_fw, o_bw, main, norm_g.reshape(1, HEAD_DIM), w_out, x, mod3,
      norm_ffn_g.reshape(1, d), mod3, mod3)


def _ffn_up_kernel(hp_ref, hm_ref, hn_ref, wg_ref, wu_ref, cw_ref, act_ref, h_ref, gp_ref):
    i = pl.program_id(1)
    tm = hm_ref.shape[0]

    @pl.when(pl.program_id(2) == 0)
    def _():
        h_ref[HALO:HALO + tm, :] = hm_ref[...]
        h_ref[0:HALO, :] = jnp.where(i == 0, 0.0, hp_ref[...]).astype(BF16)
        h_ref[HALO + tm:, :] = jnp.where(i == pl.num_programs(1) - 1, 0.0, hn_ref[...]).astype(BF16)

    gp_ref[...] = jnp.dot(h_ref[...], wg_ref[...], preferred_element_type=F32)
    u = jnp.dot(hm_ref[...], wu_ref[...], preferred_element_type=F32)
    cw = cw_ref[...]
    a = gp_ref[pl.ds(HALO - 1, tm), :] * cw[0:1, :]
    a = a + gp_ref[pl.ds(HALO, tm), :] * cw[1:2, :]
    a = a + gp_ref[pl.ds(HALO + 1, tm), :] * cw[2:3, :]
    act_ref[...] = ((a * jax.nn.sigmoid(a)) * u).astype(BF16)


def _ffn_down_kernel(act_ref, wd_ref, x_ref, gate_ref, o_ref):
    o_ref[...] = x_ref[...] + gate_ref[...] * jnp.dot(act_ref[...], wd_ref[...],
                                                     preferred_element_type=F32)


def _ffn_split_call(x, h2, mod3, w_gate, w_up, conv_w, w_down, l):
    bsz, length, d = x.shape
    tm = min(1024, length)
    tf = 512
    nh = tm // HALO
    last_h = length // HALO - 1
    act = pl.pallas_call(
        _ffn_up_kernel,
        out_shape=jax.ShapeDtypeStruct((bsz, length, D_FF), BF16),
        grid=(bsz, length // tm, D_FF // tf),
        in_specs=[
            pl.BlockSpec((None, HALO, d), lambda b, i, j: (b, jnp.maximum(i * nh - 1, 0), 0)),
            pl.BlockSpec((None, tm, d), lambda b, i, j: (b, i, 0)),
            pl.BlockSpec((None, HALO, d), lambda b, i, j: (b, jnp.minimum((i + 1) * nh, last_h), 0)),
            pl.BlockSpec((None, d, tf), lambda b, i, j: (l, 0, j)),
            pl.BlockSpec((None, d, tf), lambda b, i, j: (l, 0, j)),
            pl.BlockSpec((None, FFN_CONV_K, tf), lambda b, i, j: (l, 0, j)),
        ],
        out_specs=pl.BlockSpec((None, tm, tf), lambda b, i, j: (b, i, j)),
        scratch_shapes=[pltpu.VMEM((tm + 2 * HALO, d), BF16),
                        pltpu.VMEM((tm + 2 * HALO, tf), F32)],
        compiler_params=_params(("arbitrary", "arbitrary", "arbitrary")),
        name="ffn_up",
    )(h2, h2, h2, w_gate, w_up, conv_w)

    tn = 512
    gate_blk = 5 * (d // tn)
    return pl.pallas_call(
        _ffn_down_kernel,
        out_shape=jax.ShapeDtypeStruct(x.shape, F32),
        grid=(bsz, length // tm, d // tn),
        in_specs=[
            pl.BlockSpec((None, tm, D_FF), lambda b, i, n: (b, i, 0)),
            pl.BlockSpec((None, D_FF, tn), lambda b, i, n: (l, 0, n)),
            pl.BlockSpec((None, tm, tn), lambda b, i, n: (b, i, n)),
            pl.BlockSpec((None, 1, tn), lambda b, i, n: (b, 0, gate_blk + n)),
        ],
        out_specs=pl.BlockSpec((None, tm, tn), lambda b, i, n: (b, i, n)),
        compiler_params=_params(("arbitrary", "arbitrary", "arbitrary")),
        name="ffn_down",
    )(act, w_down, x, mod3)


def _final_norm_kernel(x_ref, g_ref, o_ref):
    o_ref[...] = _plain_rmsnorm(x_ref[...], g_ref[...])


def _final_norm_call(x, g):
    bsz, length, d = x.shape
    tm = min(1024, length)
    return pl.pallas_call(
        _final_norm_kernel,
        out_shape=jax.ShapeDtypeStruct(x.shape, F32),
        grid=(bsz, length // tm),
        in_specs=[pl.BlockSpec((None, tm, d), lambda b, i: (b, i, 0)),
                  pl.BlockSpec((1, d), lambda b, i: (0, 0))],
        out_specs=pl.BlockSpec((None, tm, d), lambda b, i: (b, i, 0)),
        compiler_params=_params(("arbitrary", "arbitrary")),
        name="final_norm",
    )(x, g.reshape(1, d))


NA_KEYS = NA_WIN_R * GRID_W
MASKED = -1e30
NA_ROWS_PER_STEP = 32


def _na_bias_table(rpb):
    qc = np.arange(GRID_W)[:, None]
    kc = np.arange(GRID_W)[None, :]
    cs = np.clip(qc - NA_WIN_C // 2, 0, GRID_W - NA_WIN_C)
    valid = (kc >= cs) & (kc < cs + NA_WIN_C)
    dc = np.clip(kc - qc + NA_WIN_C - 1, 0, 2 * NA_WIN_C - 2)
    dr = np.arange(NA_WIN_R)[None, :] - np.arange(NA_WIN_R)[:, None] + NA_WIN_R - 1
    t = rpb.astype(F32)[:, dr][:, :, :, dc]
    t = jnp.where(valid[None, None, None], t * LOG2_E, MASKED)
    return t.transpose(0, 1, 3, 2, 4).reshape(rpb.shape[0], NA_WIN_R, GRID_W, NA_KEYS)


def _na_kernel(q_ref, k_ref, v_ref, bias_ref, o_ref, kb_ref, vb_ref):
    rows = q_ref.shape[0] // GRID_W
    kb_ref[...] = k_ref[...].astype(BF16)
    vb_ref[...] = v_ref[...].astype(BF16)

    def row_group(g, carry):
        todo = []
        for u in range(NA_ROWS_PER_STEP):
            r = g * NA_ROWS_PER_STEP + u
            rs = jnp.clip(r - NA_WIN_R // 2, 0, rows - NA_WIN_R)
            q0 = pl.multiple_of(r * GRID_W, GRID_W)
            k0 = pl.multiple_of(rs * GRID_W, GRID_W)
            q = (q_ref[pl.ds(q0, GRID_W), :] * (HEAD_DIM ** -0.5 * LOG2_E)).astype(BF16)
            s = lax.dot_general(q, kb_ref[pl.ds(k0, NA_KEYS), :], (((1,), (1,)), ((), ())),
                                preferred_element_type=F32)
            todo.append((q0, k0, s, r - rs))
        for u, (q0, k0, s, off) in enumerate(todo):
            s = s + bias_ref[off]
            p = jnp.exp2(s - jnp.max(s, axis=-1, keepdims=True))
            todo[u] = (q0, k0, p.astype(BF16), jnp.sum(p, axis=-1, keepdims=True))
        for u, (q0, k0, p, l) in enumerate(todo):
            todo[u] = (q0, jnp.dot(p, vb_ref[pl.ds(k0, NA_KEYS), :], preferred_element_type=F32), l)
        for q0, o, l in todo:
            o_ref[pl.ds(q0, GRID_W), :] = (o / l).astype(o_ref.dtype)
        return carry

    lax.fori_loop(0, rows // NA_ROWS_PER_STEP, row_group, 0)


def _na_call(main, rpb):
    bsz, length, _ = main.shape
    assert length % (GRID_W * NA_ROWS_PER_STEP) == 0 and length // GRID_W >= NA_WIN_R
    hd = HEAD_DIM
    return pl.pallas_call(
        _na_kernel,
        out_shape=jax.ShapeDtypeStruct((bsz, length, GROUP_W), BF16),
        grid=(bsz, H_GROUP),
        in_specs=[
            pl.BlockSpec((None, length, hd), lambda b, h: (b, 0, h)),
            pl.BlockSpec((None, length, hd), lambda b, h: (b, 0, H_GROUP + h)),
            pl.BlockSpec((None, length, hd), lambda b, h: (b, 0, 2 * H_GROUP + h)),
            pl.BlockSpec((None, NA_WIN_R, GRID_W, NA_KEYS), lambda b, h: (h, 0, 0, 0)),
        ],
        out_specs=pl.BlockSpec((None, length, hd), lambda b, h: (b, 0, h)),
        scratch_shapes=[pltpu.VMEM((length, hd), BF16), pltpu.VMEM((length, hd), BF16)],
        compiler_params=_params(("arbitrary", "arbitrary")),
        name="nbr_attn",
    )(main, main, main, _na_bias_table(rpb))


REC_TILE = 512
REC_HEADS = 8
HIGHEST = lax.Precision.HIGHEST


def _tri(n, upper):
    r = lax.broadcasted_iota(jnp.int32, (n, n), 0)
    c = lax.broadcasted_iota(jnp.int32, (n, n), 1)
    return (c >= r) if upper else (c <= r)


def _gdn_conv_kernel(x_ref, w_ref, o_ref, xp_ref):
    length, cw = x_ref.shape
    col0 = pl.program_id(1) * cw
    pad = 8
    xp_ref[0:pad, :] = jnp.zeros((pad, cw), F32)
    xp_ref[pad + length:, :] = jnp.zeros((pad, cw), F32)
    xp_ref[pad:pad + length, :] = x_ref[...]
    w = w_ref[...]
    y = xp_ref[pl.ds(pad - DN_CONV_K // 2, length), :] * w[0:1, :]
    for t in range(1, DN_CONV_K):
        y = y + xp_ref[pl.ds(pad - DN_CONV_K // 2 + t, length), :] * w[t:t + 1, :]
    y = y * jax.nn.sigmoid(y)
    qscale = jnp.where(col0 < GROUP_W, HEAD_DIM ** -0.5, 1.0)
    for h in range(cw // HEAD_DIM):
        yh = y[:, h * HEAD_DIM:(h + 1) * HEAD_DIM]
        nh = yh * (lax.rsqrt(jnp.sum(yh * yh, axis=-1, keepdims=True) + EPS) * qscale)
        o_ref[:, h * HEAD_DIM:(h + 1) * HEAD_DIM] = jnp.where(col0 < 2 * GROUP_W, nh, yh)


def _gdn_conv_call(main, conv_w):
    bsz, length, _ = main.shape
    cw = HEAD_DIM
    nblk = 3 * GROUP_W // cw
    return pl.pallas_call(
        _gdn_conv_kernel,
        out_shape=jax.ShapeDtypeStruct((bsz, length, 3 * GROUP_W), F32),
        grid=(bsz, nblk),
        in_specs=[pl.BlockSpec((None, length, cw), lambda b, j: (b, 0, nblk + j)),
                  pl.BlockSpec((DN_CONV_K, cw), lambda b, j: (0, j))],
        out_specs=pl.BlockSpec((None, length, cw), lambda b, j: (b, 0, j)),
        scratch_shapes=[pltpu.VMEM((length + 16, cw), F32)],
        compiler_params=_params(("arbitrary", "arbitrary")),
        name="gdn_conv",
    )(main, conv_w)


def _gdn_gates_kernel(s_ref, alog_ref, dt_ref, g_ref, gt_ref):
    tm = s_ref.shape[0]
    s = s_ref[...]
    x = s + dt_ref[...]
    softplus = jnp.maximum(x, 0.0) + jnp.log1p(jnp.exp(-jnp.abs(x)))
    g = -jnp.exp(alog_ref[...]) * softplus
    r = lax.broadcasted_iota(jnp.int32, (tm, tm), 0)
    c = lax.broadcasted_iota(jnp.int32, (tm, tm), 1)
    same = (r // DN_CHUNK) == (c // DN_CHUNK)
    lo = jnp.where(same & (c <= r), 1.0, 0.0)
    up = jnp.where(same & (c >= r), 1.0, 0.0)
    gcf = jnp.dot(lo, g, precision=HIGHEST, preferred_element_type=F32)
    gcb = jnp.dot(up, g, precision=HIGHEST, preferred_element_type=F32)
    lane = lax.broadcasted_iota(jnp.int32, s.shape, 1)
    out = jnp.where(lane < H_GROUP, gcf, jnp.where(lane < 2 * H_GROUP, gcb, jax.nn.sigmoid(s)))
    g_ref[...] = out
    for p in range(tm // 128):
        gt_ref[p] = out[p * 128:(p + 1) * 128, :].T[0:4 * H_GROUP, :]


def _gdn_gates_call(side, a_log, dt_bias):
    bsz, length, _ = side.shape
    tm = min(512, length)
    row = lambda t: jnp.pad(t.reshape(1, 2 * H_GROUP).astype(F32), ((0, 0), (0, SIDE_W - 2 * H_GROUP)))
    return pl.pallas_call(
        _gdn_gates_kernel,
        out_shape=(jax.ShapeDtypeStruct((bsz, length, SIDE_W), F32),
                   jax.ShapeDtypeStruct((bsz, length // 128, 4 * H_GROUP, 128), F32)),
        grid=(bsz, length // tm),
        in_specs=[pl.BlockSpec((None, tm, SIDE_W), lambda b, i: (b, i, 0)),
                  pl.BlockSpec((1, SIDE_W), lambda b, i: (0, 0)),
                  pl.BlockSpec((1, SIDE_W), lambda b, i: (0, 0))],
        out_specs=(pl.BlockSpec((None, tm, SIDE_W), lambda b, i: (b, i, 0)),
                   pl.BlockSpec((None, tm // 128, 4 * H_GROUP, 128), lambda b, i: (b, i, 0, 0))),
        compiler_params=_params(("arbitrary", "arbitrary")),
        name="gdn_gates",
    )(side, row(a_log), row(dt_bias))


def _bdot(a, b):
    return jnp.dot(a.astype(BF16), b.astype(BF16), preferred_element_type=F32)


def _gdn_chunks_local(chunks):
    c = DN_CHUNK
    r = lax.broadcasted_iota(jnp.int32, (c, c), 0)
    cc = lax.broadcasted_iota(jnp.int32, (c, c), 1)
    for d in chunks:
        incl = (cc >= r) if d["rev"] else (cc <= r)
        d["gam"] = jnp.exp(jnp.where(incl, d["gci"] - d["gcj"], MASKED))
        d["kb"] = d["k"] * d["beta"]
    for d in chunks:
        kq = jnp.concatenate([d["kb"], d["q"]], axis=0).astype(BF16)
        d["prod"] = lax.dot_general(kq, d["k"].astype(BF16), (((1,), (1,)), ((), ())),
                                    preferred_element_type=F32)
    for d in chunks:
        strict = (cc > r) if d["rev"] else (cc < r)
        d["aqk"] = d["prod"][c:] * d["gam"]
        d["p"] = -jnp.where(strict, d["prod"][0:c] * d["gam"], 0.0)
        d["t"] = jnp.where(r == cc, 1.0, 0.0) + d["p"]
    for _ in range(int(math.log2(c)) - 1):
        for d in chunks:
            d["p"] = _bdot(d["p"], d["p"])
        for d in chunks:
            d["t"] = d["t"] + _bdot(d["t"], d["p"])
    for d in chunks:
        egc = jnp.exp(d["gci"])
        rhs = jnp.concatenate([d["v"] * d["beta"], d["kb"] * egc], axis=1)
        uw = _bdot(d["t"], rhs)
        d["u"], d["w"] = uw[:, 0:HEAD_DIM], uw[:, HEAD_DIM:]
        glast = d["gci"][0:1, :] if d["rev"] else d["gci"][c - 1:c, :]
        d["qd"] = d["q"] * egc
        d["kd"] = d["k"] * jnp.exp(glast - d["gci"])
        d["dlast"] = jnp.exp(glast)


def _gdn_chunks_scan(chunks):
    c = DN_CHUNK
    for d in chunks:
        d["s"] = d["s_ref"][...]
        d["ws"] = _bdot(jnp.concatenate([d["w"], d["qd"]], axis=0), d["s"])
    for d in chunks:
        d["vn"] = (d["u"] - d["ws"][0:c]).astype(BF16)
        d["o"] = d["ws"][c:] + _bdot(d["aqk"], d["vn"])
    for d in chunks:
        d["s_ref"][...] = d["s"] * d["dlast"] + lax.dot_general(
            d["kd"].astype(BF16), d["vn"], (((0,), (0,)), ((), ())), preferred_element_type=F32)


def _gdn_kernel(qf_ref, kf_ref, vf_ref, gf_ref, gtf_ref, qb_ref, kb_ref, vb_ref, gb_ref, gtb_ref,
                of_ref, ob_ref, s_ref):
    hg = pl.program_id(1)

    @pl.when(pl.program_id(2) == 0)
    def _():
        s_ref[...] = jnp.zeros_like(s_ref)

    npair = qf_ref.shape[0] // 128
    c = DN_CHUNK

    def pair(p, carry):
        chunks = []
        for rev in (False, True):
            q_ref, k_ref, v_ref, g_ref, gt_ref, o_ref = (
                (qb_ref, kb_ref, vb_ref, gb_ref, gtb_ref, ob_ref) if rev
                else (qf_ref, kf_ref, vf_ref, gf_ref, gtf_ref, of_ref))
            pp = (npair - 1 - p) if rev else p
            base = pl.multiple_of(pp * 128, 128)
            gt = gt_ref[pp]
            sub = lax.broadcasted_iota(jnp.int32, gt.shape, 0)
            for step, half in enumerate((1, 0) if rev else (0, 1)):
                rows = pl.ds(base + half * c, c)
                g = g_ref[rows, :]
                lane = lax.broadcasted_iota(jnp.int32, g.shape, 1)
                for hl in range(REC_HEADS):
                    cols = slice(hl * HEAD_DIM, (hl + 1) * HEAD_DIM)
                    gcol = hg * REC_HEADS + hl + (H_GROUP if rev else 0)
                    chunks.append(dict(
                        rev=rev, step=step, rows=rows, cols=cols, o_ref=o_ref,
                        s_ref=s_ref.at[int(rev), hl],
                        q=q_ref[rows, cols], k=k_ref[rows, cols], v=v_ref[rows, cols],
                        gci=jnp.sum(jnp.where(lane == gcol, g, 0.0), axis=-1, keepdims=True),
                        beta=jnp.sum(jnp.where(lane == gcol + 2 * H_GROUP, g, 0.0), axis=-1,
                                     keepdims=True),
                        gcj=jnp.sum(jnp.where(sub == gcol, gt, 0.0), axis=0,
                                    keepdims=True)[:, half * c:(half + 1) * c]))
        _gdn_chunks_local(chunks)
        for step in (0, 1):
            now = [d for d in chunks if d["step"] == step]
            _gdn_chunks_scan(now)
            for d in now:
                d["o_ref"][d["rows"], d["cols"]] = d["o"]
        return carry

    lax.fori_loop(0, npair, pair, 0)


def _gdn_call(main, side, conv_w, a_log, dt_bias):
    bsz, length, _ = main.shape
    tl = min(REC_TILE, length)
    nt = length // tl
    w = REC_HEADS * HEAD_DIM
    ng = H_GROUP // REC_HEADS
    qkv = _gdn_conv_call(main, conv_w)
    gates, gates_t = _gdn_gates_call(side, a_log, dt_bias)

    def specs(tile):
        return [pl.BlockSpec((None, tl, w), lambda b, g, i: (b, tile(i), g)),
                pl.BlockSpec((None, tl, w), lambda b, g, i: (b, tile(i), ng + g)),
                pl.BlockSpec((None, tl, w), lambda b, g, i: (b, tile(i), 2 * ng + g)),
                pl.BlockSpec((None, tl, SIDE_W), lambda b, g, i: (b, tile(i), 0)),
                pl.BlockSpec((None, tl // 128, 4 * H_GROUP, 128), lambda b, g, i: (b, tile(i), 0, 0))]

    fw = lambda i: i
    bw = lambda i: nt - 1 - i
    return pl.pallas_call(
        _gdn_kernel,
        out_shape=(jax.ShapeDtypeStruct((bsz, length, GROUP_W), F32),) * 2,
        grid=(bsz, ng, nt),
        in_specs=specs(fw) + specs(bw),
        out_specs=(pl.BlockSpec((None, tl, w), lambda b, g, i: (b, fw(i), g)),
                   pl.BlockSpec((None, tl, w), lambda b, g, i: (b, bw(i), g))),
        scratch_shapes=[pltpu.VMEM((2, REC_HEADS, HEAD_DIM, HEAD_DIM), F32)],
        compiler_params=_params(("arbitrary", "arbitrary", "arbitrary")),
        name="gdn_scan",
    )(qkv, qkv, qkv, gates, gates_t, qkv, qkv, qkv, gates, gates_t)


HG_TILE = 64
HG_SUB = 8
HG_NSUB = HG_TILE // HG_SUB


def _tri_cumsum(x, rev):
    n = x.shape[0]
    tri = jnp.where(_tri(n, rev), 1.0, 0.0).astype(BF16)
    hi = x.astype(BF16)
    r1 = x - hi.astype(F32)
    mid = r1.astype(BF16)
    lo = (r1 - mid.astype(F32)).astype(BF16)
    parts = jnp.dot(tri, jnp.concatenate([hi, mid, lo], axis=1), preferred_element_type=F32)
    w = x.shape[1]
    return (parts[:, 0:w] + parts[:, w:2 * w]) + parts[:, 2 * w:]


def _hg_chunks(chains):
    n, sub, nsub = HG_TILE, HG_SUB, HG_NSUB
    row = lax.broadcasted_iota(jnp.int32, (n, HEAD_DIM), 0)
    pos = row % sub
    tn_dims = (((0,), (0,)), ((), ()))
    for c in chains:
        z, loglb, log1mlb = c["z"], c["loglb"], c["log1mlb"]
        c["q"] = c["q_raw"] * jax.nn.sigmoid(c["q_raw"])
        log_sig = jnp.minimum(z, 0.0) - jnp.log1p(jnp.exp(-jnp.abs(z)))
        bb = log1mlb + log_sig
        lf = jnp.maximum(loglb, bb) + jnp.log1p(jnp.exp(-jnp.abs(loglb - bb)))
        b = _tri_cumsum(lf, c["rev"])
        c["b2"] = b * LOG2_E
        c["c2"] = (b - (bb - z)) * LOG2_E
        c["inp16"] = c["inp"].astype(BF16)

    for c in chains:
        b2, c2, rev = c["b2"], c["c2"], c["rev"]
        c["order"] = list(range(nsub - 1, -1, -1)) if rev else list(range(nsub))
        c["ends"], c["contrib"] = [], []
        for j in c["order"]:
            lo = j * sub
            bend = b2[lo:lo + 1, :] if rev else b2[lo + sub - 1:lo + sub, :]
            c["ends"].append(bend)
            if len(c["contrib"]) < nsub - 1:
                ks = jnp.exp2(bend - c2[lo:lo + sub, :])
                c["contrib"].append(lax.dot_general(c["inp16"][lo:lo + sub, :], ks.astype(BF16),
                                                    tn_dims, preferred_element_type=F32))

    for c in chains:
        b2, c2, q, inp, rev = c["b2"], c["c2"], c["q"], c["inp"], c["rev"]
        o = jnp.sum(q * jnp.exp2(b2 - c2), axis=-1, keepdims=True) * inp
        for d in range(1, sub):
            shift = (sub - d) % sub if rev else d
            valid = (pos <= sub - 1 - d) if rev else (pos >= d)
            sh = lambda x: pltpu.roll(x.reshape(nsub, sub, HEAD_DIM), shift, axis=1).reshape(
                n, HEAD_DIM)
            e = jnp.where(valid, jnp.exp2(b2 - sh(c2)), 0.0)
            o = o + jnp.sum(q * e, axis=-1, keepdims=True) * sh(inp)
        c["o"] = o

    for c in chains:
        b2, q, rev = c["b2"], c["q"], c["rev"]
        c["s"] = c["s_ref"][...]
        qext = [q * jnp.exp2(b2)]
        for m in range(nsub - 1):
            j = c["order"][m]
            lo, hi = (0, j * sub) if rev else ((j + 1) * sub, n)
            part = q[lo:hi] * jnp.exp2(b2[lo:hi] - c["ends"][m])
            zeros = jnp.zeros((n - (hi - lo), HEAD_DIM), F32)
            qext.append(jnp.concatenate([part, zeros] if rev else [zeros, part], axis=0))
        ncat = jnp.concatenate([c["s"]] + c["contrib"], axis=1).astype(BF16)
        c["o"] = c["o"] + lax.dot_general(jnp.concatenate(qext, axis=1).astype(BF16), ncat,
                                          (((1,), (1,)), ((), ())), preferred_element_type=F32)
    for c in chains:
        btot = c["ends"][-1]
        kd = jnp.exp2(btot - c["c2"]).astype(BF16)
        c["s_ref"][...] = c["s"] * jnp.exp2(btot) + lax.dot_general(
            c["inp16"], kd, tn_dims, preferred_element_type=F32)


def _hgrn2_kernel(qf_ref, ff_ref, if_ref, qb_ref, fb_ref, ib_ref, lbf_ref, lbb_ref,
                  of_ref, ob_ref, s_ref):
    @pl.when(pl.program_id(2) == 0)
    def _():
        s_ref[...] = jnp.zeros_like(s_ref)

    nstep = qf_ref.shape[0] // HG_TILE

    def step(t, carry):
        chains = []
        for rev in (False, True):
            q_ref, f_ref, i_ref, lb_ref, o_ref = ((qb_ref, fb_ref, ib_ref, lbb_ref, ob_ref) if rev
                                                  else (qf_ref, ff_ref, if_ref, lbf_ref, of_ref))
            tt = (nstep - 1 - t) if rev else t
            rows = pl.ds(pl.multiple_of(tt * HG_TILE, HG_TILE), HG_TILE)
            for hl in range(REC_HEADS):
                cols = slice(hl * HEAD_DIM, (hl + 1) * HEAD_DIM)
                chains.append(dict(rev=rev, rows=rows, cols=cols, o_ref=o_ref,
                                   s_ref=s_ref.at[int(rev), hl], q_raw=q_ref[rows, cols],
                                   z=f_ref[rows, cols], inp=i_ref[rows, cols],
                                   loglb=lb_ref[0:1, cols], log1mlb=lb_ref[1:2, cols]))
        _hg_chunks(chains)
        for c in chains:
            c["o_ref"][c["rows"], c["cols"]] = c["o"]
        return carry

    lax.fori_loop(0, nstep, step, 0)


def _hgrn2_call(main, lb_fw, lb_bw):
    bsz, length, _ = main.shape
    tl = min(REC_TILE, length)
    nt = length // tl
    w = REC_HEADS * HEAD_DIM
    ng = H_GROUP // REC_HEADS
    fw = lambda i: i
    bw = lambda i: nt - 1 - i
    col = lambda group, tile: pl.BlockSpec((None, tl, w), lambda b, g, i: (b, tile(i), group * ng + g))
    lbrow = pl.BlockSpec((2, w), lambda b, g, i: (0, g))
    logs = lambda lb: jnp.stack([jnp.log(lb), jnp.log1p(-lb)])
    return pl.pallas_call(
        _hgrn2_kernel,
        out_shape=(jax.ShapeDtypeStruct((bsz, length, GROUP_W), F32),) * 2,
        grid=(bsz, ng, nt),
        in_specs=[col(0, fw), col(1, fw), col(3, fw), col(0, bw), col(2, bw), col(3, bw), lbrow, lbrow],
        out_specs=(pl.BlockSpec((None, tl, w), lambda b, g, i: (b, fw(i), g)),
                   pl.BlockSpec((None, tl, w), lambda b, g, i: (b, bw(i), g))),
        scratch_shapes=[pltpu.VMEM((2, REC_HEADS, HEAD_DIM, HEAD_DIM), F32)],
        compiler_params=_params(("arbitrary", "arbitrary", "arbitrary")),
        name="hgrn2_scan",
    )(main, main, main, main, main, main, logs(lb_fw), logs(lb_bw))


MLA_QK = 2 * MLA_NOPE
MLA_KEY_BLOCK = 512
ROPE_SWAP = np.concatenate([np.arange(MLA_ROPE // 2, MLA_ROPE), np.arange(MLA_ROPE // 2)])


def _rope_table(length):
    half = MLA_ROPE // 2
    inv = ROPE_BASE ** (-jnp.arange(half, dtype=F32) / half)
    ang = jnp.arange(length, dtype=F32)[:, None] * inv[None, :]
    cos, sin = jnp.cos(ang), jnp.sin(ang)
    return jnp.concatenate([cos, cos, -sin, sin], axis=-1)


def _mla_q_weights(w_uq):
    w = w_uq.reshape(MLA_Q_LORA, H_GROUP, MLA_NOPE + MLA_ROPE)
    rope = w[:, :, MLA_NOPE:]
    w = jnp.concatenate([w[:, :, :MLA_NOPE], rope, rope[:, :, ROPE_SWAP]], axis=-1)
    return w.reshape(MLA_Q_LORA, H_GROUP * MLA_QK).astype(BF16)


def _plain_rmsnorm(x, g):
    return x * lax.rsqrt(jnp.mean(x * x, axis=-1, keepdims=True) + EPS) * g


def _mla_proj_kernel(cq_ref, ckv_ref, side_ref, tab_ref, qn_ref, kvn_ref, wq_ref, wkv_ref,
                     q_out, k_out, v_out):
    scale = (MLA_NOPE + MLA_ROPE) ** -0.5 * LOG2_E
    cqn = _plain_rmsnorm(cq_ref[...], qn_ref[...]).astype(BF16)
    ckvn = _plain_rmsnorm(ckv_ref[...], kvn_ref[...]).astype(BF16)
    q = jnp.dot(cqn, wq_ref[...], preferred_element_type=F32)
    kv = jnp.dot(ckvn, wkv_ref[...], preferred_element_type=F32)
    tab = tab_ref[...]
    kr = (side_ref[...] * tab).astype(BF16)
    for h in range(H_GROUP):
        c0 = h * MLA_QK
        qr = q[:, c0 + MLA_NOPE:c0 + MLA_QK] * tab
        qr = qr + pltpu.roll(qr, MLA_ROPE, axis=1)
        q_out[h, :, 0:MLA_NOPE] = (q[:, c0:c0 + MLA_NOPE] * scale).astype(BF16)
        q_out[h, :, MLA_NOPE:MLA_QK] = (qr * scale).astype(BF16)
        k_out[h, :, 0:MLA_NOPE] = kv[:, c0:c0 + MLA_NOPE].astype(BF16)
        k_out[h, :, MLA_NOPE:MLA_QK] = kr
        v_out[h] = kv[:, c0 + MLA_NOPE:c0 + MLA_QK].astype(BF16)


def _mla_attn_kernel(q_ref, k_ref, v_ref, o_ref):
    length = k_ref.shape[0]
    tk = MLA_KEY_BLOCK
    nk = length // tk
    q = q_ref[...]

    def scores(kb):
        return lax.dot_general(q, k_ref[kb * tk:(kb + 1) * tk, :], (((1,), (1,)), ((), ())),
                               preferred_element_type=F32)

    s_next = scores(0)
    m = l = acc = None
    for kb in range(nk):
        s = s_next
        if kb + 1 < nk:
            s_next = scores(kb + 1)
        m_blk = jnp.max(s, axis=-1, keepdims=True)
        if kb == 0:
            m = m_blk
            p = jnp.exp2(s - m)
            l = jnp.sum(p, axis=-1, keepdims=True)
            acc = jnp.dot(p.astype(BF16), v_ref[0:tk, :], preferred_element_type=F32)
        else:
            m_new = jnp.maximum(m, m_blk)
            alpha = jnp.exp2(m - m_new)
            p = jnp.exp2(s - m_new)
            l = alpha * l + jnp.sum(p, axis=-1, keepdims=True)
            acc = alpha * acc + jnp.dot(p.astype(BF16), v_ref[kb * tk:(kb + 1) * tk, :],
                                        preferred_element_type=F32)
            m = m_new
    o_ref[...] = (acc / l).astype(o_ref.dtype)


def _mla_call(main, side, q_norm, w_uq, kv_norm, w_ukv):
    bsz, length, _ = main.shape
    tm = min(512, length)
    lora = MLA_Q_LORA
    q_all, k_all, v_all = pl.pallas_call(
        _mla_proj_kernel,
        out_shape=(jax.ShapeDtypeStruct((bsz, H_GROUP, length, MLA_QK), BF16),
                   jax.ShapeDtypeStruct((bsz, H_GROUP, length, MLA_QK), BF16),
                   jax.ShapeDtypeStruct((bsz, H_GROUP, length, MLA_V), BF16)),
        grid=(bsz, length // tm),
        in_specs=[
            pl.BlockSpec((None, tm, lora), lambda b, i: (b, i, 10)),
            pl.BlockSpec((None, tm, lora), lambda b, i: (b, i, 11)),
            pl.BlockSpec((None, tm, SIDE_W), lambda b, i: (b, i, 0)),
            pl.BlockSpec((tm, SIDE_W), lambda b, i: (i, 0)),
            pl.BlockSpec((1, lora), lambda b, i: (0, 0)),
            pl.BlockSpec((1, lora), lambda b, i: (0, 0)),
            pl.BlockSpec((lora, H_GROUP * MLA_QK), lambda b, i: (0, 0)),
            pl.BlockSpec((lora, H_GROUP * MLA_QK), lambda b, i: (0, 0)),
        ],
        out_specs=(pl.BlockSpec((None, H_GROUP, tm, MLA_QK), lambda b, i: (b, 0, i, 0)),
                   pl.BlockSpec((None, H_GROUP, tm, MLA_QK), lambda b, i: (b, 0, i, 0)),
                   pl.BlockSpec((None, H_GROUP, tm, MLA_V), lambda b, i: (b, 0, i, 0))),
        compiler_params=_params(("arbitrary", "arbitrary")),
        name="mla_proj",
    )(main, main, side, _rope_table(length), q_norm.reshape(1, lora), kv_norm.reshape(1, lora),
      _mla_q_weights(w_uq), w_ukv.astype(BF16))

    tq = 1024
    return pl.pallas_call(
        _mla_attn_kernel,
        out_shape=jax.ShapeDtypeStruct((bsz, length, H_GROUP * MLA_V), BF16),
        grid=(bsz, H_GROUP, length // tq),
        in_specs=[
            pl.BlockSpec((None, None, tq, MLA_QK), lambda b, h, i: (b, h, i, 0)),
            pl.BlockSpec((None, None, length, MLA_QK), lambda b, h, i: (b, h, 0, 0)),
            pl.BlockSpec((None, None, length, MLA_V), lambda b, h, i: (b, h, 0, 0)),
        ],
        out_specs=pl.BlockSpec((None, tq, MLA_V), lambda b, h, i: (b, i, h)),
        compiler_params=_params(("arbitrary", "arbitrary", "arbitrary")),
        name="mla_attn",
    )(q_all, k_all, v_all)


def _pad_side(w):
    return jnp.pad(w, ((0, 0), (0, SIDE_W - w.shape[1]))).astype(BF16)


def _trunk(x, mods, lb, p):
    for l in range(DEPTH):
        mod3 = mods[l]
        if l % 2 == 0:
            e = l // 2
            main, side = _inproj_call(x, mod3, p["norm_mix"][l], p["w_in_ab_main"],
                                      p["w_in_ab_side"], e)
            att = _na_call(main, p["na_rpb"][e])
            o_fw, o_bw = _gdn_call(main, side, p["dn_conv"][e], p["dn_a_log"][e], p["dn_dt_bias"][e])
            x, h2 = _outproj_call(att, o_fw, o_bw, main, 6, p["dn_norm"][e], False,
                                  p["w_out_ab"], e, x, mod3, p["norm_ffn"][l])
        else:
            o = l // 2
            main, side = _inproj_call(x, mod3, p["norm_mix"][l], p["w_in_cd_main"],
                                      p["w_in_cd_side"], o)
            o_fw, o_bw = _hgrn2_call(main, lb[0, l], lb[1, l])
            att = _mla_call(main, side, p["mla_q_norm"][o], p["mla_w_uq"][o], p["mla_kv_norm"][o],
                            p["mla_w_ukv"][o])
            x, h2 = _outproj_call(att, o_fw, o_bw, main, 4, p["hg_norm"][o], True,
                                  p["w_out_cd"], o, x, mod3, p["norm_ffn"][l])
        x = _ffn_split_call(x, h2, mod3, p["ffn_w_gate"], p["ffn_w_up"], p["ffn_conv"],
                            p["ffn_w_down"], l)
    return _final_norm_call(x, p["final_norm"])


def kernel(x_prompt, x_sample, c_prompt, c_sample, norm_mix, norm_ffn, w_ada, b_ada, w_in_ab, w_out_ab, na_rpb, dn_conv, dn_a_log, dn_dt_bias, dn_norm, w_in_cd, w_out_cd, hg_lower_bounds, hg_norm, mla_q_norm, mla_w_uq, mla_kv_norm, mla_w_ukv, ffn_w_gate, ffn_w_up, ffn_conv, ffn_w_down, final_norm):
    n_p, n_s = c_prompt.shape[0], c_sample.shape[0]
    rows = -(-(n_p + n_s) // 8) * 8
    c_all = jnp.concatenate([c_prompt, c_sample, jnp.zeros((rows - n_p - n_s, D_MODEL), F32)], axis=0)
    mod_all = _ada_call(c_all, w_ada, b_ada)
    mods_p = [mod_all[l, :n_p].reshape(n_p, 1, 6 * D_MODEL) for l in range(DEPTH)]
    mods_s = [mod_all[l, n_p:n_p + n_s].reshape(n_s, 1, 6 * D_MODEL) for l in range(DEPTH)]

    lb = jnp.cumsum(jax.nn.softmax(hg_lower_bounds.astype(F32), axis=1), axis=1)
    lb = lb - lb[:, :1]

    p = dict(
        norm_mix=norm_mix, norm_ffn=norm_ffn, na_rpb=na_rpb, dn_conv=dn_conv, dn_a_log=dn_a_log,
        dn_dt_bias=dn_dt_bias, dn_norm=dn_norm, hg_norm=hg_norm, mla_q_norm=mla_q_norm,
        mla_w_uq=mla_w_uq, mla_kv_norm=mla_kv_norm, mla_w_ukv=mla_w_ukv, ffn_conv=ffn_conv,
        final_norm=final_norm,
        w_in_ab_main=w_in_ab.astype(BF16),
        w_in_ab_side=jax.vmap(_pad_side)(w_in_ab[:, :, AB_MAIN:]),
        w_in_cd_main=w_in_cd.astype(BF16),
        w_in_cd_side=jnp.concatenate([w_in_cd[:, :, CD_MAIN:],
                                      w_in_cd[:, :, CD_MAIN:][:, :, ROPE_SWAP]], axis=-1).astype(BF16),
        w_out_ab=w_out_ab.astype(BF16), w_out_cd=w_out_cd.astype(BF16),
        ffn_w_gate=ffn_w_gate.astype(BF16), ffn_w_up=ffn_w_up.astype(BF16),
        ffn_w_down=ffn_w_down.astype(BF16),
    )
    y_prompt = _trunk(x_prompt, mods_p, lb, p)
    y_sample = _trunk(x_sample, mods_s, lb, p)
    return (y_prompt, y_sample)
```
